```python
import math
import jax
import jax.numpy as jnp
from jax import lax
import numpy as np

D_MODEL = 1024
BATCH = 8
SEQ = 8192
DEPTH = 1
DEC_BATCH = 16
DEC_SEQ = 16
PAST_LEN = 2048

CHUNK = 64
Q_BLOCK = 128
GDN_HEADS = 8
GDN_DK = 128
GDN_DV = 128
GDN_CONV = 4
GDN_CONV_CH = 2 * GDN_HEADS * GDN_DK + GDN_HEADS * GDN_DV
MLA_HEADS = 8
MLA_Q_LORA = 384
MLA_KV_LORA = 256
MLA_NOPE = 128
MLA_ROPE = 64
MLA_V = 128
MLA_SCALE = (MLA_NOPE + MLA_ROPE) ** -0.5
ROPE_THETA = 10000.0
MEM_TOKENS = 256
MEM_HEADS = 4
MEM_HEAD_DIM = 128
N_EXPERTS = 64
TOP_K = 8
N_GROUPS = 8
TOPK_GROUPS = 4
EXPERT_FF = 256
SHARED_FF = 256
ROUTED_SCALE = 2.5
MOE_BLOCK = 2048
DEEPNORM_ALPHA = (2.0 * DEPTH) ** 0.25
DEEPNORM_BETA = (8.0 * DEPTH) ** -0.25
IN_WIDTHS = (GDN_CONV_CH, GDN_HEADS * GDN_DV, GDN_HEADS, GDN_HEADS, MLA_Q_LORA, MLA_KV_LORA, MLA_ROPE, D_MODEL, D_MODEL)
IN_DIM = sum(IN_WIDTHS)
IN_SPLITS = tuple(int(s) for s in np.cumsum(IN_WIDTHS)[:-1])

kernel_name = 'hybrid_stream_gdn_mla_moe'


def layer_norm(x, g, b, eps=1e-5):
    xf = x.astype(jnp.float32)
    mu = jnp.mean(xf, axis=-1, keepdims=True)
    var = jnp.mean(jnp.square(xf - mu), axis=-1, keepdims=True)
    return ((xf - mu) * lax.rsqrt(var + eps) * g + b).astype(x.dtype)


def rms_norm(x, g, eps=1e-6):
    xf = x.astype(jnp.float32)
    return (xf * lax.rsqrt(jnp.mean(jnp.square(xf), axis=-1, keepdims=True) + eps) * g).astype(x.dtype)


def l2_normalize(x, eps=1e-6):
    xf = x.astype(jnp.float32)
    return (xf * lax.rsqrt(jnp.sum(jnp.square(xf), axis=-1, keepdims=True) + eps)).astype(x.dtype)


def rope_cos_sin(pos):
    inv_freq = jnp.power(ROPE_THETA, -jnp.arange(0, MLA_ROPE, 2, dtype=jnp.float32) / MLA_ROPE)
    ang = pos.astype(jnp.float32)[:, None] * inv_freq[None, :]
    return jnp.cos(ang), jnp.sin(ang)


def apply_rope(x, cos, sin):
    half = MLA_ROPE // 2
    shape = (x.shape[1],) + (1,) * (x.ndim - 3) + (half,)
    c, s = cos.reshape(shape), sin.reshape(shape)
    xf = x.astype(jnp.float32)
    x1, x2 = xf[..., :half], xf[..., half:]
    return jnp.concatenate([x1 * c - x2 * s, x2 * c + x1 * s], axis=-1).astype(x.dtype)


def causal_conv(x_new, buf, w):
    xp = jnp.concatenate([buf, x_new], axis=1)
    L = x_new.shape[1]
    y = w[0] * xp[:, 0:L]
    for i in range(1, GDN_CONV):
        y = y + w[i] * xp[:, i:i + L]
    return jax.nn.silu(y), xp[:, -(GDN_CONV - 1):]


def gdn_chunked(q, k, v, g, beta, s0, block):
    B, L, H, DK = q.shape
    DV = v.shape[-1]
    n = L // block
    f32 = jnp.float32

    def to_blocks(t):
        t = t.astype(f32).reshape((B, n, block) + t.shape[2:])
        return jnp.moveaxis(jnp.moveaxis(t, 1, 0), 3, 2)

    qb, kb, vb, gb, bb = to_blocks(q), to_blocks(k), to_blocks(v), to_blocks(g), to_blocks(beta)
    gc = jnp.cumsum(gb, axis=-1)
    idx = jnp.arange(block)
    incl = idx[:, None] >= idx[None, :]
    strict = idx[:, None] > idx[None, :]
    diff = gc[..., :, None] - gc[..., None, :]
    decay = jnp.where(incl, jnp.exp(jnp.where(incl, diff, 0.0)), 0.0)
    a_mat = jnp.where(strict, bb[..., :, None] * decay * jnp.einsum('nbhcd,nbhsd->nbhcs', kb, kb), 0.0)
    m_mat = a_mat + jnp.eye(block, dtype=f32)
    gamma = jnp.exp(gc)
    rhs = jnp.concatenate([bb[..., None] * vb, (bb * gamma)[..., None] * kb], axis=-1)
    sol = lax.linalg.triangular_solve(m_mat, rhs, left_side=True, lower=True, unit_diagonal=True)
    u_v, w_k = sol[..., :DV], sol[..., DV:]
    qk = decay * jnp.einsum('nbhcd,nbhsd->nbhcs', qb, kb)
    q_dec = gamma[..., None] * qb
    k_dec = jnp.exp(gc[..., -1:] - gc)[..., None] * kb
    g_last = gamma[..., -1]

    def step(s, xs):
        u_v_c, w_c, qk_c, q_c, k_c, gl_c = xs
        u = u_v_c - jnp.einsum('bhcd,bhde->bhce', w_c, s)
        o = jnp.einsum('bhcd,bhde->bhce', q_c, s) + jnp.einsum('bhcs,bhse->bhce', qk_c, u)
        s = gl_c[..., None, None] * s + jnp.einsum('bhcd,bhce->bhde', k_c, u)
        return s, o

    s_fin, o = lax.scan(step, s0.astype(f32), (u_v, w_k, qk, q_dec, k_dec, g_last))
    o = jnp.moveaxis(jnp.moveaxis(o, 2, 3), 0, 1).reshape(B, L, H, DV)
    return o, s_fin


def mla_attend(q_nope, q_rope, q_pos, k_nope, k_rope, v, k_pos):
    s = (jnp.einsum('bqhd,bkhd->bhqk', q_nope, k_nope)
         + jnp.einsum('bqhd,bkd->bhqk', q_rope, k_rope)).astype(jnp.float32) * MLA_SCALE
    visible = (k_pos // CHUNK)[None, :] <= (q_pos // CHUNK)[:, None]
    p = jax.nn.softmax(jnp.where(visible, s, -jnp.inf), axis=-1).astype(v.dtype)
    return jnp.einsum('bhqk,bkhd->bqhd', p, v)


def token_mixers(h, pos0, conv_buf, s0, ckv_past, kr_past, w_in, w_conv, a_log, dt_bias, gdn_norm_g,
                 w_proj_gdn, q_norm_g, w_uq, kv_norm_g, w_ukv, w_proj_mla, w_out):
    B, L, _ = h.shape
    proj = h @ w_in
    qkv, z, a_in, b_in, c_q, c_kv, k_r, gate_gdn, gate_mla = jnp.split(proj, IN_SPLITS, axis=-1)
    qkv, conv_new = causal_conv(qkv, conv_buf, w_conv)
    q, k, v = jnp.split(qkv, (GDN_HEADS * GDN_DK, 2 * GDN_HEADS * GDN_DK), axis=-1)
    q = l2_normalize(q.reshape(B, L, GDN_HEADS, GDN_DK)) * GDN_DK ** -0.5
    k = l2_normalize(k.reshape(B, L, GDN_HEADS, GDN_DK))
    v = v.reshape(B, L, GDN_HEADS, GDN_DV)
    g = -jnp.exp(a_log.astype(jnp.float32)) * jax.nn.softplus(a_in.astype(jnp.float32) + dt_bias.astype(jnp.float32))
    beta = jax.nn.sigmoid(b_in.astype(jnp.float32))
    block = CHUNK if L % CHUNK == 0 else L
    o_g, s_new = gdn_chunked(q, k, v, g, beta, s0, block)
    o_g = rms_norm(o_g, gdn_norm_g) * jax.nn.silu(z.reshape(B, L, GDN_HEADS, GDN_DV).astype(jnp.float32))
    branch_gdn = o_g.reshape(B, L, GDN_HEADS * GDN_DV).astype(h.dtype) @ w_proj_gdn
    q_pos = pos0 + jnp.arange(L)
    cos_q, sin_q = rope_cos_sin(q_pos)
    qm = (rms_norm(c_q, q_norm_g) @ w_uq).reshape(B, L, MLA_HEADS, MLA_NOPE + MLA_ROPE)
    q_nope = qm[..., :MLA_NOPE]
    q_rope = apply_rope(qm[..., MLA_NOPE:], cos_q, sin_q)
    c_kv = rms_norm(c_kv, kv_norm_g)
    k_r = apply_rope(k_r, cos_q, sin_q)
    if ckv_past is None:
        ckv_all, kr_all, k_pos = c_kv, k_r, q_pos
    else:
        ckv_all = jnp.concatenate([ckv_past, c_kv], axis=1)
        kr_all = jnp.concatenate([kr_past, k_r], axis=1)
        k_pos = jnp.arange(ckv_past.shape[1] + L)
    kv = (ckv_all @ w_ukv).reshape(B, ckv_all.shape[1], MLA_HEADS, MLA_NOPE + MLA_V)
    k_nope, v_m = kv[..., :MLA_NOPE], kv[..., MLA_NOPE:]
    if L > Q_BLOCK:
        nb = L // Q_BLOCK
        qn_b = jnp.moveaxis(q_nope.reshape(B, nb, Q_BLOCK, MLA_HEADS, MLA_NOPE), 1, 0)
        qr_b = jnp.moveaxis(q_rope.reshape(B, nb, Q_BLOCK, MLA_HEADS, MLA_ROPE), 1, 0)
        pos_b = q_pos.reshape(nb, Q_BLOCK)
        o_m = lax.map(lambda xs: mla_attend(xs[0], xs[1], xs[2], k_nope, kr_all, v_m, k_pos), (qn_b, qr_b, pos_b))
        o_m = jnp.moveaxis(o_m, 0, 1).reshape(B, L, MLA_HEADS, MLA_V)
    else:
        o_m = mla_attend(q_nope, q_rope, q_pos, k_nope, kr_all, v_m, k_pos)
    branch_mla = o_m.reshape(B, L, MLA_HEADS * MLA_V) @ w_proj_mla
    merged = jax.nn.sigmoid(gate_gdn) * branch_gdn + jax.nn.sigmoid(gate_mla) * branch_mla
    return merged @ w_out, conv_new, s_new.astype(s0.dtype), c_kv, k_r


def memory_attention(h, mem_k, mem_v, w_mq, w_mo):
    B, L, _ = h.shape
    q = (h @ w_mq).reshape(B, L, MEM_HEADS, MEM_HEAD_DIM)
    s = jnp.einsum('bqhd,bkhd->bhqk', q, mem_k).astype(jnp.float32) * MEM_HEAD_DIM ** -0.5
    p = jax.nn.softmax(s, axis=-1).astype(mem_v.dtype)
    o = jnp.einsum('bhqk,bkhd->bqhd', p, mem_v).reshape(B, L, MEM_HEADS * MEM_HEAD_DIM)
    return o @ w_mo


def moe_tokens(x, w_router, router_bias, w_gate, w_up, w_down, ws_gate, ws_up, ws_down):
    n = x.shape[0]
    scores = jax.nn.sigmoid((x @ w_router).astype(jnp.float32))
    sel = scores + router_bias.astype(jnp.float32)
    grp_score = lax.top_k(sel.reshape(n, N_GROUPS, N_EXPERTS // N_GROUPS), 2)[0].sum(-1)
    _, gidx = lax.top_k(grp_score, TOPK_GROUPS)
    gmask = jnp.any(gidx[:, :, None] == jnp.arange(N_GROUPS)[None, None, :], axis=1)
    emask = jnp.repeat(gmask, N_EXPERTS // N_GROUPS, axis=-1)
    _, eidx = lax.top_k(jnp.where(emask, sel, -jnp.inf), TOP_K)
    w = jnp.take_along_axis(scores, eidx, axis=-1)
    w = w / (jnp.sum(w, axis=-1, keepdims=True) + 1e-20) * ROUTED_SCALE
    gates = jnp.einsum('nk,nke->ne', w, jax.nn.one_hot(eidx, N_EXPERTS, dtype=jnp.float32)).astype(x.dtype)
    hid = jax.nn.silu(jnp.einsum('nd,edf->nef', x, w_gate)) * jnp.einsum('nd,edf->nef', x, w_up)
    routed = jnp.einsum('nef,efd->nd', hid * gates[:, :, None], w_down)
    shared = (jax.nn.silu(x @ ws_gate) * (x @ ws_up)) @ ws_down
    return routed + shared


def moe_ffn(h, w_router, router_bias, w_gate, w_up, w_down, ws_gate, ws_up, ws_down):
    x = h.reshape(-1, h.shape[-1])
    n = x.shape[0]
    blk = math.gcd(n, MOE_BLOCK)
    out = lax.map(lambda xb: moe_tokens(xb, w_router, router_bias, w_gate, w_up, w_down, ws_gate, ws_up, ws_down),
                  x.reshape(n // blk, blk, x.shape[-1]))
    return out.reshape(h.shape)


def post_sublayers(h, mix, mem_k, mem_v, ln1_g, ln1_b, w_mq, w_mo, ln2_g, ln2_b, w_router, router_bias,
                   w_gate, w_up, w_down, ws_gate, ws_up, ws_down, ln3_g, ln3_b):
    h = layer_norm(DEEPNORM_ALPHA * h + mix, ln1_g, ln1_b)
    h = layer_norm(DEEPNORM_ALPHA * h + memory_attention(h, mem_k, mem_v, w_mq, w_mo), ln2_g, ln2_b)
    h = layer_norm(DEEPNORM_ALPHA * h + moe_ffn(h, w_router, router_bias, w_gate, w_up, w_down,
                                                ws_gate, ws_up, ws_down), ln3_g, ln3_b)
    return h


def setup_inputs(seed: int = 0) -> dict:
    key = jax.random.key(seed)
    ks = iter(jax.random.split(key, 64))

    def nrm(shape, scale=1.0):
        return jax.random.normal(next(ks), shape, jnp.float32) * scale

    def gain(shape):
        return 1.0 + nrm(shape, 0.02)

    dl = DEPTH
    dt = jnp.exp(jax.random.uniform(next(ks), (dl, GDN_HEADS), jnp.float32, math.log(1e-3), math.log(1e-1)))
    a_log = jnp.log(jax.random.uniform(next(ks), (dl, GDN_HEADS), jnp.float32, 1.0, 16.0))
    return {
        'x_prompt': nrm((BATCH, SEQ, D_MODEL)),
        'x_sample': nrm((DEC_BATCH, DEC_SEQ, D_MODEL)),
        'mem_prompt': nrm((BATCH, MEM_TOKENS, D_MODEL)),
        'state_gdn_conv': nrm((dl, DEC_BATCH, GDN_CONV - 1, GDN_CONV_CH)),
        'state_gdn_ssm': nrm((dl, DEC_BATCH, GDN_HEADS, GDN_DK, GDN_DV), 0.1),
        'cache_mla_ckv': nrm((dl, DEC_BATCH, PAST_LEN, MLA_KV_LORA)),
        'cache_mla_krope': nrm((dl, DEC_BATCH, PAST_LEN, MLA_ROPE)),
        'cache_mem_k': nrm((dl, DEC_BATCH, MEM_TOKENS, MEM_HEADS, MEM_HEAD_DIM)),
        'cache_mem_v': nrm((dl, DEC_BATCH, MEM_TOKENS, MEM_HEADS, MEM_HEAD_DIM)),
        'ln_in_g': gain((D_MODEL,)),
        'ln_in_b': nrm((D_MODEL,), 0.02),
        'w_in': nrm((dl, D_MODEL, IN_DIM), D_MODEL ** -0.5),
        'w_conv': nrm((dl, GDN_CONV, GDN_CONV_CH), GDN_CONV ** -0.5),
        'a_log': a_log,
        'dt_bias': dt + jnp.log(-jnp.expm1(-dt)),
        'gdn_norm_g': gain((dl, GDN_DV)),
        'w_proj_gdn': nrm((dl, GDN_HEADS * GDN_DV, D_MODEL), (GDN_HEADS * GDN_DV) ** -0.5),
        'q_norm_g': gain((dl, MLA_Q_LORA)),
        'w_uq': nrm((dl, MLA_Q_LORA, MLA_HEADS * (MLA_NOPE + MLA_ROPE)), MLA_Q_LORA ** -0.5),
        'kv_norm_g': gain((dl, MLA_KV_LORA)),
        'w_ukv': nrm((dl, MLA_KV_LORA, MLA_HEADS * (MLA_NOPE + MLA_V)), MLA_KV_LORA ** -0.5),
        'w_proj_mla': nrm((dl, MLA_HEADS * MLA_V, D_MODEL), (MLA_HEADS * MLA_V) ** -0.5),
        'w_out': nrm((dl, D_MODEL, D_MODEL), D_MODEL ** -0.5 * DEEPNORM_BETA),
        'ln1_g': gain((dl, D_MODEL)),
        'ln1_b': nrm((dl, D_MODEL), 0.02),
        'w_mq': nrm((dl, D_MODEL, MEM_HEADS * MEM_HEAD_DIM), D_MODEL ** -0.5),
        'w_mk': nrm((dl, D_MODEL, MEM_HEADS * MEM_HEAD_DIM), D_MODEL ** -0.5),
        'w_mv': nrm((dl, D_MODEL, MEM_HEADS * MEM_HEAD_DIM), D_MODEL ** -0.5),
        'w_mo': nrm((dl, MEM_HEADS * MEM_HEAD_DIM, D_MODEL), (MEM_HEADS * MEM_HEAD_DIM) ** -0.5 * DEEPNORM_BETA),
        'ln2_g': gain((dl, D_MODEL)),
        'ln2_b': nrm((dl, D_MODEL), 0.02),
        'w_router': nrm((dl, D_MODEL, N_EXPERTS), D_MODEL ** -0.5),
        'router_bias': nrm((dl, N_EXPERTS), 0.01),
        'w_gate': nrm((dl, N_EXPERTS, D_MODEL, EXPERT_FF), D_MODEL ** -0.5),
        'w_up': nrm((dl, N_EXPERTS, D_MODEL, EXPERT_FF), D_MODEL ** -0.5),
        'w_down': nrm((dl, N_EXPERTS, EXPERT_FF, D_MODEL), EXPERT_FF ** -0.5 * DEEPNORM_BETA),
        'ws_gate': nrm((dl, D_MODEL, SHARED_FF), D_MODEL ** -0.5),
        'ws_up': nrm((dl, D_MODEL, SHARED_FF), D_MODEL ** -0.5),
        'ws_down': nrm((dl, SHARED_FF, D_MODEL), SHARED_FF ** -0.5 * DEEPNORM_BETA),
        'ln3_g': gain((dl, D_MODEL)),
        'ln3_b': nrm((dl, D_MODEL), 0.02),
    }


def reference(x_prompt, x_sample, mem_prompt, state_gdn_conv, state_gdn_ssm, cache_mla_ckv, cache_mla_krope,
              cache_mem_k, cache_mem_v, ln_in_g, ln_in_b, w_in, w_conv, a_log, dt_bias, gdn_norm_g, w_proj_gdn,
              q_norm_g, w_uq, kv_norm_g, w_ukv, w_proj_mla, w_out, ln1_g, ln1_b, w_mq, w_mk, w_mv, w_mo,
              ln2_g, ln2_b, w_router, router_bias, w_gate, w_up, w_down, ws_gate, ws_up, ws_down, ln3_g, ln3_b):
    hp = layer_norm(x_prompt, ln_in_g, ln_in_b)
    hs = layer_norm(x_sample, ln_in_g, ln_in_b)
    B = x_prompt.shape[0]
    past = cache_mla_ckv.shape[2]
    conv_p_l, ssm_p_l, ckv_p_l, kr_p_l, mk_p_l, mv_p_l = [], [], [], [], [], []
    conv_s_l, ssm_s_l, ckv_s_l, kr_s_l = [], [], [], []
    for l in range(DEPTH):
        mix_w = (w_in[l], w_conv[l], a_log[l], dt_bias[l], gdn_norm_g[l], w_proj_gdn[l], q_norm_g[l], w_uq[l],
                 kv_norm_g[l], w_ukv[l], w_proj_mla[l], w_out[l])
        post_w = (ln1_g[l], ln1_b[l], w_mq[l], w_mo[l], ln2_g[l], ln2_b[l], w_router[l], router_bias[l],
                  w_gate[l], w_up[l], w_down[l], ws_gate[l], ws_up[l], ws_down[l], ln3_g[l], ln3_b[l])
        mem_k = (mem_prompt @ w_mk[l]).reshape(B, -1, MEM_HEADS, MEM_HEAD_DIM)
        mem_v = (mem_prompt @ w_mv[l]).reshape(B, -1, MEM_HEADS, MEM_HEAD_DIM)
        conv0 = jnp.zeros((B, GDN_CONV - 1, GDN_CONV_CH), hp.dtype)
        s0 = jnp.zeros((B, GDN_HEADS, GDN_DK, GDN_DV), hp.dtype)
        mix, conv_p, ssm_p, ckv_p, kr_p = token_mixers(hp, 0, conv0, s0, None, None, *mix_w)
        hp = post_sublayers(hp, mix, mem_k, mem_v, *post_w)
        conv_p_l.append(conv_p)
        ssm_p_l.append(ssm_p)
        ckv_p_l.append(ckv_p)
        kr_p_l.append(kr_p)
        mk_p_l.append(mem_k)
        mv_p_l.append(mem_v)
        mix, conv_s, ssm_s, ckv_s, kr_s = token_mixers(hs, past, state_gdn_conv[l], state_gdn_ssm[l],
                                                       cache_mla_ckv[l], cache_mla_krope[l], *mix_w)
        hs = post_sublayers(hs, mix, cache_mem_k[l], cache_mem_v[l], *post_w)
        conv_s_l.append(conv_s)
        ssm_s_l.append(ssm_s)
        ckv_s_l.append(ckv_s)
        kr_s_l.append(kr_s)
    new_conv_p = jnp.stack(conv_p_l, axis=0)
    new_ssm_p = jnp.stack(ssm_p_l, axis=0)
    new_ckv_p = jnp.stack(ckv_p_l, axis=0)
    new_krope_p = jnp.stack(kr_p_l, axis=0)
    new_mem_k_p = jnp.stack(mk_p_l, axis=0)
    new_mem_v_p = jnp.stack(mv_p_l, axis=0)
    new_conv_s = jnp.stack(conv_s_l, axis=0)
    new_ssm_s = jnp.stack(ssm_s_l, axis=0)
    new_ckv_s = jnp.stack(ckv_s_l, axis=0)
    new_krope_s = jnp.stack(kr_s_l, axis=0)
    return (hp, hs, new_conv_p, new_ssm_p, new_ckv_p, new_krope_p, new_mem_k_p, new_mem_v_p,
            new_conv_s, new_ssm_s, new_ckv_s, new_krope_s)
```

```python
import functools
import math

import jax
import jax.numpy as jnp
from jax import lax
from jax.experimental import pallas as pl
from jax.experimental.pallas import tpu as pltpu

F32 = jnp.float32
BF16 = jnp.bfloat16

D_MODEL = 1024
CHUNK = 64
GDN_HEADS = 8
GDN_DK = 128
GDN_DV = 128
GDN_CONV = 4
GDN_QK = GDN_HEADS * GDN_DK
GDN_CONV_CH = 2 * GDN_QK + GDN_HEADS * GDN_DV
MLA_HEADS = 8
MLA_Q_LORA = 384
MLA_KV_LORA = 256
MLA_NOPE = 128
MLA_ROPE = 64
MLA_V = 128
MLA_SCALE = (MLA_NOPE + MLA_ROPE) ** -0.5
MLA_QK_PAD = 256
ROPE_THETA = 10000.0
MEM_HEADS = 4
MEM_HEAD_DIM = 128
N_EXPERTS = 64
TOP_K = 8
N_GROUPS = 8
GROUP_SIZE = N_EXPERTS // N_GROUPS
TOPK_GROUPS = 4
EXPERT_FF = 256
SHARED_FF = 256
ROUTED_SCALE = 2.5
DEPTH = 1
DEEPNORM_ALPHA = (2.0 * DEPTH) ** 0.25
IN_WIDTHS = (GDN_CONV_CH, GDN_HEADS * GDN_DV, GDN_HEADS, GDN_HEADS, MLA_Q_LORA, MLA_KV_LORA, MLA_ROPE,
             D_MODEL, D_MODEL)

VMEM_LIMIT_BYTES = 56 * 1024 * 1024


def _dot(a, b, precision=None):
    return jnp.dot(a, b, preferred_element_type=F32, precision=precision)


def _dot_nt(a, b, precision=None):
    return lax.dot_general(a, b, (((1,), (1,)), ((), ())), preferred_element_type=F32, precision=precision)


def _dot_tn(a, b):
    return lax.dot_general(a, b, (((0,), (0,)), ((), ())), preferred_element_type=F32)


def _layer_norm(x, g, b, eps=1e-5):
    mu = jnp.mean(x, axis=-1, keepdims=True)
    xc = x - mu
    var = jnp.mean(xc * xc, axis=-1, keepdims=True)
    return xc * lax.rsqrt(var + eps) * g + b


def _rms_norm(x, g, eps=1e-6):
    return x * lax.rsqrt(jnp.mean(x * x, axis=-1, keepdims=True) + eps) * g


def _sigmoid(x):
    return 1.0 / (1.0 + jnp.exp(-x))


def _silu(x):
    return x * _sigmoid(x)


def _softplus(x):
    return jnp.maximum(x, 0.0) + jnp.log1p(jnp.exp(-jnp.abs(x)))


def _const_spec(shape):
    nd = len(shape)
    return pl.BlockSpec(shape, lambda *_: (0,) * nd, pipeline_mode=pl.Buffered(1))


def _inproj_kernel(x_ref, lng_ref, lnb_ref, wqkv_ref, wz_ref, wgate_ref, wcq_ref, wckv_ref, wkr_ref, wab_ref,
                   wabt_ref, convw_ref, conv0_ref, arow_ref, acol_ref, qng_ref, wuq_ref, wuqs_ref, kvng_ref,
                   wukv_ref, t1_ref, t2_ref,
                   qkv_o, z_o, gate_o, qp_o, kn_o, krp_o, v_o, ckv_o, kr_o, conv_o, gbc_o, gbr_o,
                   cbuf):
    tl = x_ref.shape[0]
    li = pl.program_id(1)

    h = _layer_norm(x_ref[...], lng_ref[...], lnb_ref[...])
    hb = h.astype(BF16)

    @pl.when(li == 0)
    def _():
        cbuf[0:8, :] = conv0_ref[...]

    cbuf[8:8 + tl, :] = _dot(hb, wqkv_ref[...])
    conv_o[...] = cbuf[tl + 5:tl + 8, :]
    for part in range(3):
        cols = slice(part * GDN_QK, (part + 1) * GDN_QK)
        y = convw_ref[0:1, cols] * cbuf[5:5 + tl, cols]
        for i in range(1, GDN_CONV):
            y = y + convw_ref[i:i + 1, cols] * cbuf[5 + i:5 + i + tl, cols]
        y = _silu(y)
        if part < 2:
            for hh in range(GDN_HEADS):
                hc = slice(hh * GDN_DK, (hh + 1) * GDN_DK)
                yh = y[:, hc]
                r = lax.rsqrt(jnp.sum(yh * yh, axis=-1, keepdims=True) + 1e-6)
                yh = yh * r
                if part == 0:
                    yh = yh * (GDN_DK ** -0.5)
                qkv_o[:, part * GDN_QK + hh * GDN_DK:part * GDN_QK + (hh + 1) * GDN_DK] = yh.astype(BF16)
        else:
            qkv_o[:, cols] = y.astype(BF16)
    cbuf[0:8, :] = cbuf[tl:tl + 8, :]

    z_o[...] = _dot(hb, wz_ref[...]).astype(BF16)
    gate_o[...] = _dot(hb, wgate_ref[...]).astype(BF16)

    ab = _dot(hb, wab_ref[...])
    arow = arow_ref[...]
    g_c = arow[0:1, :] * _softplus(ab[:, 0:GDN_HEADS] + arow[1:2, :])
    b_c = _sigmoid(ab[:, GDN_HEADS:2 * GDN_HEADS])
    gbc_o[:, 0:GDN_HEADS] = g_c
    gbc_o[:, GDN_HEADS:2 * GDN_HEADS] = b_c
    abt = _dot_nt(wabt_ref[...], hb)
    acol = acol_ref[...]
    gbr_o[0:GDN_HEADS, :] = acol[:, 0:1] * _softplus(abt[0:GDN_HEADS, :] + acol[:, 1:2])
    gbr_o[GDN_HEADS:2 * GDN_HEADS, :] = _sigmoid(abt[GDN_HEADS:2 * GDN_HEADS, :])

    t1 = t1_ref[...]
    t2 = t2_ref[...]

    cq = _rms_norm(_dot(hb, wcq_ref[...]), qng_ref[...]).astype(BF16)
    qm = _dot(cq, wuq_ref[...])
    qs = _dot(cq, wuqs_ref[...])
    for hh in range(MLA_HEADS):
        base = hh * MLA_QK_PAD
        qp_o[:, base:base + MLA_NOPE] = (qm[:, base:base + MLA_NOPE] * MLA_SCALE).astype(BF16)
        hi = qm[:, base + MLA_NOPE:base + MLA_QK_PAD] * t1 + qs[:, hh * 128:(hh + 1) * 128] * t2
        qp_o[:, base + MLA_NOPE:base + MLA_QK_PAD] = (hi * MLA_SCALE).astype(BF16)

    ckv = _rms_norm(_dot(hb, wckv_ref[...]), kvng_ref[...])
    ckv_o[...] = ckv
    kv = _dot(ckv.astype(BF16), wukv_ref[...])
    kn_o[...] = kv[:, 0:MLA_HEADS * MLA_NOPE].astype(BF16)
    v_o[...] = kv[:, MLA_HEADS * MLA_NOPE:].astype(BF16)
    krm = _dot(hb, wkr_ref[...])
    krp = krm[:, 0:128] * t1 + krm[:, 128:256] * t2
    kr_o[...] = krp[:, 0:MLA_ROPE]
    krp_o[...] = krp.astype(BF16)


def _swap_halves(w):
    half = w.shape[-1] // 2
    return jnp.concatenate([w[..., half:], w[..., :half]], axis=-1)


def _rope_tables(pos0, length):
    inv_freq = jnp.power(ROPE_THETA, -jnp.arange(0, MLA_ROPE, 2, dtype=F32) / MLA_ROPE)
    ang = (pos0 + jnp.arange(length)).astype(F32)[:, None] * inv_freq[None, :]
    c, s = jnp.cos(ang), jnp.sin(ang)
    zeros = jnp.zeros((length, 128 - MLA_ROPE), F32)
    return jnp.concatenate([c, c, zeros], axis=-1), jnp.concatenate([-s, s, zeros], axis=-1)


def _prep_mixer_weights(w_in, w_conv, a_log, dt_bias, q_norm_g, w_uq, kv_norm_g, w_ukv):
    splits = []
    off = 0
    for wd in IN_WIDTHS:
        splits.append(w_in[:, off:off + wd])
        off += wd
    w_qkv, w_z, w_a, w_b, w_cq, w_ckv, w_kr, w_gg, w_gm = splits
    z64 = jnp.zeros((D_MODEL, 128 - MLA_ROPE), F32)
    w_krg = jnp.concatenate([w_kr, z64, _swap_halves(w_kr), z64], axis=-1)
    w_ab = jnp.concatenate([w_a, w_b], axis=-1)
    wq = w_uq.reshape(MLA_Q_LORA, MLA_HEADS, MLA_NOPE + MLA_ROPE)
    nope, rope = wq[:, :, :MLA_NOPE], wq[:, :, MLA_NOPE:]
    zq = jnp.zeros((MLA_Q_LORA, MLA_HEADS, MLA_QK_PAD - MLA_NOPE - MLA_ROPE), F32)
    wq_main = jnp.concatenate([nope, rope, zq], axis=-1).reshape(MLA_Q_LORA, MLA_HEADS * MLA_QK_PAD)
    wq_swap = jnp.concatenate([_swap_halves(rope), zq], axis=-1).reshape(MLA_Q_LORA, MLA_HEADS * 128)
    wkv = w_ukv.reshape(MLA_KV_LORA, MLA_HEADS, MLA_NOPE + MLA_V)
    w_kv = jnp.concatenate([wkv[:, :, :MLA_NOPE].reshape(MLA_KV_LORA, -1),
                            wkv[:, :, MLA_NOPE:].reshape(MLA_KV_LORA, -1)], axis=-1)
    neg_a = -jnp.exp(a_log.astype(F32))
    arow = jnp.stack([neg_a, dt_bias.astype(F32)], axis=0)
    return dict(
        w_qkv=w_qkv.astype(BF16), w_z=w_z.astype(BF16),
        w_gate=jnp.concatenate([w_gg, w_gm], axis=-1).astype(BF16),
        w_cq=w_cq.astype(BF16), w_ckv=w_ckv.astype(BF16), w_krg=w_krg.astype(BF16),
        w_ab=w_ab.astype(BF16), w_abt=w_ab.T.astype(BF16),
        w_conv=w_conv.astype(F32), arow=arow, acol=arow.T,
        q_norm_g=q_norm_g.reshape(1, -1).astype(F32), wq_main=wq_main.astype(BF16), wq_swap=wq_swap.astype(BF16),
        kv_norm_g=kv_norm_g.reshape(1, -1).astype(F32), w_kv=w_kv.astype(BF16))


def _inproj(x, ln_g, ln_b, mw, conv_state, pos0, tl):
    bsz, length, _ = x.shape
    nl = length // tl
    t1, t2 = _rope_tables(pos0, length)
    conv0 = jnp.concatenate([jnp.zeros((bsz, 8 - (GDN_CONV - 1), GDN_CONV_CH), F32), conv_state.astype(F32)], axis=1)

    def rows(width, dtype):
        return (jax.ShapeDtypeStruct((bsz, length, width), dtype),
                pl.BlockSpec((None, tl, width), lambda b, l: (b, l, 0)))

    outs = [rows(GDN_CONV_CH, BF16), rows(GDN_HEADS * GDN_DV, BF16), rows(2 * D_MODEL, BF16),
            rows(MLA_HEADS * MLA_QK_PAD, BF16), rows(MLA_HEADS * MLA_NOPE, BF16), rows(128, BF16),
            rows(MLA_HEADS * MLA_V, BF16), rows(MLA_KV_LORA, F32), rows(MLA_ROPE, F32),
            (jax.ShapeDtypeStruct((bsz, GDN_CONV - 1, GDN_CONV_CH), F32),
             pl.BlockSpec((None, GDN_CONV - 1, GDN_CONV_CH), lambda b, l: (b, 0, 0))),
            rows(2 * GDN_HEADS, F32),
            (jax.ShapeDtypeStruct((bsz, 2 * GDN_HEADS, length), F32),
             pl.BlockSpec((None, 2 * GDN_HEADS, tl), lambda b, l: (b, 0, l)))]
    consts = [ln_g.reshape(1, -1), ln_b.reshape(1, -1), mw['w_qkv'], mw['w_z'], mw['w_gate'], mw['w_cq'],
              mw['w_ckv'], mw['w_krg'], mw['w_ab'], mw['w_abt'], mw['w_conv']]
    consts2 = [mw['arow'], mw['acol'], mw['q_norm_g'], mw['wq_main'], mw['wq_swap'], mw['kv_norm_g'], mw['w_kv']]
    in_specs = ([pl.BlockSpec((None, tl, D_MODEL), lambda b, l: (b, l, 0))]
                + [_const_spec(c.shape) for c in consts]
                + [pl.BlockSpec((None, 8, GDN_CONV_CH), lambda b, l: (b, 0, 0))]
                + [_const_spec(c.shape) for c in consts2]
                + [pl.BlockSpec((tl, 128), lambda b, l: (l, 0)), pl.BlockSpec((tl, 128), lambda b, l: (l, 0))])
    return pl.pallas_call(
        _inproj_kernel,
        grid=(bsz, nl),
        in_specs=in_specs,
        out_specs=[o[1] for o in outs],
        out_shape=[o[0] for o in outs],
        scratch_shapes=[pltpu.VMEM((tl + 8, GDN_CONV_CH), F32)],
        compiler_params=pltpu.CompilerParams(dimension_semantics=("arbitrary", "arbitrary"),
                                             vmem_limit_bytes=VMEM_LIMIT_BYTES),
        name="inproj",
    )(x, *consts, conv0, *consts2, t1, t2)


def _gdn_kernel(qkv_ref, gbc_ref, gbr_ref, z_ref, s0_ref, ng_ref, og_o, s_o, s_scr, *, chunk):
    tg = qkv_ref.shape[0]
    nchunks = tg // chunk
    li = pl.program_id(1)

    @pl.when(li == 0)
    def _():
        s_scr[...] = s0_ref[...]

    ri = lax.broadcasted_iota(jnp.int32, (chunk, chunk), 0)
    ci = lax.broadcasted_iota(jnp.int32, (chunk, chunk), 1)
    incl = ri >= ci
    strict = ri > ci
    tri = jnp.where(incl, 1.0, 0.0).astype(F32)
    eye = jnp.where(ri == ci, 1.0, 0.0).astype(F32)
    ng = ng_ref[...]
    n_sq = int(math.log2(chunk)) - 1

    def chunk_body(c, carry):
        r0 = pl.multiple_of(c * chunk, chunk)
        rows = pl.ds(r0, chunk)
        gbc = gbc_ref[rows, :]
        gbr = gbr_ref[c]
        gc_col = _dot(tri, gbc, precision=lax.Precision.HIGHEST)
        gc_row = _dot_nt(gbr, tri, precision=lax.Precision.HIGHEST)
        for hh in range(GDN_HEADS):
            q = qkv_ref[rows, hh * GDN_DK:(hh + 1) * GDN_DK]
            k = qkv_ref[rows, GDN_QK + hh * GDN_DK:GDN_QK + (hh + 1) * GDN_DK]
            v = qkv_ref[rows, 2 * GDN_QK + hh * GDN_DV:2 * GDN_QK + (hh + 1) * GDN_DV]
            gcc = gc_col[:, hh:hh + 1]
            gcr = gc_row[hh:hh + 1, :]
            beta = gbc[:, GDN_HEADS + hh:GDN_HEADS + hh + 1]
            decay = jnp.where(incl, jnp.exp(jnp.where(incl, gcc - gcr, 0.0)), 0.0)
            kk = _dot_nt(k, k)
            qk = decay * _dot_nt(q, k)
            a_mat = jnp.where(strict, beta * decay * kk, 0.0)
            t_inv = eye - a_mat
            xp = a_mat
            for _ in range(n_sq):
                xb = xp.astype(BF16)
                xp = _dot(xb, xb)
                t_inv = t_inv + _dot(t_inv.astype(BF16), xp.astype(BF16))
            kf = k.astype(F32)
            gamma = jnp.exp(gcc)
            rhs = jnp.concatenate([beta * v.astype(F32), (beta * gamma) * kf], axis=-1).astype(BF16)
            sol = _dot(t_inv.astype(BF16), rhs)
            u_v, w_k = sol[:, :GDN_DV], sol[:, GDN_DV:]
            g_end = gc_col[chunk - 1:chunk, hh:hh + 1]
            q_dec = (gamma * q.astype(F32)).astype(BF16)
            k_dec = (jnp.exp(g_end - gcc) * kf).astype(BF16)
            s = s_scr[hh]
            sb = s.astype(BF16)
            u = u_v - _dot(w_k.astype(BF16), sb)
            ub = u.astype(BF16)
            o = _dot(q_dec, sb) + _dot(qk.astype(BF16), ub)
            s_scr[hh] = jnp.exp(g_end) * s + _dot_tn(k_dec, ub)
            zz = z_ref[rows, hh * GDN_DV:(hh + 1) * GDN_DV].astype(F32)
            og_o[rows, hh * GDN_DV:(hh + 1) * GDN_DV] = (_rms_norm(o, ng) * _silu(zz)).astype(BF16)
        return carry

    lax.fori_loop(0, nchunks, chunk_body, 0)
    s_o[...] = s_scr[...]


def _gdn(qkv, gbc, gbr, z, s0, norm_g, chunk, tg):
    bsz, length, _ = qkv.shape
    nl = length // tg
    st_shape = (GDN_HEADS, GDN_DK, GDN_DV)
    gbr = gbr.reshape(bsz, 2 * GDN_HEADS, length // chunk, chunk).transpose(0, 2, 1, 3)
    return pl.pallas_call(
        functools.partial(_gdn_kernel, chunk=chunk),
        grid=(bsz, nl),
        in_specs=[pl.BlockSpec((None, tg, GDN_CONV_CH), lambda b, l: (b, l, 0)),
                  pl.BlockSpec((None, tg, 2 * GDN_HEADS), lambda b, l: (b, l, 0)),
                  pl.BlockSpec((None, tg // chunk, 2 * GDN_HEADS, chunk), lambda b, l: (b, l, 0, 0)),
                  pl.BlockSpec((None, tg, GDN_HEADS * GDN_DV), lambda b, l: (b, l, 0)),
                  pl.BlockSpec((None,) + st_shape, lambda b, l: (b, 0, 0, 0)),
                  _const_spec((1, GDN_DV))],
        out_specs=[pl.BlockSpec((None, tg, GDN_HEADS * GDN_DV), lambda b, l: (b, l, 0)),
                   pl.BlockSpec((None,) + st_shape, lambda b, l: (b, 0, 0, 0))],
        out_shape=[jax.ShapeDtypeStruct((bsz, length, GDN_HEADS * GDN_DV), BF16),
                   jax.ShapeDtypeStruct((bsz,) + st_shape, F32)],
        scratch_shapes=[pltpu.VMEM(st_shape, F32)],
        compiler_params=pltpu.CompilerParams(dimension_semantics=("arbitrary", "arbitrary"),
                                             vmem_limit_bytes=VMEM_LIMIT_BYTES),
        name="gdn",
    )(qkv, gbc, gbr, z, s0.astype(F32), norm_g.reshape(1, -1).astype(F32))


def _flash_kernel(q_ref, kn_ref, krp_ref, v_ref, o_ref, kcat, *, tq):
    qi = pl.program_id(2)

    @pl.when(qi == 0)
    def _():
        kcat[:, 0:MLA_NOPE] = kn_ref[...]
        kcat[:, MLA_NOPE:MLA_QK_PAD] = krp_ref[...]

    q = q_ref[...]

    def tile(ki, carry, masked):
        m, l, acc = carry
        rows = pl.ds(pl.multiple_of(ki * tq, tq), tq)
        s = _dot_nt(q, kcat[rows, :])
        if masked:
            ri = lax.broadcasted_iota(jnp.int32, (tq, tq), 0) // CHUNK
            ci = lax.broadcasted_iota(jnp.int32, (tq, tq), 1) // CHUNK
            s = jnp.where(ci <= ri, s, -jnp.inf)
        m_new = jnp.maximum(m, jnp.max(s, axis=-1, keepdims=True))
        p = jnp.exp(s - m_new)
        alpha = jnp.exp(m - m_new)
        l = alpha * l + jnp.sum(p, axis=-1, keepdims=True)
        acc = alpha * acc + _dot(p.astype(BF16), v_ref[rows, :])
        return m_new, l, acc

    init = (jnp.full((tq, 1), -jnp.inf, F32), jnp.zeros((tq, 1), F32), jnp.zeros((tq, MLA_V), F32))
    carry = lax.fori_loop(0, qi, lambda ki, c: tile(ki, c, False), init)
    m, l, acc = tile(qi, carry, True)
    o_ref[...] = (acc / l).astype(BF16)


def _flash_attention(qp, kn, krp, v, tq):
    bsz, length, _ = qp.shape
    nq = length // tq
    return pl.pallas_call(
        functools.partial(_flash_kernel, tq=tq),
        grid=(bsz, MLA_HEADS, nq),
        in_specs=[pl.BlockSpec((None, tq, MLA_QK_PAD), lambda b, h, i: (b, i, h)),
                  pl.BlockSpec((None, length, MLA_NOPE), lambda b, h, i: (b, 0, h)),
                  pl.BlockSpec((None, length, 128), lambda b, h, i: (b, 0, 0)),
                  pl.BlockSpec((None, length, MLA_V), lambda b, h, i: (b, 0, h))],
        out_specs=pl.BlockSpec((None, tq, MLA_V), lambda b, h, i: (b, i, h)),
        out_shape=jax.ShapeDtypeStruct((bsz, length, MLA_HEADS * MLA_V), BF16),
        scratch_shapes=[pltpu.VMEM((length, MLA_QK_PAD), BF16)],
        compiler_params=pltpu.CompilerParams(dimension_semantics=("arbitrary", "arbitrary", "arbitrary"),
                                             vmem_limit_bytes=VMEM_LIMIT_BYTES),
        name="mla_flash",
    )(qp, kn, krp, v)


def _kvup_kernel(c_ref, w_ref, kn_o, v_o):
    kv = _dot(c_ref[...].astype(BF16), w_ref[...])
    kn_o[...] = kv[:, 0:MLA_HEADS * MLA_NOPE].astype(BF16)
    v_o[...] = kv[:, MLA_HEADS * MLA_NOPE:].astype(BF16)


def _kvup(ckv, w_kv, tr):
    bsz, length, _ = ckv.shape
    wide = MLA_HEADS * MLA_NOPE
    return pl.pallas_call(
        _kvup_kernel,
        grid=(bsz, length // tr),
        in_specs=[pl.BlockSpec((None, tr, MLA_KV_LORA), lambda b, l: (b, l, 0)), _const_spec(w_kv.shape)],
        out_specs=[pl.BlockSpec((None, tr, wide), lambda b, l: (b, l, 0)),
                   pl.BlockSpec((None, tr, wide), lambda b, l: (b, l, 0))],
        out_shape=[jax.ShapeDtypeStruct((bsz, length, wide), BF16), jax.ShapeDtypeStruct((bsz, length, wide), BF16)],
        compiler_params=pltpu.CompilerParams(dimension_semantics=("arbitrary", "arbitrary"),
                                             vmem_limit_bytes=VMEM_LIMIT_BYTES),
        name="kv_up_cached",
    )(ckv, w_kv)


def _cached_attn_kernel(q_ref, knp_ref, krp_ref, vp_ref, knn_ref, krn_ref, vn_ref, o_ref, *, past):
    lq = q_ref.shape[0]
    q_chunk = (past + lax.broadcasted_iota(jnp.int32, (lq, past), 0)) // CHUNK
    vis_p = lax.broadcasted_iota(jnp.int32, (lq, past), 1) // CHUNK <= q_chunk
    q_chunk_n = (past + lax.broadcasted_iota(jnp.int32, (lq, lq), 0)) // CHUNK
    vis_n = (past + lax.broadcasted_iota(jnp.int32, (lq, lq), 1)) // CHUNK <= q_chunk_n
    kr_past = krp_ref[...].astype(BF16)
    kr_new = krn_ref[...]
    for hh in range(MLA_HEADS):
        base = hh * MLA_QK_PAD
        qn = q_ref[:, base:base + MLA_NOPE]
        qr = q_ref[:, base + MLA_NOPE:base + MLA_QK_PAD]
        hc = slice(hh * MLA_NOPE, (hh + 1) * MLA_NOPE)
        s_p = _dot_nt(qn, knp_ref[:, hc]) + _dot_nt(qr[:, 0:MLA_ROPE], kr_past)
        s_n = _dot_nt(qn, knn_ref[:, hc]) + _dot_nt(qr, kr_new)
        s_p = jnp.where(vis_p, s_p, -jnp.inf)
        s_n = jnp.where(vis_n, s_n, -jnp.inf)
        m = jnp.maximum(jnp.max(s_p, axis=-1, keepdims=True), jnp.max(s_n, axis=-1, keepdims=True))
        p_p = jnp.exp(s_p - m)
        p_n = jnp.exp(s_n - m)
        l = jnp.sum(p_p, axis=-1, keepdims=True) + jnp.sum(p_n, axis=-1, keepdims=True)
        vc = slice(hh * MLA_V, (hh + 1) * MLA_V)
        o = _dot(p_p.astype(BF16), vp_ref[:, vc]) + _dot(p_n.astype(BF16), vn_ref[:, vc])
        o_ref[:, vc] = (o / l).astype(BF16)


def _cached_attention(qp, kn_past, kr_past, v_past, kn_new, krp_new, v_new):
    bsz, lq, _ = qp.shape
    past = kn_past.shape[1]
    wide = MLA_HEADS * MLA_NOPE

    def per_b(rows, width):
        return pl.BlockSpec((None, rows, width), lambda b: (b, 0, 0))

    return pl.pallas_call(
        functools.partial(_cached_attn_kernel, past=past),
        grid=(bsz,),
        in_specs=[per_b(lq, MLA_HEADS * MLA_QK_PAD), per_b(past, wide), per_b(past, MLA_ROPE), per_b(past, wide),
                  per_b(lq, wide), per_b(lq, 128), per_b(lq, wide)],
        out_specs=per_b(lq, MLA_HEADS * MLA_V),
        out_shape=jax.ShapeDtypeStruct((bsz, lq, MLA_HEADS * MLA_V), BF16),
        compiler_params=pltpu.CompilerParams(dimension_semantics=("arbitrary",),
                                             vmem_limit_bytes=VMEM_LIMIT_BYTES),
        name="mla_cached",
    )(qp, kn_past, kr_past, v_past, kn_new, krp_new, v_new)


def _memproj_kernel(m_ref, wk_ref, wv_ref, k_o, v_o):
    mb = m_ref[...].astype(BF16)
    k_o[...] = _dot(mb, wk_ref[...])
    v_o[...] = _dot(mb, wv_ref[...])


def _memproj(mem, w_mk, w_mv):
    bsz, tokens, _ = mem.shape
    wide = MEM_HEADS * MEM_HEAD_DIM
    return pl.pallas_call(
        _memproj_kernel,
        grid=(bsz,),
        in_specs=[pl.BlockSpec((None, tokens, D_MODEL), lambda b: (b, 0, 0)),
                  _const_spec(w_mk.shape), _const_spec(w_mv.shape)],
        out_specs=[pl.BlockSpec((None, tokens, wide), lambda b: (b, 0, 0)),
                   pl.BlockSpec((None, tokens, wide), lambda b: (b, 0, 0))],
        out_shape=[jax.ShapeDtypeStruct((bsz, tokens, wide), F32), jax.ShapeDtypeStruct((bsz, tokens, wide), F32)],
        compiler_params=pltpu.CompilerParams(dimension_semantics=("arbitrary",),
                                             vmem_limit_bytes=VMEM_LIMIT_BYTES),
        name="mem_proj",
    )(mem, w_mk.astype(BF16), w_mv.astype(BF16))


def _route(logits_t, bias_col):
    scores = _sigmoid(logits_t)
    sel = scores + bias_col
    t = logits_t.shape[1]
    sc = [scores[j * N_GROUPS:(j + 1) * N_GROUPS, :] for j in range(GROUP_SIZE)]
    se = [sel[j * N_GROUPS:(j + 1) * N_GROUPS, :] for j in range(GROUP_SIZE)]
    top1 = se[0]
    top2 = jnp.full_like(top1, -jnp.inf)
    for j in range(1, GROUP_SIZE):
        top2 = jnp.maximum(top2, jnp.minimum(top1, se[j]))
        top1 = jnp.maximum(top1, se[j])
    gs = top1 + top2
    gidx = lax.broadcasted_iota(jnp.int32, (N_GROUPS, t), 0)
    beaten = jnp.zeros((N_GROUPS, t), F32)
    for g2 in range(N_GROUPS):
        row = gs[g2:g2 + 1, :]
        tie = jnp.where(gidx > g2, 1.0, 0.0)
        beaten = beaten + jnp.where(row > gs, 1.0, jnp.where(row == gs, tie, 0.0))
    keep = beaten < TOPK_GROUPS
    ms = [jnp.where(keep, se[j], -jnp.inf) for j in range(GROUP_SIZE)]
    cnt = [jnp.zeros((N_GROUPS, t), F32) for _ in range(GROUP_SIZE)]
    for j2 in range(GROUP_SIZE):
        for g2 in range(N_GROUPS):
            row = ms[j2][g2:g2 + 1, :]
            for j in range(GROUP_SIZE):
                if j2 < j:
                    tie = jnp.where(gidx >= g2, 1.0, 0.0)
                else:
                    tie = jnp.where(gidx > g2, 1.0, 0.0)
                cnt[j] = cnt[j] + jnp.where(row > ms[j], 1.0, jnp.where(row == ms[j], tie, 0.0))
    w = [jnp.where(cnt[j] < TOP_K, sc[j], 0.0) for j in range(GROUP_SIZE)]
    tot = w[0]
    for j in range(1, GROUP_SIZE):
        tot = tot + w[j]
    denom = jnp.sum(tot, axis=0, keepdims=True) + 1e-20
    return [w[j] / denom * ROUTED_SCALE for j in range(GROUP_SIZE)]


def _postmix_kernel(x_ref, og_ref, om_ref, gate_ref, mk_ref, mv_ref, lng_ref, lnb_ref, wpg_ref, wpm_ref, wout_ref,
                    ln1g_ref, ln1b_ref, wmq_ref, wmo_ref, ln2g_ref, ln2b_ref, wrt_ref, rb_ref,
                    h2_o, gt_o):
    h = _layer_norm(x_ref[...], lng_ref[...], lnb_ref[...])
    bg = _dot(og_ref[...], wpg_ref[...])
    bm = _dot(om_ref[...], wpm_ref[...])
    merged = (_sigmoid(gate_ref[:, 0:D_MODEL].astype(F32)) * bg
              + _sigmoid(gate_ref[:, D_MODEL:2 * D_MODEL].astype(F32)) * bm)
    mix = _dot(merged.astype(BF16), wout_ref[...])
    h1 = _layer_norm(DEEPNORM_ALPHA * h + mix, ln1g_ref[...], ln1b_ref[...])
    q = _dot(h1.astype(BF16), wmq_ref[...]).astype(BF16)
    outs = []
    for hh in range(MEM_HEADS):
        hc = slice(hh * MEM_HEAD_DIM, (hh + 1) * MEM_HEAD_DIM)
        s = _dot_nt(q[:, hc], mk_ref[:, hc].astype(BF16)) * (MEM_HEAD_DIM ** -0.5)
        p = jnp.exp(s - jnp.max(s, axis=-1, keepdims=True))
        p = p / jnp.sum(p, axis=-1, keepdims=True)
        outs.append(_dot(p.astype(BF16), mv_ref[:, hc].astype(BF16)).astype(BF16))
    o = jnp.concatenate(outs, axis=-1)
    h2 = _layer_norm(DEEPNORM_ALPHA * h1 + _dot(o, wmo_ref[...]), ln2g_ref[...], ln2b_ref[...])
    h2_o[...] = h2
    logits_t = _dot_nt(wrt_ref[...], h2, precision=lax.Precision.HIGHEST)
    gates = _route(logits_t, rb_ref[...])
    for j in range(GROUP_SIZE):
        gt_o[j * N_GROUPS:(j + 1) * N_GROUPS, :] = gates[j]


def _postmix(x, og, om, gate, mem_k, mem_v, pw, td):
    bsz, length, _ = x.shape
    nl = length // td
    tokens = mem_k.shape[1]
    wide = MEM_HEADS * MEM_HEAD_DIM
    consts = [pw['ln_in_g'], pw['ln_in_b'], pw['w_proj_gdn'], pw['w_proj_mla'], pw['w_out'], pw['ln1_g'],
              pw['ln1_b'], pw['w_mq'], pw['w_mo'], pw['ln2_g'], pw['ln2_b'], pw['w_router_t'], pw['router_bias']]

    def rows(width):
        return pl.BlockSpec((None, td, width), lambda b, l: (b, l, 0))

    return pl.pallas_call(
        _postmix_kernel,
        grid=(bsz, nl),
        in_specs=[rows(D_MODEL), rows(D_MODEL), rows(D_MODEL), rows(2 * D_MODEL),
                  pl.BlockSpec((None, tokens, wide), lambda b, l: (b, 0, 0)),
                  pl.BlockSpec((None, tokens, wide), lambda b, l: (b, 0, 0))]
                 + [_const_spec(c.shape) for c in consts],
        out_specs=[rows(D_MODEL), pl.BlockSpec((None, N_EXPERTS, td), lambda b, l: (b, 0, l))],
        out_shape=[jax.ShapeDtypeStruct((bsz, length, D_MODEL), F32),
                   jax.ShapeDtypeStruct((bsz, N_EXPERTS, length), F32)],
        compiler_params=pltpu.CompilerParams(dimension_semantics=("arbitrary", "arbitrary"),
                                             vmem_limit_bytes=VMEM_LIMIT_BYTES),
        name="postmix",
    )(x, og, om, gate, mem_k, mem_v, *consts)


def _moe_kernel(h_ref, g_ref, wgu_ref, wd_ref, wsgu_ref, wsd_ref, ln3g_ref, ln3b_ref, y_o, xb, acc, *, eb):
    e = pl.program_id(1)

    @pl.when(e == 0)
    def _():
        x = h_ref[...].astype(BF16)
        xb[...] = x
        sgu = _dot(x, wsgu_ref[...])
        hid = _silu(sgu[:, 0:SHARED_FF]) * sgu[:, SHARED_FF:]
        acc[...] = _dot(hid.astype(BF16), wsd_ref[...])

    x = xb[...]
    gb = g_ref[...].astype(BF16)
    erow = lax.broadcasted_iota(jnp.int32, (N_EXPERTS, EXPERT_FF), 0)
    for i in range(eb):
        gu = _dot(x, wgu_ref[i])
        pick = jnp.where(erow == e * eb + i, 1.0, 0.0).astype(BF16)
        gcol = _dot(gb, pick)
        hid = _silu(gu[:, 0:EXPERT_FF]) * gu[:, EXPERT_FF:] * gcol
        acc[...] += _dot(hid.astype(BF16), wd_ref[i])

    @pl.when(e == pl.num_programs(1) - 1)
    def _():
        y_o[...] = _layer_norm(DEEPNORM_ALPHA * h_ref[...] + acc[...], ln3g_ref[...], ln3b_ref[...])


def _moe(h2, gates, ew, tm, eb):
    n = h2.shape[0]
    consts = [ew['ws_gu'], ew['ws_down'], ew['ln3_g'], ew['ln3_b']]
    return pl.pallas_call(
        functools.partial(_moe_kernel, eb=eb),
        grid=(n // tm, N_EXPERTS // eb),
        in_specs=[pl.BlockSpec((tm, D_MODEL), lambda t, e: (t, 0)),
                  pl.BlockSpec((tm, N_EXPERTS), lambda t, e: (t, 0)),
                  pl.BlockSpec((eb, D_MODEL, 2 * EXPERT_FF), lambda t, e: (e, 0, 0)),
                  pl.BlockSpec((eb, EXPERT_FF, D_MODEL), lambda t, e: (e, 0, 0))]
                 + [_const_spec(c.shape) for c in consts],
        out_specs=pl.BlockSpec((tm, D_MODEL), lambda t, e: (t, 0)),
        out_shape=jax.ShapeDtypeStruct((n, D_MODEL), F32),
        scratch_shapes=[pltpu.VMEM((tm, D_MODEL), BF16), pltpu.VMEM((tm, D_MODEL), F32)],
        compiler_params=pltpu.CompilerParams(dimension_semantics=("arbitrary", "arbitrary"),
                                             vmem_limit_bytes=VMEM_LIMIT_BYTES),
        name="moe",
    )(h2, gates, ew['w_gu'], ew['w_down'], *consts)


def _prep_post_weights(ln_in_g, ln_in_b, w_proj_gdn, w_proj_mla, w_out, ln1_g, ln1_b, w_mq, w_mo, ln2_g, ln2_b,
                       w_router, router_bias):
    def row(v):
        return v.reshape(1, -1).astype(F32)

    wrt = w_router.T.reshape(N_GROUPS, GROUP_SIZE, D_MODEL).transpose(1, 0, 2).reshape(N_EXPERTS, D_MODEL)
    rb = router_bias.reshape(N_GROUPS, GROUP_SIZE).T.reshape(N_EXPERTS, 1)
    return dict(ln_in_g=row(ln_in_g), ln_in_b=row(ln_in_b), w_proj_gdn=w_proj_gdn.astype(BF16),
                w_proj_mla=w_proj_mla.astype(BF16), w_out=w_out.astype(BF16), ln1_g=row(ln1_g), ln1_b=row(ln1_b),
                w_mq=w_mq.astype(BF16), w_mo=w_mo.astype(BF16), ln2_g=row(ln2_g), ln2_b=row(ln2_b),
                w_router_t=wrt.astype(F32), router_bias=rb.astype(F32))


def _prep_expert_weights(w_gate, w_up, w_down, ws_gate, ws_up, ws_down, ln3_g, ln3_b):
    return dict(w_gu=jnp.concatenate([w_gate, w_up], axis=-1).astype(BF16), w_down=w_down.astype(BF16),
                ws_gu=jnp.concatenate([ws_gate, ws_up], axis=-1).astype(BF16), ws_down=ws_down.astype(BF16),
                ln3_g=ln3_g.reshape(1, -1).astype(F32), ln3_b=ln3_b.reshape(1, -1).astype(F32))


def _gates_to_tokens(gt):
    bsz, _, length = gt.shape
    g = gt.reshape(bsz, GROUP_SIZE, N_GROUPS, length).transpose(0, 3, 2, 1)
    return g.reshape(bsz * length, N_EXPERTS)


def _tail(x, og, om, gate, mem_k, mem_v, pw, ew, td, tm, eb):
    bsz, length, _ = x.shape
    h2, gt = _postmix(x, og, om, gate, mem_k, mem_v, pw, td)
    y = _moe(h2.reshape(bsz * length, D_MODEL), _gates_to_tokens(gt), ew, tm, eb)
    return y.reshape(bsz, length, D_MODEL)


def kernel(x_prompt, x_sample, mem_prompt, state_gdn_conv, state_gdn_ssm, cache_mla_ckv, cache_mla_krope,
           cache_mem_k, cache_mem_v, ln_in_g, ln_in_b, w_in, w_conv, a_log, dt_bias, gdn_norm_g, w_proj_gdn,
           q_norm_g, w_uq, kv_norm_g, w_ukv, w_proj_mla, w_out, ln1_g, ln1_b, w_mq, w_mk, w_mv, w_mo,
           ln2_g, ln2_b, w_router, router_bias, w_gate, w_up, w_down, ws_gate, ws_up, ws_down, ln3_g, ln3_b):
    lyr = 0
    bsz, seq, _ = x_prompt.shape
    dbsz, dseq, _ = x_sample.shape
    past = cache_mla_ckv.shape[2]
    mw = _prep_mixer_weights(w_in[lyr], w_conv[lyr], a_log[lyr], dt_bias[lyr], q_norm_g[lyr], w_uq[lyr],
                             kv_norm_g[lyr], w_ukv[lyr])
    pw = _prep_post_weights(ln_in_g, ln_in_b, w_proj_gdn[lyr], w_proj_mla[lyr], w_out[lyr], ln1_g[lyr], ln1_b[lyr],
                            w_mq[lyr], w_mo[lyr], ln2_g[lyr], ln2_b[lyr], w_router[lyr], router_bias[lyr])
    ew = _prep_expert_weights(w_gate[lyr], w_up[lyr], w_down[lyr], ws_gate[lyr], ws_up[lyr], ws_down[lyr],
                              ln3_g[lyr], ln3_b[lyr])
    wide = MEM_HEADS * MEM_HEAD_DIM

    conv0 = jnp.zeros((bsz, GDN_CONV - 1, GDN_CONV_CH), F32)
    s0 = jnp.zeros((bsz, GDN_HEADS, GDN_DK, GDN_DV), F32)
    (qkv, z, gate, qp, kn, krp, v, ckv_p, kr_p, conv_p, gbc, gbr) = _inproj(
        x_prompt, ln_in_g, ln_in_b, mw, conv0, 0, tl=min(256, seq))
    og, ssm_p = _gdn(qkv, gbc, gbr, z, s0, gdn_norm_g[lyr], chunk=CHUNK, tg=min(512, seq))
    om = _flash_attention(qp, kn, krp, v, tq=min(512, seq))
    mem_k, mem_v = _memproj(mem_prompt, w_mk[lyr], w_mv[lyr])
    y_p = _tail(x_prompt, og, om, gate, mem_k, mem_v, pw, ew, td=min(256, seq), tm=min(1024, bsz * seq), eb=2)

    (qkv, z, gate, qp, kn, krp, v, ckv_s, kr_s, conv_s, gbc, gbr) = _inproj(
        x_sample, ln_in_g, ln_in_b, mw, state_gdn_conv[lyr], past, tl=dseq)
    chunk_s = CHUNK if dseq % CHUNK == 0 else dseq
    og, ssm_s = _gdn(qkv, gbc, gbr, z, state_gdn_ssm[lyr], gdn_norm_g[lyr], chunk=chunk_s, tg=dseq)
    kn_past, v_past = _kvup(cache_mla_ckv[lyr], mw['w_kv'], tr=min(1024, past))
    om = _cached_attention(qp, kn_past, cache_mla_krope[lyr], v_past, kn, krp, v)
    y_s = _tail(x_sample, og, om, gate, cache_mem_k[lyr].reshape(dbsz, -1, wide),
                cache_mem_v[lyr].reshape(dbsz, -1, wide), pw, ew, td=dseq, tm=dbsz * dseq, eb=2)

    return (y_p, y_s, conv_p[None], ssm_p[None], ckv_p[None], kr_p[None],
            mem_k.reshape(1, bsz, -1, MEM_HEADS, MEM_HEAD_DIM), mem_v.reshape(1, bsz, -1, MEM_HEADS, MEM_HEAD_DIM),
            conv_s[None], ssm_s[None], ckv_s[None], kr_s[None])
```

```python
import functools
import math

import jax
import jax.numpy as jnp
from jax import lax
from jax.experimental import pallas as pl
from jax.experimental.pallas import tpu as pltpu

F32 = jnp.float32
BF16 = jnp.bfloat16

D_MODEL = 1024
CHUNK = 64
GDN_HEADS = 8
GDN_DK = 128
GDN_DV = 128
GDN_CONV = 4
GDN_QK = GDN_HEADS * GDN_DK
GDN_CONV_CH = 2 * GDN_QK + GDN_HEADS * GDN_DV
MLA_HEADS = 8
MLA_Q_LORA = 384
MLA_KV_LORA = 256
MLA_NOPE = 128
MLA_ROPE = 64
MLA_V = 128
MLA_SCALE = (MLA_NOPE + MLA_ROPE) ** -0.5
MLA_SCALE_LOG2 = MLA_SCALE * math.log2(math.e)
MLA_QK_PAD = 256
ROPE_THETA = 10000.0
MEM_HEADS = 4
MEM_HEAD_DIM = 128
N_EXPERTS = 64
TOP_K = 8
N_GROUPS = 8
GROUP_SIZE = N_EXPERTS // N_GROUPS
TOPK_GROUPS = 4
EXPERT_FF = 256
SHARED_FF = 256
ROUTED_SCALE = 2.5
DEPTH = 1
DEEPNORM_ALPHA = (2.0 * DEPTH) ** 0.25
IN_WIDTHS = (GDN_CONV_CH, GDN_HEADS * GDN_DV, GDN_HEADS, GDN_HEADS, MLA_Q_LORA, MLA_KV_LORA, MLA_ROPE,
             D_MODEL, D_MODEL)

VMEM_LIMIT_BYTES = 56 * 1024 * 1024


def _dot(a, b, precision=None):
    return jnp.dot(a, b, preferred_element_type=F32, precision=precision)


def _dot_nt(a, b, precision=None):
    return lax.dot_general(a, b, (((1,), (1,)), ((), ())), preferred_element_type=F32, precision=precision)


def _dot_tn(a, b):
    return lax.dot_general(a, b, (((0,), (0,)), ((), ())), preferred_element_type=F32)


def _layer_norm(x, g, b, eps=1e-5):
    mu = jnp.mean(x, axis=-1, keepdims=True)
    xc = x - mu
    var = jnp.mean(xc * xc, axis=-1, keepdims=True)
    return xc * lax.rsqrt(var + eps) * g + b


def _rms_norm(x, g, eps=1e-6):
    return x * lax.rsqrt(jnp.mean(x * x, axis=-1, keepdims=True) + eps) * g


def _sigmoid(x):
    return 1.0 / (1.0 + jnp.exp(-x))


def _silu(x):
    return x * _sigmoid(x)


def _softplus(x):
    return jnp.maximum(x, 0.0) + jnp.log1p(jnp.exp(-jnp.abs(x)))


def _const_spec(shape):
    nd = len(shape)
    return pl.BlockSpec(shape, lambda *_: (0,) * nd, pipeline_mode=pl.Buffered(1))


def _inproj_kernel(x_ref, lng_ref, lnb_ref, wqkv_ref, wz_ref, wgate_ref, wcq_ref, wckv_ref, wkr_ref, wab_ref,
                   wabt_ref, convw_ref, conv0_ref, arow_ref, acol_ref, qng_ref, wuq_ref, wuqs_ref, kvng_ref,
                   wukv_ref, t1_ref, t2_ref,
                   qkv_o, z_o, gate_o, qp_o, kn_o, krp_o, v_o, ckv_o, kr_o, conv_o, gbc_o, gbr_o,
                   cbuf):
    tl = x_ref.shape[0]
    li = pl.program_id(1)

    h = _layer_norm(x_ref[...], lng_ref[...], lnb_ref[...])
    hb = h.astype(BF16)

    @pl.when(li == 0)
    def _():
        cbuf[0:8, :] = conv0_ref[...]

    cbuf[8:8 + tl, :] = _dot(hb, wqkv_ref[...])
    conv_o[...] = cbuf[tl + 5:tl + 8, :]
    for part in range(3):
        cols = slice(part * GDN_QK, (part + 1) * GDN_QK)
        y = convw_ref[0:1, cols] * cbuf[5:5 + tl, cols]
        for i in range(1, GDN_CONV):
            y = y + convw_ref[i:i + 1, cols] * cbuf[5 + i:5 + i + tl, cols]
        y = _silu(y)
        if part < 2:
            for hh in range(GDN_HEADS):
                hc = slice(hh * GDN_DK, (hh + 1) * GDN_DK)
                yh = y[:, hc]
                r = lax.rsqrt(jnp.sum(yh * yh, axis=-1, keepdims=True) + 1e-6)
                yh = yh * r
                if part == 0:
                    yh = yh * (GDN_DK ** -0.5)
                qkv_o[:, part * GDN_QK + hh * GDN_DK:part * GDN_QK + (hh + 1) * GDN_DK] = yh.astype(BF16)
        else:
            qkv_o[:, cols] = y.astype(BF16)
    cbuf[0:8, :] = cbuf[tl:tl + 8, :]

    z_o[...] = _dot(hb, wz_ref[...]).astype(BF16)
    gate_o[...] = _dot(hb, wgate_ref[...]).astype(BF16)

    ab = _dot(hb, wab_ref[...])
    arow = arow_ref[...]
    g_c = arow[0:1, :] * _softplus(ab[:, 0:GDN_HEADS] + arow[1:2, :])
    b_c = _sigmoid(ab[:, GDN_HEADS:2 * GDN_HEADS])
    gbc_o[:, 0:GDN_HEADS] = g_c
    gbc_o[:, GDN_HEADS:2 * GDN_HEADS] = b_c
    abt = _dot_nt(wabt_ref[...], hb)
    acol = acol_ref[...]
    gbr_o[0:GDN_HEADS, :] = acol[:, 0:1] * _softplus(abt[0:GDN_HEADS, :] + acol[:, 1:2])
    gbr_o[GDN_HEADS:2 * GDN_HEADS, :] = _sigmoid(abt[GDN_HEADS:2 * GDN_HEADS, :])

    t1 = t1_ref[...]
    t2 = t2_ref[...]

    cq = _rms_norm(_dot(hb, wcq_ref[...]), qng_ref[...]).astype(BF16)
    qm = _dot(cq, wuq_ref[...])
    qs = _dot(cq, wuqs_ref[...])
    for hh in range(MLA_HEADS):
        base = hh * MLA_QK_PAD
        qp_o[:, base:base + MLA_NOPE] = (qm[:, base:base + MLA_NOPE] * MLA_SCALE_LOG2).astype(BF16)
        hi = qm[:, base + MLA_NOPE:base + MLA_QK_PAD] * t1 + qs[:, hh * 128:(hh + 1) * 128] * t2
        qp_o[:, base + MLA_NOPE:base + MLA_QK_PAD] = (hi * MLA_SCALE_LOG2).astype(BF16)

    ckv = _rms_norm(_dot(hb, wckv_ref[...]), kvng_ref[...])
    ckv_o[...] = ckv
    kv = _dot(ckv.astype(BF16), wukv_ref[...])
    kn_o[...] = kv[:, 0:MLA_HEADS * MLA_NOPE].astype(BF16)
    v_o[...] = kv[:, MLA_HEADS * MLA_NOPE:].astype(BF16)
    krm = _dot(hb, wkr_ref[...])
    krp = krm[:, 0:128] * t1 + krm[:, 128:256] * t2
    kr_o[...] = krp[:, 0:MLA_ROPE]
    krp_o[...] = krp.astype(BF16)


def _swap_halves(w):
    half = w.shape[-1] // 2
    return jnp.concatenate([w[..., half:], w[..., :half]], axis=-1)


def _rope_tables(pos0, length):
    inv_freq = jnp.power(ROPE_THETA, -jnp.arange(0, MLA_ROPE, 2, dtype=F32) / MLA_ROPE)
    ang = (pos0 + jnp.arange(length)).astype(F32)[:, None] * inv_freq[None, :]
    c, s = jnp.cos(ang), jnp.sin(ang)
    zeros = jnp.zeros((length, 128 - MLA_ROPE), F32)
    return jnp.concatenate([c, c, zeros], axis=-1), jnp.concatenate([-s, s, zeros], axis=-1)


def _prep_mixer_weights(w_in, w_conv, a_log, dt_bias, q_norm_g, w_uq, kv_norm_g, w_ukv):
    splits = []
    off = 0
    for wd in IN_WIDTHS:
        splits.append(w_in[:, off:off + wd])
        off += wd
    w_qkv, w_z, w_a, w_b, w_cq, w_ckv, w_kr, w_gg, w_gm = splits
    z64 = jnp.zeros((D_MODEL, 128 - MLA_ROPE), F32)
    w_krg = jnp.concatenate([w_kr, z64, _swap_halves(w_kr), z64], axis=-1)
    w_ab = jnp.concatenate([w_a, w_b], axis=-1)
    wq = w_uq.reshape(MLA_Q_LORA, MLA_HEADS, MLA_NOPE + MLA_ROPE)
    nope, rope = wq[:, :, :MLA_NOPE], wq[:, :, MLA_NOPE:]
    zq = jnp.zeros((MLA_Q_LORA, MLA_HEADS, MLA_QK_PAD - MLA_NOPE - MLA_ROPE), F32)
    wq_main = jnp.concatenate([nope, rope, zq], axis=-1).reshape(MLA_Q_LORA, MLA_HEADS * MLA_QK_PAD)
    wq_swap = jnp.concatenate([_swap_halves(rope), zq], axis=-1).reshape(MLA_Q_LORA, MLA_HEADS * 128)
    wkv = w_ukv.reshape(MLA_KV_LORA, MLA_HEADS, MLA_NOPE + MLA_V)
    w_kv = jnp.concatenate([wkv[:, :, :MLA_NOPE].reshape(MLA_KV_LORA, -1),
                            wkv[:, :, MLA_NOPE:].reshape(MLA_KV_LORA, -1)], axis=-1)
    neg_a = -jnp.exp(a_log.astype(F32))
    arow = jnp.stack([neg_a, dt_bias.astype(F32)], axis=0)
    return dict(
        w_qkv=w_qkv.astype(BF16), w_z=w_z.astype(BF16),
        w_gate=jnp.concatenate([w_gg, w_gm], axis=-1).astype(BF16),
        w_cq=w_cq.astype(BF16), w_ckv=w_ckv.astype(BF16), w_krg=w_krg.astype(BF16),
        w_ab=w_ab.astype(BF16), w_abt=w_ab.T.astype(BF16),
        w_conv=w_conv.astype(F32), arow=arow, acol=arow.T,
        q_norm_g=q_norm_g.reshape(1, -1).astype(F32), wq_main=wq_main.astype(BF16), wq_swap=wq_swap.astype(BF16),
        kv_norm_g=kv_norm_g.reshape(1, -1).astype(F32), w_kv=w_kv.astype(BF16))


def _inproj(x, ln_g, ln_b, mw, conv_state, pos0, tl):
    bsz, length, _ = x.shape
    nl = length // tl
    t1, t2 = _rope_tables(pos0, length)
    conv0 = jnp.concatenate([jnp.zeros((bsz, 8 - (GDN_CONV - 1), GDN_CONV_CH), F32), conv_state.astype(F32)], axis=1)

    def rows(width, dtype):
        return (jax.ShapeDtypeStruct((bsz, length, width), dtype),
                pl.BlockSpec((None, tl, width), lambda b, l: (b, l, 0)))

    outs = [rows(GDN_CONV_CH, BF16), rows(GDN_HEADS * GDN_DV, BF16), rows(2 * D_MODEL, BF16),
            rows(MLA_HEADS * MLA_QK_PAD, BF16), rows(MLA_HEADS * MLA_NOPE, BF16), rows(128, BF16),
            rows(MLA_HEADS * MLA_V, BF16), rows(MLA_KV_LORA, F32), rows(MLA_ROPE, F32),
            (jax.ShapeDtypeStruct((bsz, GDN_CONV - 1, GDN_CONV_CH), F32),
             pl.BlockSpec((None, GDN_CONV - 1, GDN_CONV_CH), lambda b, l: (b, 0, 0))),
            rows(2 * GDN_HEADS, F32),
            (jax.ShapeDtypeStruct((bsz, 2 * GDN_HEADS, length), F32),
             pl.BlockSpec((None, 2 * GDN_HEADS, tl), lambda b, l: (b, 0, l)))]
    consts = [ln_g.reshape(1, -1), ln_b.reshape(1, -1), mw['w_qkv'], mw['w_z'], mw['w_gate'], mw['w_cq'],
              mw['w_ckv'], mw['w_krg'], mw['w_ab'], mw['w_abt'], mw['w_conv']]
    consts2 = [mw['arow'], mw['acol'], mw['q_norm_g'], mw['wq_main'], mw['wq_swap'], mw['kv_norm_g'], mw['w_kv']]
    in_specs = ([pl.BlockSpec((None, tl, D_MODEL), lambda b, l: (b, l, 0))]
                + [_const_spec(c.shape) for c in consts]
                + [pl.BlockSpec((None, 8, GDN_CONV_CH), lambda b, l: (b, 0, 0))]
                + [_const_spec(c.shape) for c in consts2]
                + [pl.BlockSpec((tl, 128), lambda b, l: (l, 0)), pl.BlockSpec((tl, 128), lambda b, l: (l, 0))])
    return pl.pallas_call(
        _inproj_kernel,
        grid=(bsz, nl),
        in_specs=in_specs,
        out_specs=[o[1] for o in outs],
        out_shape=[o[0] for o in outs],
        scratch_shapes=[pltpu.VMEM((tl + 8, GDN_CONV_CH), F32)],
        compiler_params=pltpu.CompilerParams(dimension_semantics=("arbitrary", "arbitrary"),
                                             vmem_limit_bytes=VMEM_LIMIT_BYTES),
        name="inproj",
    )(x, *consts, conv0, *consts2, t1, t2)


def _gdn_kernel(qkv_ref, gbc_ref, gbr_ref, z_ref, s0_ref, ng_ref, og_o, s_o, s_scr, *, chunk):
    tg = qkv_ref.shape[1]
    nchunks = tg // chunk
    li = pl.program_id(1)

    @pl.when(li == 0)
    def _():
        s_scr[...] = s0_ref[...]

    ri = lax.broadcasted_iota(jnp.int32, (chunk, chunk), 0)
    ci = lax.broadcasted_iota(jnp.int32, (chunk, chunk), 1)
    incl = ri >= ci
    strict = ri > ci
    tri = jnp.where(incl, 1.0, 0.0).astype(F32)
    eye = jnp.where(ri == ci, 1.0, 0.0).astype(F32)
    ng = ng_ref[...]
    n_sq = int(math.log2(chunk)) - 1

    nb = qkv_ref.shape[0]
    chains = [(g, hh) for g in range(nb) for hh in range(GDN_HEADS)]

    def chunk_body(c, carry):
        r0 = pl.multiple_of(c * chunk, chunk)
        rows = pl.ds(r0, chunk)
        gbc = [gbc_ref[g, rows, :] for g in range(nb)]
        gc_col = [_dot(tri, gbc[g], precision=lax.Precision.HIGHEST) for g in range(nb)]
        gc_row = [_dot_nt(gbr_ref[g, c], tri, precision=lax.Precision.HIGHEST) for g in range(nb)]
        q, k, qkk = {}, {}, {}
        for ch in chains:
            g, hh = ch
            q[ch] = qkv_ref[g, rows, hh * GDN_DK:(hh + 1) * GDN_DK]
            k[ch] = qkv_ref[g, rows, GDN_QK + hh * GDN_DK:GDN_QK + (hh + 1) * GDN_DK]
            qkk[ch] = _dot_nt(jnp.concatenate([q[ch], k[ch]], axis=0), k[ch])
        gcc, beta, gamma, g_end, qk, xp, t_inv = {}, {}, {}, {}, {}, {}, {}
        for ch in chains:
            g, hh = ch
            gcc[ch] = gc_col[g][:, hh:hh + 1]
            gcr = gc_row[g][hh:hh + 1, :]
            beta[ch] = gbc[g][:, GDN_HEADS + hh:GDN_HEADS + hh + 1]
            g_end[ch] = gc_col[g][chunk - 1:chunk, hh:hh + 1]
            gamma[ch] = jnp.exp(gcc[ch])
            decay = jnp.where(incl, jnp.exp(jnp.where(incl, gcc[ch] - gcr, 0.0)), 0.0)
            qk[ch] = (decay * qkk[ch][0:chunk, :]).astype(BF16)
            xp[ch] = jnp.where(strict, beta[ch] * decay * qkk[ch][chunk:2 * chunk, :], 0.0)
            t_inv[ch] = eye - xp[ch]
        for _ in range(n_sq):
            for ch in chains:
                xb = xp[ch].astype(BF16)
                xp[ch] = _dot(xb, xb)
            for ch in chains:
                t_inv[ch] = t_inv[ch] + _dot(t_inv[ch].astype(BF16), xp[ch].astype(BF16))
        sol, q_s, sb, s_old, k_dec = {}, {}, {}, {}, {}
        for ch in chains:
            g, hh = ch
            kf = k[ch].astype(F32)
            v = qkv_ref[g, rows, 2 * GDN_QK + hh * GDN_DV:2 * GDN_QK + (hh + 1) * GDN_DV].astype(F32)
            rhs = jnp.concatenate([beta[ch] * v, (beta[ch] * gamma[ch]) * kf], axis=-1).astype(BF16)
            sol[ch] = _dot(t_inv[ch].astype(BF16), rhs)
            k_dec[ch] = (jnp.exp(g_end[ch] - gcc[ch]) * kf).astype(BF16)
            s_old[ch] = s_scr[g, hh]
            sb[ch] = s_old[ch].astype(BF16)
            q_s[ch] = _dot((gamma[ch] * q[ch].astype(F32)).astype(BF16), sb[ch])
        ub = {}
        for ch in chains:
            u = sol[ch][:, :GDN_DV] - _dot(sol[ch][:, GDN_DV:].astype(BF16), sb[ch])
            ub[ch] = u.astype(BF16)
        for ch in chains:
            g, hh = ch
            o = q_s[ch] + _dot(qk[ch], ub[ch])
            s_scr[g, hh] = jnp.exp(g_end[ch]) * s_old[ch] + _dot_tn(k_dec[ch], ub[ch])
            zz = z_ref[g, rows, hh * GDN_DV:(hh + 1) * GDN_DV].astype(F32)
            og_o[g, rows, hh * GDN_DV:(hh + 1) * GDN_DV] = (_rms_norm(o, ng) * _silu(zz)).astype(BF16)
        return carry

    lax.fori_loop(0, nchunks, chunk_body, 0)
    s_o[...] = s_scr[...]


def _gdn(qkv, gbc, gbr, z, s0, norm_g, chunk, tg, nb):
    bsz, length, _ = qkv.shape
    nl = length // tg
    st_shape = (GDN_HEADS, GDN_DK, GDN_DV)
    gbr = gbr.reshape(bsz, 2 * GDN_HEADS, length // chunk, chunk).transpose(0, 2, 1, 3)
    return pl.pallas_call(
        functools.partial(_gdn_kernel, chunk=chunk),
        grid=(bsz // nb, nl),
        in_specs=[pl.BlockSpec((nb, tg, GDN_CONV_CH), lambda b, l: (b, l, 0)),
                  pl.BlockSpec((nb, tg, 2 * GDN_HEADS), lambda b, l: (b, l, 0)),
                  pl.BlockSpec((nb, tg // chunk, 2 * GDN_HEADS, chunk), lambda b, l: (b, l, 0, 0)),
                  pl.BlockSpec((nb, tg, GDN_HEADS * GDN_DV), lambda b, l: (b, l, 0)),
                  pl.BlockSpec((nb,) + st_shape, lambda b, l: (b, 0, 0, 0)),
                  _const_spec((1, GDN_DV))],
        out_specs=[pl.BlockSpec((nb, tg, GDN_HEADS * GDN_DV), lambda b, l: (b, l, 0)),
                   pl.BlockSpec((nb,) + st_shape, lambda b, l: (b, 0, 0, 0))],
        out_shape=[jax.ShapeDtypeStruct((bsz, length, GDN_HEADS * GDN_DV), BF16),
                   jax.ShapeDtypeStruct((bsz,) + st_shape, F32)],
        scratch_shapes=[pltpu.VMEM((nb,) + st_shape, F32)],
        compiler_params=pltpu.CompilerParams(dimension_semantics=("arbitrary", "arbitrary"),
                                             vmem_limit_bytes=VMEM_LIMIT_BYTES),
        name="gdn",
    )(qkv, gbc, gbr, z, s0.astype(F32), norm_g.reshape(1, -1).astype(F32))


def _flash_kernel(q_ref, kn_ref, krp_ref, v_ref, o_ref, kcat, *, tq, nh):
    qi = pl.program_id(2)

    @pl.when(qi == 0)
    def _():
        for j in range(nh):
            kcat[j, :, 0:MLA_NOPE] = kn_ref[:, j * MLA_NOPE:(j + 1) * MLA_NOPE]
            kcat[j, :, MLA_NOPE:MLA_QK_PAD] = krp_ref[...]

    qs = [q_ref[:, j * MLA_QK_PAD:(j + 1) * MLA_QK_PAD] for j in range(nh)]

    def tile(ki, carry, masked):
        rows = pl.ds(pl.multiple_of(ki * tq, tq), tq)
        ss = [_dot_nt(qs[j], kcat[j, rows, :]) for j in range(nh)]
        if masked:
            ri = lax.broadcasted_iota(jnp.int32, (tq, tq), 0) // CHUNK
            ci = lax.broadcasted_iota(jnp.int32, (tq, tq), 1) // CHUNK
            ss = [jnp.where(ci <= ri, s, -jnp.inf) for s in ss]
        ps, stats = [], []
        for j in range(nh):
            m, l, _ = carry[j]
            m_new = jnp.maximum(m, jnp.max(ss[j], axis=-1, keepdims=True))
            p = jnp.exp2(ss[j] - m_new)
            alpha = jnp.exp2(m - m_new)
            stats.append((m_new, alpha * l + jnp.sum(p, axis=-1, keepdims=True), alpha))
            ps.append(p.astype(BF16))
        out = []
        for j in range(nh):
            m_new, l_new, alpha = stats[j]
            acc = alpha * carry[j][2] + _dot(ps[j], v_ref[rows, j * MLA_V:(j + 1) * MLA_V])
            out.append((m_new, l_new, acc))
        return tuple(out)

    init = tuple((jnp.full((tq, 1), -jnp.inf, F32), jnp.zeros((tq, 1), F32), jnp.zeros((tq, MLA_V), F32))
                 for _ in range(nh))
    carry = lax.fori_loop(0, qi, lambda ki, c: tile(ki, c, False), init)
    carry = tile(qi, carry, True)
    for j in range(nh):
        _, l, acc = carry[j]
        o_ref[:, j * MLA_V:(j + 1) * MLA_V] = (acc / l).astype(BF16)


def _flash_attention(qp, kn, krp, v, tq, nh):
    bsz, length, _ = qp.shape
    nq = length // tq
    return pl.pallas_call(
        functools.partial(_flash_kernel, tq=tq, nh=nh),
        grid=(bsz, MLA_HEADS // nh, nq),
        in_specs=[pl.BlockSpec((None, tq, nh * MLA_QK_PAD), lambda b, h, i: (b, i, h)),
                  pl.BlockSpec((None, length, nh * MLA_NOPE), lambda b, h, i: (b, 0, h)),
                  pl.BlockSpec((None, length, 128), lambda b, h, i: (b, 0, 0)),
                  pl.BlockSpec((None, length, nh * MLA_V), lambda b, h, i: (b, 0, h))],
        out_specs=pl.BlockSpec((None, tq, nh * MLA_V), lambda b, h, i: (b, i, h)),
        out_shape=jax.ShapeDtypeStruct((bsz, length, MLA_HEADS * MLA_V), BF16),
        scratch_shapes=[pltpu.VMEM((nh, length, MLA_QK_PAD), BF16)],
        compiler_params=pltpu.CompilerParams(dimension_semantics=("arbitrary", "arbitrary", "arbitrary"),
                                             vmem_limit_bytes=VMEM_LIMIT_BYTES),
        name="mla_flash",
    )(qp, kn, krp, v)


def _kvup_kernel(c_ref, w_ref, kn_o, v_o):
    kv = _dot(c_ref[...].astype(BF16), w_ref[...])
    kn_o[...] = kv[:, 0:MLA_HEADS * MLA_NOPE].astype(BF16)
    v_o[...] = kv[:, MLA_HEADS * MLA_NOPE:].astype(BF16)


def _kvup(ckv, w_kv, tr):
    bsz, length, _ = ckv.shape
    wide = MLA_HEADS * MLA_NOPE
    return pl.pallas_call(
        _kvup_kernel,
        grid=(bsz, length // tr),
        in_specs=[pl.BlockSpec((None, tr, MLA_KV_LORA), lambda b, l: (b, l, 0)), _const_spec(w_kv.shape)],
        out_specs=[pl.BlockSpec((None, tr, wide), lambda b, l: (b, l, 0)),
                   pl.BlockSpec((None, tr, wide), lambda b, l: (b, l, 0))],
        out_shape=[jax.ShapeDtypeStruct((bsz, length, wide), BF16), jax.ShapeDtypeStruct((bsz, length, wide), BF16)],
        compiler_params=pltpu.CompilerParams(dimension_semantics=("arbitrary", "arbitrary"),
                                             vmem_limit_bytes=VMEM_LIMIT_BYTES),
        name="kv_up_cached",
    )(ckv, w_kv)


def _cached_attn_kernel(q_ref, knp_ref, krp_ref, vp_ref, knn_ref, krn_ref, vn_ref, o_ref, *, past):
    lq = q_ref.shape[0]
    q_chunk = (past + lax.broadcasted_iota(jnp.int32, (lq, past), 0)) // CHUNK
    vis_p = lax.broadcasted_iota(jnp.int32, (lq, past), 1) // CHUNK <= q_chunk
    q_chunk_n = (past + lax.broadcasted_iota(jnp.int32, (lq, lq), 0)) // CHUNK
    vis_n = (past + lax.broadcasted_iota(jnp.int32, (lq, lq), 1)) // CHUNK <= q_chunk_n
    kr_past = krp_ref[...].astype(BF16)
    kr_new = krn_ref[...]
    for hh in range(MLA_HEADS):
        base = hh * MLA_QK_PAD
        qn = q_ref[:, base:base + MLA_NOPE]
        qr = q_ref[:, base + MLA_NOPE:base + MLA_QK_PAD]
        hc = slice(hh * MLA_NOPE, (hh + 1) * MLA_NOPE)
        s_p = _dot_nt(qn, knp_ref[:, hc]) + _dot_nt(qr[:, 0:MLA_ROPE], kr_past)
        s_n = _dot_nt(qn, knn_ref[:, hc]) + _dot_nt(qr, kr_new)
        s_p = jnp.where(vis_p, s_p, -jnp.inf)
        s_n = jnp.where(vis_n, s_n, -jnp.inf)
        m = jnp.maximum(jnp.max(s_p, axis=-1, keepdims=True), jnp.max(s_n, axis=-1, keepdims=True))
        p_p = jnp.exp2(s_p - m)
        p_n = jnp.exp2(s_n - m)
        l = jnp.sum(p_p, axis=-1, keepdims=True) + jnp.sum(p_n, axis=-1, keepdims=True)
        vc = slice(hh * MLA_V, (hh + 1) * MLA_V)
        o = _dot(p_p.astype(BF16), vp_ref[:, vc]) + _dot(p_n.astype(BF16), vn_ref[:, vc])
        o_ref[:, vc] = (o / l).astype(BF16)


def _cached_attention(qp, kn_past, kr_past, v_past, kn_new, krp_new, v_new):
    bsz, lq, _ = qp.shape
    past = kn_past.shape[1]
    wide = MLA_HEADS * MLA_NOPE

    def per_b(rows, width):
        return pl.BlockSpec((None, rows, width), lambda b: (b, 0, 0))

    return pl.pallas_call(
        functools.partial(_cached_attn_kernel, past=past),
        grid=(bsz,),
        in_specs=[per_b(lq, MLA_HEADS * MLA_QK_PAD), per_b(past, wide), per_b(past, MLA_ROPE), per_b(past, wide),
                  per_b(lq, wide), per_b(lq, 128), per_b(lq, wide)],
        out_specs=per_b(lq, MLA_HEADS * MLA_V),
        out_shape=jax.ShapeDtypeStruct((bsz, lq, MLA_HEADS * MLA_V), BF16),
        compiler_params=pltpu.CompilerParams(dimension_semantics=("arbitrary",),
                                             vmem_limit_bytes=VMEM_LIMIT_BYTES),
        name="mla_cached",
    )(qp, kn_past, kr_past, v_past, kn_new, krp_new, v_new)


def _memproj_kernel(m_ref, wk_ref, wv_ref, k_o, v_o):
    mb = m_ref[...].astype(BF16)
    k_o[...] = _dot(mb, wk_ref[...])
    v_o[...] = _dot(mb, wv_ref[...])


def _memproj(mem, w_mk, w_mv):
    bsz, tokens, _ = mem.shape
    wide = MEM_HEADS * MEM_HEAD_DIM
    return pl.pallas_call(
        _memproj_kernel,
        grid=(bsz,),
        in_specs=[pl.BlockSpec((None, tokens, D_MODEL), lambda b: (b, 0, 0)),
                  _const_spec(w_mk.shape), _const_spec(w_mv.shape)],
        out_specs=[pl.BlockSpec((None, tokens, wide), lambda b: (b, 0, 0)),
                   pl.BlockSpec((None, tokens, wide), lambda b: (b, 0, 0))],
        out_shape=[jax.ShapeDtypeStruct((bsz, tokens, wide), F32), jax.ShapeDtypeStruct((bsz, tokens, wide), F32)],
        compiler_params=pltpu.CompilerParams(dimension_semantics=("arbitrary",),
                                             vmem_limit_bytes=VMEM_LIMIT_BYTES),
        name="mem_proj",
    )(mem, w_mk.astype(BF16), w_mv.astype(BF16))


def _route(logits_t, bias_col):
    scores = _sigmoid(logits_t)
    sel = scores + bias_col
    t = logits_t.shape[1]
    sc = [scores[j * N_GROUPS:(j + 1) * N_GROUPS, :] for j in range(GROUP_SIZE)]
    se = [sel[j * N_GROUPS:(j + 1) * N_GROUPS, :] for j in range(GROUP_SIZE)]
    top1 = se[0]
    top2 = jnp.full_like(top1, -jnp.inf)
    for j in range(1, GROUP_SIZE):
        top2 = jnp.maximum(top2, jnp.minimum(top1, se[j]))
        top1 = jnp.maximum(top1, se[j])
    gs = top1 + top2
    gidx = lax.broadcasted_iota(jnp.int32, (N_GROUPS, t), 0)
    beaten = jnp.zeros((N_GROUPS, t), F32)
    for g2 in range(N_GROUPS):
        row = gs[g2:g2 + 1, :]
        tie = jnp.where(gidx > g2, 1.0, 0.0)
        beaten = beaten + jnp.where(row > gs, 1.0, jnp.where(row == gs, tie, 0.0))
    keep = beaten < TOPK_GROUPS
    ms = [jnp.where(keep, se[j], -jnp.inf) for j in range(GROUP_SIZE)]
    cnt = [jnp.zeros((N_GROUPS, t), F32) for _ in range(GROUP_SIZE)]
    for j2 in range(GROUP_SIZE):
        for g2 in range(N_GROUPS):
            row = ms[j2][g2:g2 + 1, :]
            for j in range(GROUP_SIZE):
                if j2 < j:
                    tie = jnp.where(gidx >= g2, 1.0, 0.0)
                else:
                    tie = jnp.where(gidx > g2, 1.0, 0.0)
                cnt[j] = cnt[j] + jnp.where(row > ms[j], 1.0, jnp.where(row == ms[j], tie, 0.0))
    w = [jnp.where(cnt[j] < TOP_K, sc[j], 0.0) for j in range(GROUP_SIZE)]
    tot = w[0]
    for j in range(1, GROUP_SIZE):
        tot = tot + w[j]
    denom = jnp.sum(tot, axis=0, keepdims=True) + 1e-20
    return [w[j] / denom * ROUTED_SCALE for j in range(GROUP_SIZE)]


def _postmix_kernel(x_ref, og_ref, om_ref, gate_ref, mk_ref, mv_ref, lng_ref, lnb_ref, wpg_ref, wpm_ref, wout_ref,
                    ln1g_ref, ln1b_ref, wmq_ref, wmo_ref, ln2g_ref, ln2b_ref, wrt_ref, rb_ref,
                    h2_o, gt_o):
    h = _layer_norm(x_ref[...], lng_ref[...], lnb_ref[...])
    bg = _dot(og_ref[...], wpg_ref[...])
    bm = _dot(om_ref[...], wpm_ref[...])
    merged = (_sigmoid(gate_ref[:, 0:D_MODEL].astype(F32)) * bg
              + _sigmoid(gate_ref[:, D_MODEL:2 * D_MODEL].astype(F32)) * bm)
    mix = _dot(merged.astype(BF16), wout_ref[...])
    h1 = _layer_norm(DEEPNORM_ALPHA * h + mix, ln1g_ref[...], ln1b_ref[...])
    q = _dot(h1.astype(BF16), wmq_ref[...]).astype(BF16)
    outs = []
    for hh in range(MEM_HEADS):
        hc = slice(hh * MEM_HEAD_DIM, (hh + 1) * MEM_HEAD_DIM)
        s = _dot_nt(q[:, hc], mk_ref[:, hc].astype(BF16)) * (MEM_HEAD_DIM ** -0.5)
        p = jnp.exp(s - jnp.max(s, axis=-1, keepdims=True))
        p = p / jnp.sum(p, axis=-1, keepdims=True)
        outs.append(_dot(p.astype(BF16), mv_ref[:, hc].astype(BF16)).astype(BF16))
    o = jnp.concatenate(outs, axis=-1)
    h2 = _layer_norm(DEEPNORM_ALPHA * h1 + _dot(o, wmo_ref[...]), ln2g_ref[...], ln2b_ref[...])
    h2_o[...] = h2
    logits_t = _dot_nt(wrt_ref[...], h2, precision=lax.Precision.HIGHEST)
    gates = _route(logits_t, rb_ref[...])
    for j in range(GROUP_SIZE):
        gt_o[j * N_GROUPS:(j + 1) * N_GROUPS, :] = gates[j]


def _postmix(x, og, om, gate, mem_k, mem_v, pw, td):
    bsz, length, _ = x.shape
    nl = length // td
    tokens = mem_k.shape[1]
    wide = MEM_HEADS * MEM_HEAD_DIM
    consts = [pw['ln_in_g'], pw['ln_in_b'], pw['w_proj_gdn'], pw['w_proj_mla'], pw['w_out'], pw['ln1_g'],
              pw['ln1_b'], pw['w_mq'], pw['w_mo'], pw['ln2_g'], pw['ln2_b'], pw['w_router_t'], pw['router_bias']]

    def rows(width):
        return pl.BlockSpec((None, td, width), lambda b, l: (b, l, 0))

    return pl.pallas_call(
        _postmix_kernel,
        grid=(bsz, nl),
        in_specs=[rows(D_MODEL), rows(D_MODEL), rows(D_MODEL), rows(2 * D_MODEL),
                  pl.BlockSpec((None, tokens, wide), lambda b, l: (b, 0, 0)),
                  pl.BlockSpec((None, tokens, wide), lambda b, l: (b, 0, 0))]
                 + [_const_spec(c.shape) for c in consts],
        out_specs=[rows(D_MODEL), pl.BlockSpec((None, N_EXPERTS, td), lambda b, l: (b, 0, l))],
        out_shape=[jax.ShapeDtypeStruct((bsz, length, D_MODEL), F32),
                   jax.ShapeDtypeStruct((bsz, N_EXPERTS, length), F32)],
        compiler_params=pltpu.CompilerParams(dimension_semantics=("arbitrary", "arbitrary"),
                                             vmem_limit_bytes=VMEM_LIMIT_BYTES),
        name="postmix",
    )(x, og, om, gate, mem_k, mem_v, *consts)


def _moe_kernel(h_ref, g_ref, wgu_ref, wd_ref, wsgu_ref, wsd_ref, ln3g_ref, ln3b_ref, y_o, xb, acc, *, eb):
    e = pl.program_id(1)

    @pl.when(e == 0)
    def _():
        x = h_ref[...].astype(BF16)
        xb[...] = x
        sgu = _dot(x, wsgu_ref[...])
        hid = _silu(sgu[:, 0:SHARED_FF]) * sgu[:, SHARED_FF:]
        acc[...] = _dot(hid.astype(BF16), wsd_ref[...])

    x = xb[...]
    gb = g_ref[...].astype(BF16)
    erow = lax.broadcasted_iota(jnp.int32, (N_EXPERTS, EXPERT_FF), 0)
    for i in range(eb):
        gu = _dot(x, wgu_ref[i])
        pick = jnp.where(erow == e * eb + i, 1.0, 0.0).astype(BF16)
        gcol = _dot(gb, pick)
        hid = _silu(gu[:, 0:EXPERT_FF]) * gu[:, EXPERT_FF:] * gcol
        acc[...] += _dot(hid.astype(BF16), wd_ref[i])

    @pl.when(e == pl.num_programs(1) - 1)
    def _():
        y_o[...] = _layer_norm(DEEPNORM_ALPHA * h_ref[...] + acc[...], ln3g_ref[...], ln3b_ref[...])


def _moe(h2, gates, ew, tm, eb):
    n = h2.shape[0]
    consts = [ew['ws_gu'], ew['ws_down'], ew['ln3_g'], ew['ln3_b']]
    return pl.pallas_call(
        functools.partial(_moe_kernel, eb=eb),
        grid=(n // tm, N_EXPERTS // eb),
        in_specs=[pl.BlockSpec((tm, D_MODEL), lambda t, e: (t, 0)),
                  pl.BlockSpec((tm, N_EXPERTS), lambda t, e: (t, 0)),
                  pl.BlockSpec((eb, D_MODEL, 2 * EXPERT_FF), lambda t, e: (e, 0, 0)),
                  pl.BlockSpec((eb, EXPERT_FF, D_MODEL), lambda t, e: (e, 0, 0))]
                 + [_const_spec(c.shape) for c in consts],
        out_specs=pl.BlockSpec((tm, D_MODEL), lambda t, e: (t, 0)),
        out_shape=jax.ShapeDtypeStruct((n, D_MODEL), F32),
        scratch_shapes=[pltpu.VMEM((tm, D_MODEL), BF16), pltpu.VMEM((tm, D_MODEL), F32)],
        compiler_params=pltpu.CompilerParams(dimension_semantics=("arbitrary", "arbitrary"),
                                             vmem_limit_bytes=VMEM_LIMIT_BYTES),
        name="moe",
    )(h2, gates, ew['w_gu'], ew['w_down'], *consts)


def _prep_post_weights(ln_in_g, ln_in_b, w_proj_gdn, w_proj_mla, w_out, ln1_g, ln1_b, w_mq, w_mo, ln2_g, ln2_b,
                       w_router, router_bias):
    def row(v):
        return v.reshape(1, -1).astype(F32)

    wrt = w_router.T.reshape(N_GROUPS, GROUP_SIZE, D_MODEL).transpose(1, 0, 2).reshape(N_EXPERTS, D_MODEL)
    rb = router_bias.reshape(N_GROUPS, GROUP_SIZE).T.reshape(N_EXPERTS, 1)
    return dict(ln_in_g=row(ln_in_g), ln_in_b=row(ln_in_b), w_proj_gdn=w_proj_gdn.astype(BF16),
                w_proj_mla=w_proj_mla.astype(BF16), w_out=w_out.astype(BF16), ln1_g=row(ln1_g), ln1_b=row(ln1_b),
                w_mq=w_mq.astype(BF16), w_mo=w_mo.astype(BF16), ln2_g=row(ln2_g), ln2_b=row(ln2_b),
                w_router_t=wrt.astype(F32), router_bias=rb.astype(F32))


def _prep_expert_weights(w_gate, w_up, w_down, ws_gate, ws_up, ws_down, ln3_g, ln3_b):
    return dict(w_gu=jnp.concatenate([w_gate, w_up], axis=-1).astype(BF16), w_down=w_down.astype(BF16),
                ws_gu=jnp.concatenate([ws_gate, ws_up], axis=-1).astype(BF16), ws_down=ws_down.astype(BF16),
                ln3_g=ln3_g.reshape(1, -1).astype(F32), ln3_b=ln3_b.reshape(1, -1).astype(F32))


def _gates_to_tokens(gt):
    bsz, _, length = gt.shape
    g = gt.reshape(bsz, GROUP_SIZE, N_GROUPS, length).transpose(0, 3, 2, 1)
    return g.reshape(bsz * length, N_EXPERTS)


def _tail(x, og, om, gate, mem_k, mem_v, pw, ew, td, tm, eb):
    bsz, length, _ = x.shape
    h2, gt = _postmix(x, og, om, gate, mem_k, mem_v, pw, td)
    y = _moe(h2.reshape(bsz * length, D_MODEL), _gates_to_tokens(gt), ew, tm, eb)
    return y.reshape(bsz, length, D_MODEL)


def kernel(x_prompt, x_sample, mem_prompt, state_gdn_conv, state_gdn_ssm, cache_mla_ckv, cache_mla_krope,
           cache_mem_k, cache_mem_v, ln_in_g, ln_in_b, w_in, w_conv, a_log, dt_bias, gdn_norm_g, w_proj_gdn,
           q_norm_g, w_uq, kv_norm_g, w_ukv, w_proj_mla, w_out, ln1_g, ln1_b, w_mq, w_mk, w_mv, w_mo,
           ln2_g, ln2_b, w_router, router_bias, w_gate, w_up, w_down, ws_gate, ws_up, ws_down, ln3_g, ln3_b):
    lyr = 0
    bsz, seq, _ = x_prompt.shape
    dbsz, dseq, _ = x_sample.shape
    past = cache_mla_ckv.shape[2]
    mw = _prep_mixer_weights(w_in[lyr], w_conv[lyr], a_log[lyr], dt_bias[lyr], q_norm_g[lyr], w_uq[lyr],
                             kv_norm_g[lyr], w_ukv[lyr])
    pw = _prep_post_weights(ln_in_g, ln_in_b, w_proj_gdn[lyr], w_proj_mla[lyr], w_out[lyr], ln1_g[lyr], ln1_b[lyr],
                            w_mq[lyr], w_mo[lyr], ln2_g[lyr], ln2_b[lyr], w_router[lyr], router_bias[lyr])
    ew = _prep_expert_weights(w_gate[lyr], w_up[lyr], w_down[lyr], ws_gate[lyr], ws_up[lyr], ws_down[lyr],
                              ln3_g[lyr], ln3_b[lyr])
    wide = MEM_HEADS * MEM_HEAD_DIM

    conv0 = jnp.zeros((bsz, GDN_CONV - 1, GDN_CONV_CH), F32)
    s0 = jnp.zeros((bsz, GDN_HEADS, GDN_DK, GDN_DV), F32)
    (qkv, z, gate, qp, kn, krp, v, ckv_p, kr_p, conv_p, gbc, gbr) = _inproj(
        x_prompt, ln_in_g, ln_in_b, mw, conv0, 0, tl=min(256, seq))
    og, ssm_p = _gdn(qkv, gbc, gbr, z, s0, gdn_norm_g[lyr], chunk=CHUNK, tg=min(512, seq), nb=2)
    om = _flash_attention(qp, kn, krp, v, tq=min(512, seq), nh=2)
    mem_k, mem_v = _memproj(mem_prompt, w_mk[lyr], w_mv[lyr])
    y_p = _tail(x_prompt, og, om, gate, mem_k, mem_v, pw, ew, td=min(256, seq), tm=min(1024, bsz * seq), eb=2)

    (qkv, z, gate, qp, kn, krp, v, ckv_s, kr_s, conv_s, gbc, gbr) = _inproj(
        x_sample, ln_in_g, ln_in_b, mw, state_gdn_conv[lyr], past, tl=dseq)
    chunk_s = CHUNK if dseq % CHUNK == 0 else dseq
    og, ssm_s = _gdn(qkv, gbc, gbr, z, state_gdn_ssm[lyr], gdn_norm_g[lyr], chunk=chunk_s, tg=dseq, nb=2)
    kn_past, v_past = _kvup(cache_mla_ckv[lyr], mw['w_kv'], tr=min(1024, past))
    om = _cached_attention(qp, kn_past, cache_mla_krope[lyr], v_past, kn, krp, v)
    y_s = _tail(x_sample, og, om, gate, cache_mem_k[lyr].reshape(dbsz, -1, wide),
                cache_mem_v[lyr].reshape(dbsz, -1, wide), pw, ew, td=dseq, tm=dbsz * dseq, eb=2)

    return (y_p, y_s, conv_p[None], ssm_p[None], ckv_p[None], kr_p[None],
            mem_k.reshape(1, bsz, -1, MEM_HEADS, MEM_HEAD_DIM), mem_v.reshape(1, bsz, -1, MEM_HEADS, MEM_HEAD_DIM),
            conv_s[None], ssm_s[None], ckv_s[None], kr_s[None])
```

```python
import functools
import math

import jax
import jax.numpy as jnp
from jax import lax
from jax.experimental import pallas as pl
from jax.experimental.pallas import tpu as pltpu

F32 = jnp.float32
BF16 = jnp.bfloat16

D_MODEL = 1024
CHUNK = 64
GDN_HEADS = 8
GDN_DK = 128
GDN_DV = 128
GDN_CONV = 4
GDN_QK = GDN_HEADS * GDN_DK
GDN_CONV_CH = 2 * GDN_QK + GDN_HEADS * GDN_DV
MLA_HEADS = 8
MLA_Q_LORA = 384
MLA_KV_LORA = 256
MLA_NOPE = 128
MLA_ROPE = 64
MLA_V = 128
MLA_SCALE = (MLA_NOPE + MLA_ROPE) ** -0.5
MLA_SCALE_LOG2 = MLA_SCALE * math.log2(math.e)
MLA_QK_PAD = 256
ROPE_THETA = 10000.0
MEM_HEADS = 4
MEM_HEAD_DIM = 128
N_EXPERTS = 64
TOP_K = 8
N_GROUPS = 8
GROUP_SIZE = N_EXPERTS // N_GROUPS
TOPK_GROUPS = 4
EXPERT_FF = 256
SHARED_FF = 256
ROUTED_SCALE = 2.5
DEPTH = 1
DEEPNORM_ALPHA = (2.0 * DEPTH) ** 0.25
IN_WIDTHS = (GDN_CONV_CH, GDN_HEADS * GDN_DV, GDN_HEADS, GDN_HEADS, MLA_Q_LORA, MLA_KV_LORA, MLA_ROPE,
             D_MODEL, D_MODEL)

VMEM_LIMIT_BYTES = 56 * 1024 * 1024


def _dot(a, b, precision=None):
    return jnp.dot(a, b, preferred_element_type=F32, precision=precision)


def _dot_nt(a, b, precision=None):
    return lax.dot_general(a, b, (((1,), (1,)), ((), ())), preferred_element_type=F32, precision=precision)


def _dot_tn(a, b):
    return lax.dot_general(a, b, (((0,), (0,)), ((), ())), preferred_element_type=F32)


def _layer_norm(x, g, b, eps=1e-5):
    mu = jnp.mean(x, axis=-1, keepdims=True)
    xc = x - mu
    var = jnp.mean(xc * xc, axis=-1, keepdims=True)
    return xc * lax.rsqrt(var + eps) * g + b


def _rms_norm(x, g, eps=1e-6):
    return x * lax.rsqrt(jnp.mean(x * x, axis=-1, keepdims=True) + eps) * g


def _sigmoid(x):
    return 1.0 / (1.0 + jnp.exp(-x))


def _silu(x):
    return x * _sigmoid(x)


def _softplus(x):
    return jnp.maximum(x, 0.0) + jnp.log1p(jnp.exp(-jnp.abs(x)))


def _const_spec(shape):
    nd = len(shape)
    return pl.BlockSpec(shape, lambda *_: (0,) * nd, pipeline_mode=pl.Buffered(1))


def _inproj_kernel(x_ref, lng_ref, lnb_ref, wqkv_ref, wz_ref, wgate_ref, wcq_ref, wckv_ref, wkr_ref, wab_ref,
                   wabt_ref, convw_ref, conv0_ref, arow_ref, acol_ref, qng_ref, wuq_ref, wuqs_ref, kvng_ref,
                   wukv_ref, t1_ref, t2_ref,
                   qkv_o, z_o, gate_o, qp_o, kn_o, krp_o, v_o, ckv_o, kr_o, conv_o, gbc_o, gbr_o,
                   cbuf):
    tl = x_ref.shape[0]
    li = pl.program_id(1)

    h = _layer_norm(x_ref[...], lng_ref[...], lnb_ref[...])
    hb = h.astype(BF16)

    @pl.when(li == 0)
    def _():
        cbuf[0:8, :] = conv0_ref[...]

    cbuf[8:8 + tl, :] = _dot(hb, wqkv_ref[...])
    conv_o[...] = cbuf[tl + 5:tl + 8, :]
    for part in range(3):
        cols = slice(part * GDN_QK, (part + 1) * GDN_QK)
        y = convw_ref[0:1, cols] * cbuf[5:5 + tl, cols]
        for i in range(1, GDN_CONV):
            y = y + convw_ref[i:i + 1, cols] * cbuf[5 + i:5 + i + tl, cols]
        y = _silu(y)
        if part < 2:
            for hh in range(GDN_HEADS):
                hc = slice(hh * GDN_DK, (hh + 1) * GDN_DK)
                yh = y[:, hc]
                r = lax.rsqrt(jnp.sum(yh * yh, axis=-1, keepdims=True) + 1e-6)
                yh = yh * r
                if part == 0:
                    yh = yh * (GDN_DK ** -0.5)
                qkv_o[:, part * GDN_QK + hh * GDN_DK:part * GDN_QK + (hh + 1) * GDN_DK] = yh.astype(BF16)
        else:
            qkv_o[:, cols] = y.astype(BF16)
    cbuf[0:8, :] = cbuf[tl:tl + 8, :]

    z_o[...] = _dot(hb, wz_ref[...]).astype(BF16)
    gate_o[...] = _dot(hb, wgate_ref[...]).astype(BF16)

    ab = _dot(hb, wab_ref[...])
    arow = arow_ref[...]
    g_c = arow[0:1, :] * _softplus(ab[:, 0:GDN_HEADS] + arow[1:2, :])
    b_c = _sigmoid(ab[:, GDN_HEADS:2 * GDN_HEADS])
    gbc_o[:, 0:GDN_HEADS] = g_c
    gbc_o[:, GDN_HEADS:2 * GDN_HEADS] = b_c
    abt = _dot_nt(wabt_ref[...], hb)
    acol = acol_ref[...]
    gbr_o[0:GDN_HEADS, :] = acol[:, 0:1] * _softplus(abt[0:GDN_HEADS, :] + acol[:, 1:2])
    gbr_o[GDN_HEADS:2 * GDN_HEADS, :] = _sigmoid(abt[GDN_HEADS:2 * GDN_HEADS, :])

    t1 = t1_ref[...]
    t2 = t2_ref[...]

    cq = _rms_norm(_dot(hb, wcq_ref[...]), qng_ref[...]).astype(BF16)
    qm = _dot(cq, wuq_ref[...])
    qs = _dot(cq, wuqs_ref[...])
    for hh in range(MLA_HEADS):
        base = hh * MLA_QK_PAD
        qp_o[:, base:base + MLA_NOPE] = (qm[:, base:base + MLA_NOPE] * MLA_SCALE_LOG2).astype(BF16)
        hi = qm[:, base + MLA_NOPE:base + MLA_QK_PAD] * t1 + qs[:, hh * 128:(hh + 1) * 128] * t2
        qp_o[:, base + MLA_NOPE:base + MLA_QK_PAD] = (hi * MLA_SCALE_LOG2).astype(BF16)

    ckv = _rms_norm(_dot(hb, wckv_ref[...]), kvng_ref[...])
    ckv_o[...] = ckv
    kv = _dot(ckv.astype(BF16), wukv_ref[...])
    kn_o[...] = kv[:, 0:MLA_HEADS * MLA_NOPE].astype(BF16)
    v_o[...] = kv[:, MLA_HEADS * MLA_NOPE:].astype(BF16)
    krm = _dot(hb, wkr_ref[...])
    krp = krm[:, 0:128] * t1 + krm[:, 128:256] * t2
    kr_o[...] = krp[:, 0:MLA_ROPE]
    krp_o[...] = krp.astype(BF16)


def _swap_halves(w):
    half = w.shape[-1] // 2
    return jnp.concatenate([w[..., half:], w[..., :half]], axis=-1)


def _rope_tables(pos0, length):
    inv_freq = jnp.power(ROPE_THETA, -jnp.arange(0, MLA_ROPE, 2, dtype=F32) / MLA_ROPE)
    ang = (pos0 + jnp.arange(length)).astype(F32)[:, None] * inv_freq[None, :]
    c, s = jnp.cos(ang), jnp.sin(ang)
    zeros = jnp.zeros((length, 128 - MLA_ROPE), F32)
    return jnp.concatenate([c, c, zeros], axis=-1), jnp.concatenate([-s, s, zeros], axis=-1)


def _prep_mixer_weights(w_in, w_conv, a_log, dt_bias, q_norm_g, w_uq, kv_norm_g, w_ukv):
    splits = []
    off = 0
    for wd in IN_WIDTHS:
        splits.append(w_in[:, off:off + wd])
        off += wd
    w_qkv, w_z, w_a, w_b, w_cq, w_ckv, w_kr, w_gg, w_gm = splits
    z64 = jnp.zeros((D_MODEL, 128 - MLA_ROPE), F32)
    w_krg = jnp.concatenate([w_kr, z64, _swap_halves(w_kr), z64], axis=-1)
    w_ab = jnp.concatenate([w_a, w_b], axis=-1)
    wq = w_uq.reshape(MLA_Q_LORA, MLA_HEADS, MLA_NOPE + MLA_ROPE)
    nope, rope = wq[:, :, :MLA_NOPE], wq[:, :, MLA_NOPE:]
    zq = jnp.zeros((MLA_Q_LORA, MLA_HEADS, MLA_QK_PAD - MLA_NOPE - MLA_ROPE), F32)
    wq_main = jnp.concatenate([nope, rope, zq], axis=-1).reshape(MLA_Q_LORA, MLA_HEADS * MLA_QK_PAD)
    wq_swap = jnp.concatenate([_swap_halves(rope), zq], axis=-1).reshape(MLA_Q_LORA, MLA_HEADS * 128)
    wkv = w_ukv.reshape(MLA_KV_LORA, MLA_HEADS, MLA_NOPE + MLA_V)
    w_kv = jnp.concatenate([wkv[:, :, :MLA_NOPE].reshape(MLA_KV_LORA, -1),
                            wkv[:, :, MLA_NOPE:].reshape(MLA_KV_LORA, -1)], axis=-1)
    neg_a = -jnp.exp(a_log.astype(F32))
    arow = jnp.stack([neg_a, dt_bias.astype(F32)], axis=0)
    return dict(
        w_qkv=w_qkv.astype(BF16), w_z=w_z.astype(BF16),
        w_gate=jnp.concatenate([w_gg, w_gm], axis=-1).astype(BF16),
        w_cq=w_cq.astype(BF16), w_ckv=w_ckv.astype(BF16), w_krg=w_krg.astype(BF16),
        w_ab=w_ab.astype(BF16), w_abt=w_ab.T.astype(BF16),
        w_conv=w_conv.astype(F32), arow=arow, acol=arow.T,
        q_norm_g=q_norm_g.reshape(1, -1).astype(F32), wq_main=wq_main.astype(BF16), wq_swap=wq_swap.astype(BF16),
        kv_norm_g=kv_norm_g.reshape(1, -1).astype(F32), w_kv=w_kv.astype(BF16))


def _inproj(x, ln_g, ln_b, mw, conv_state, pos0, tl):
    bsz, length, _ = x.shape
    nl = length // tl
    t1, t2 = _rope_tables(pos0, length)
    conv0 = jnp.concatenate([jnp.zeros((bsz, 8 - (GDN_CONV - 1), GDN_CONV_CH), F32), conv_state.astype(F32)], axis=1)

    def rows(width, dtype):
        return (jax.ShapeDtypeStruct((bsz, length, width), dtype),
                pl.BlockSpec((None, tl, width), lambda b, l: (b, l, 0)))

    outs = [rows(GDN_CONV_CH, BF16), rows(GDN_HEADS * GDN_DV, BF16), rows(2 * D_MODEL, BF16),
            rows(MLA_HEADS * MLA_QK_PAD, BF16), rows(MLA_HEADS * MLA_NOPE, BF16), rows(128, BF16),
            rows(MLA_HEADS * MLA_V, BF16), rows(MLA_KV_LORA, F32), rows(MLA_ROPE, F32),
            (jax.ShapeDtypeStruct((bsz, GDN_CONV - 1, GDN_CONV_CH), F32),
             pl.BlockSpec((None, GDN_CONV - 1, GDN_CONV_CH), lambda b, l: (b, 0, 0))),
            rows(2 * GDN_HEADS, F32),
            (jax.ShapeDtypeStruct((bsz, 2 * GDN_HEADS, length), F32),
             pl.BlockSpec((None, 2 * GDN_HEADS, tl), lambda b, l: (b, 0, l)))]
    consts = [ln_g.reshape(1, -1), ln_b.reshape(1, -1), mw['w_qkv'], mw['w_z'], mw['w_gate'], mw['w_cq'],
              mw['w_ckv'], mw['w_krg'], mw['w_ab'], mw['w_abt'], mw['w_conv']]
    consts2 = [mw['arow'], mw['acol'], mw['q_norm_g'], mw['wq_main'], mw['wq_swap'], mw['kv_norm_g'], mw['w_kv']]
    in_specs = ([pl.BlockSpec((None, tl, D_MODEL), lambda b, l: (b, l, 0))]
                + [_const_spec(c.shape) for c in consts]
                + [pl.BlockSpec((None, 8, GDN_CONV_CH), lambda b, l: (b, 0, 0))]
                + [_const_spec(c.shape) for c in consts2]
                + [pl.BlockSpec((tl, 128), lambda b, l: (l, 0)), pl.BlockSpec((tl, 128), lambda b, l: (l, 0))])
    return pl.pallas_call(
        _inproj_kernel,
        grid=(bsz, nl),
        in_specs=in_specs,
        out_specs=[o[1] for o in outs],
        out_shape=[o[0] for o in outs],
        scratch_shapes=[pltpu.VMEM((tl + 8, GDN_CONV_CH), F32)],
        compiler_params=pltpu.CompilerParams(dimension_semantics=("arbitrary", "arbitrary"),
                                             vmem_limit_bytes=VMEM_LIMIT_BYTES),
        name="inproj",
    )(x, *consts, conv0, *consts2, t1, t2)


def _gdn_kernel(qkv_ref, gbc_ref, gbr_ref, z_ref, s0_ref, ng_ref, og_o, s_o, s_scr, *, chunk):
    tg = qkv_ref.shape[1]
    nchunks = tg // chunk
    li = pl.program_id(1)

    @pl.when(li == 0)
    def _():
        s_scr[...] = s0_ref[...]

    ri = lax.broadcasted_iota(jnp.int32, (chunk, chunk), 0)
    ci = lax.broadcasted_iota(jnp.int32, (chunk, chunk), 1)
    incl = ri >= ci
    strict = ri > ci
    tri = jnp.where(incl, 1.0, 0.0).astype(F32)
    eye = jnp.where(ri == ci, 1.0, 0.0).astype(F32)
    ng = ng_ref[...]
    n_sq = int(math.log2(chunk)) - 1

    nb = qkv_ref.shape[0]
    chains = [(g, hh) for g in range(nb) for hh in range(GDN_HEADS)]

    def chunk_body(c, carry):
        r0 = pl.multiple_of(c * chunk, chunk)
        rows = pl.ds(r0, chunk)
        gbc = [gbc_ref[g, rows, :] for g in range(nb)]
        gc_col = [_dot(tri, gbc[g], precision=lax.Precision.HIGHEST) for g in range(nb)]
        gc_row = [_dot_nt(gbr_ref[g, c], tri, precision=lax.Precision.HIGHEST) for g in range(nb)]
        q, k, qkk = {}, {}, {}
        for ch in chains:
            g, hh = ch
            q[ch] = qkv_ref[g, rows, hh * GDN_DK:(hh + 1) * GDN_DK]
            k[ch] = qkv_ref[g, rows, GDN_QK + hh * GDN_DK:GDN_QK + (hh + 1) * GDN_DK]
            qkk[ch] = _dot_nt(jnp.concatenate([q[ch], k[ch]], axis=0), k[ch])
        gcc, beta, gamma, g_end, qk, xp, t_inv = {}, {}, {}, {}, {}, {}, {}
        for ch in chains:
            g, hh = ch
            gcc[ch] = gc_col[g][:, hh:hh + 1]
            gcr = gc_row[g][hh:hh + 1, :]
            beta[ch] = gbc[g][:, GDN_HEADS + hh:GDN_HEADS + hh + 1]
            g_end[ch] = gc_col[g][chunk - 1:chunk, hh:hh + 1]
            gamma[ch] = jnp.exp(gcc[ch])
            decay = jnp.where(incl, jnp.exp(jnp.where(incl, gcc[ch] - gcr, 0.0)), 0.0)
            qk[ch] = (decay * qkk[ch][0:chunk, :]).astype(BF16)
            xp[ch] = jnp.where(strict, beta[ch] * decay * qkk[ch][chunk:2 * chunk, :], 0.0)
            t_inv[ch] = eye - xp[ch]
        for _ in range(n_sq):
            for ch in chains:
                xb = xp[ch].astype(BF16)
                xp[ch] = _dot(xb, xb)
            for ch in chains:
                t_inv[ch] = t_inv[ch] + _dot(t_inv[ch].astype(BF16), xp[ch].astype(BF16))
        sol, q_s, sb, s_old, k_dec = {}, {}, {}, {}, {}
        for ch in chains:
            g, hh = ch
            kf = k[ch].astype(F32)
            v = qkv_ref[g, rows, 2 * GDN_QK + hh * GDN_DV:2 * GDN_QK + (hh + 1) * GDN_DV].astype(F32)
            rhs = jnp.concatenate([beta[ch] * v, (beta[ch] * gamma[ch]) * kf], axis=-1).astype(BF16)
            sol[ch] = _dot(t_inv[ch].astype(BF16), rhs)
            k_dec[ch] = (jnp.exp(g_end[ch] - gcc[ch]) * kf).astype(BF16)
            s_old[ch] = s_scr[g, hh]
            sb[ch] = s_old[ch].astype(BF16)
            q_s[ch] = _dot((gamma[ch] * q[ch].astype(F32)).astype(BF16), sb[ch])
        ub = {}
        for ch in chains:
            u = sol[ch][:, :GDN_DV] - _dot(sol[ch][:, GDN_DV:].astype(BF16), sb[ch])
            ub[ch] = u.astype(BF16)
        for ch in chains:
            g, hh = ch
            o = q_s[ch] + _dot(qk[ch], ub[ch])
            s_scr[g, hh] = jnp.exp(g_end[ch]) * s_old[ch] + _dot_tn(k_dec[ch], ub[ch])
            zz = z_ref[g, rows, hh * GDN_DV:(hh + 1) * GDN_DV].astype(F32)
            og_o[g, rows, hh * GDN_DV:(hh + 1) * GDN_DV] = (_rms_norm(o, ng) * _silu(zz)).astype(BF16)
        return carry

    lax.fori_loop(0, nchunks, chunk_body, 0)
    s_o[...] = s_scr[...]


def _gdn(qkv, gbc, gbr, z, s0, norm_g, chunk, tg, nb):
    bsz, length, _ = qkv.shape
    nl = length // tg
    st_shape = (GDN_HEADS, GDN_DK, GDN_DV)
    gbr = gbr.reshape(bsz, 2 * GDN_HEADS, length // chunk, chunk).transpose(0, 2, 1, 3)
    return pl.pallas_call(
        functools.partial(_gdn_kernel, chunk=chunk),
        grid=(bsz // nb, nl),
        in_specs=[pl.BlockSpec((nb, tg, GDN_CONV_CH), lambda b, l: (b, l, 0)),
                  pl.BlockSpec((nb, tg, 2 * GDN_HEADS), lambda b, l: (b, l, 0)),
                  pl.BlockSpec((nb, tg // chunk, 2 * GDN_HEADS, chunk), lambda b, l: (b, l, 0, 0)),
                  pl.BlockSpec((nb, tg, GDN_HEADS * GDN_DV), lambda b, l: (b, l, 0)),
                  pl.BlockSpec((nb,) + st_shape, lambda b, l: (b, 0, 0, 0)),
                  _const_spec((1, GDN_DV))],
        out_specs=[pl.BlockSpec((nb, tg, GDN_HEADS * GDN_DV), lambda b, l: (b, l, 0)),
                   pl.BlockSpec((nb,) + st_shape, lambda b, l: (b, 0, 0, 0))],
        out_shape=[jax.ShapeDtypeStruct((bsz, length, GDN_HEADS * GDN_DV), BF16),
                   jax.ShapeDtypeStruct((bsz,) + st_shape, F32)],
        scratch_shapes=[pltpu.VMEM((nb,) + st_shape, F32)],
        compiler_params=pltpu.CompilerParams(dimension_semantics=("arbitrary", "arbitrary"),
                                             vmem_limit_bytes=VMEM_LIMIT_BYTES),
        name="gdn",
    )(qkv, gbc, gbr, z, s0.astype(F32), norm_g.reshape(1, -1).astype(F32))


def _flash_kernel(q_ref, kn_ref, krp_ref, v_ref, o_ref, kcat, *, tq, nh):
    qi = pl.program_id(2)

    @pl.when(qi == 0)
    def _():
        for j in range(nh):
            kcat[j, :, 0:MLA_NOPE] = kn_ref[:, j * MLA_NOPE:(j + 1) * MLA_NOPE]
            kcat[j, :, MLA_NOPE:MLA_QK_PAD] = krp_ref[...]

    qs = [q_ref[:, j * MLA_QK_PAD:(j + 1) * MLA_QK_PAD] for j in range(nh)]

    def tile(ki, carry, masked):
        rows = pl.ds(pl.multiple_of(ki * tq, tq), tq)
        ss = [_dot_nt(qs[j], kcat[j, rows, :]) for j in range(nh)]
        if masked:
            ri = lax.broadcasted_iota(jnp.int32, (tq, tq), 0) // CHUNK
            ci = lax.broadcasted_iota(jnp.int32, (tq, tq), 1) // CHUNK
            ss = [jnp.where(ci <= ri, s, -jnp.inf) for s in ss]
        ps, stats = [], []
        for j in range(nh):
            m, l, _ = carry[j]
            m_new = jnp.maximum(m, jnp.max(ss[j], axis=-1, keepdims=True))
            p = jnp.exp2(ss[j] - m_new)
            alpha = jnp.exp2(m - m_new)
            stats.append((m_new, alpha * l + jnp.sum(p, axis=-1, keepdims=True), alpha))
            ps.append(p.astype(BF16))
        out = []
        for j in range(nh):
            m_new, l_new, alpha = stats[j]
            acc = alpha * carry[j][2] + _dot(ps[j], v_ref[rows, j * MLA_V:(j + 1) * MLA_V])
            out.append((m_new, l_new, acc))
        return tuple(out)

    init = tuple((jnp.full((tq, 1), -jnp.inf, F32), jnp.zeros((tq, 1), F32), jnp.zeros((tq, MLA_V), F32))
                 for _ in range(nh))
    carry = lax.fori_loop(0, qi, lambda ki, c: tile(ki, c, False), init)
    carry = tile(qi, carry, True)
    for j in range(nh):
        _, l, acc = carry[j]
        o_ref[:, j * MLA_V:(j + 1) * MLA_V] = (acc / l).astype(BF16)


def _flash_attention(qp, kn, krp, v, tq, nh):
    bsz, length, _ = qp.shape
    nq = length // tq
    return pl.pallas_call(
        functools.partial(_flash_kernel, tq=tq, nh=nh),
        grid=(bsz, MLA_HEADS // nh, nq),
        in_specs=[pl.BlockSpec((None, tq, nh * MLA_QK_PAD), lambda b, h, i: (b, i, h)),
                  pl.BlockSpec((None, length, nh * MLA_NOPE), lambda b, h, i: (b, 0, h)),
                  pl.BlockSpec((None, length, 128), lambda b, h, i: (b, 0, 0)),
                  pl.BlockSpec((None, length, nh * MLA_V), lambda b, h, i: (b, 0, h))],
        out_specs=pl.BlockSpec((None, tq, nh * MLA_V), lambda b, h, i: (b, i, h)),
        out_shape=jax.ShapeDtypeStruct((bsz, length, MLA_HEADS * MLA_V), BF16),
        scratch_shapes=[pltpu.VMEM((nh, length, MLA_QK_PAD), BF16)],
        compiler_params=pltpu.CompilerParams(dimension_semantics=("arbitrary", "arbitrary", "arbitrary"),
                                             vmem_limit_bytes=VMEM_LIMIT_BYTES),
        name="mla_flash",
    )(qp, kn, krp, v)


def _kvup_kernel(c_ref, w_ref, kn_o, v_o):
    kv = _dot(c_ref[...].astype(BF16), w_ref[...])
    kn_o[...] = kv[:, 0:MLA_HEADS * MLA_NOPE].astype(BF16)
    v_o[...] = kv[:, MLA_HEADS * MLA_NOPE:].astype(BF16)


def _kvup(ckv, w_kv, tr):
    bsz, length, _ = ckv.shape
    wide = MLA_HEADS * MLA_NOPE
    return pl.pallas_call(
        _kvup_kernel,
        grid=(bsz, length // tr),
        in_specs=[pl.BlockSpec((None, tr, MLA_KV_LORA), lambda b, l: (b, l, 0)), _const_spec(w_kv.shape)],
        out_specs=[pl.BlockSpec((None, tr, wide), lambda b, l: (b, l, 0)),
                   pl.BlockSpec((None, tr, wide), lambda b, l: (b, l, 0))],
        out_shape=[jax.ShapeDtypeStruct((bsz, length, wide), BF16), jax.ShapeDtypeStruct((bsz, length, wide), BF16)],
        compiler_params=pltpu.CompilerParams(dimension_semantics=("arbitrary", "arbitrary"),
                                             vmem_limit_bytes=VMEM_LIMIT_BYTES),
        name="kv_up_cached",
    )(ckv, w_kv)


def _cached_attn_kernel(q_ref, knp_ref, krp_ref, vp_ref, knn_ref, krn_ref, vn_ref, o_ref, *, past):
    lq = q_ref.shape[0]
    q_chunk = (past + lax.broadcasted_iota(jnp.int32, (lq, past), 0)) // CHUNK
    vis_p = lax.broadcasted_iota(jnp.int32, (lq, past), 1) // CHUNK <= q_chunk
    q_chunk_n = (past + lax.broadcasted_iota(jnp.int32, (lq, lq), 0)) // CHUNK
    vis_n = (past + lax.broadcasted_iota(jnp.int32, (lq, lq), 1)) // CHUNK <= q_chunk_n
    kr_past = krp_ref[...].astype(BF16)
    kr_new = krn_ref[...]
    for hh in range(MLA_HEADS):
        base = hh * MLA_QK_PAD
        qn = q_ref[:, base:base + MLA_NOPE]
        qr = q_ref[:, base + MLA_NOPE:base + MLA_QK_PAD]
        hc = slice(hh * MLA_NOPE, (hh + 1) * MLA_NOPE)
        s_p = _dot_nt(qn, knp_ref[:, hc]) + _dot_nt(qr[:, 0:MLA_ROPE], kr_past)
        s_n = _dot_nt(qn, knn_ref[:, hc]) + _dot_nt(qr, kr_new)
        s_p = jnp.where(vis_p, s_p, -jnp.inf)
        s_n = jnp.where(vis_n, s_n, -jnp.inf)
        m = jnp.maximum(jnp.max(s_p, axis=-1, keepdims=True), jnp.max(s_n, axis=-1, keepdims=True))
        p_p = jnp.exp2(s_p - m)
        p_n = jnp.exp2(s_n - m)
        l = jnp.sum(p_p, axis=-1, keepdims=True) + jnp.sum(p_n, axis=-1, keepdims=True)
        vc = slice(hh * MLA_V, (hh + 1) * MLA_V)
        o = _dot(p_p.astype(BF16), vp_ref[:, vc]) + _dot(p_n.astype(BF16), vn_ref[:, vc])
        o_ref[:, vc] = (o / l).astype(BF16)


def _cached_attention(qp, kn_past, kr_past, v_past, kn_new, krp_new, v_new):
    bsz, lq, _ = qp.shape
    past = kn_past.shape[1]
    wide = MLA_HEADS * MLA_NOPE

    def per_b(rows, width):
        return pl.BlockSpec((None, rows, width), lambda b: (b, 0, 0))

    return pl.pallas_call(
        functools.partial(_cached_attn_kernel, past=past),
        grid=(bsz,),
        in_specs=[per_b(lq, MLA_HEADS * MLA_QK_PAD), per_b(past, wide), per_b(past, MLA_ROPE), per_b(past, wide),
                  per_b(lq, wide), per_b(lq, 128), per_b(lq, wide)],
        out_specs=per_b(lq, MLA_HEADS * MLA_V),
        out_shape=jax.ShapeDtypeStruct((bsz, lq, MLA_HEADS * MLA_V), BF16),
        compiler_params=pltpu.CompilerParams(dimension_semantics=("arbitrary",),
                                             vmem_limit_bytes=VMEM_LIMIT_BYTES),
        name="mla_cached",
    )(qp, kn_past, kr_past, v_past, kn_new, krp_new, v_new)


def _memproj_kernel(m_ref, wk_ref, wv_ref, k_o, v_o):
    mb = m_ref[...].astype(BF16)
    k_o[...] = _dot(mb, wk_ref[...])
    v_o[...] = _dot(mb, wv_ref[...])


def _memproj(mem, w_mk, w_mv):
    bsz, tokens, _ = mem.shape
    wide = MEM_HEADS * MEM_HEAD_DIM
    return pl.pallas_call(
        _memproj_kernel,
        grid=(bsz,),
        in_specs=[pl.BlockSpec((None, tokens, D_MODEL), lambda b: (b, 0, 0)),
                  _const_spec(w_mk.shape), _const_spec(w_mv.shape)],
        out_specs=[pl.BlockSpec((None, tokens, wide), lambda b: (b, 0, 0)),
                   pl.BlockSpec((None, tokens, wide), lambda b: (b, 0, 0))],
        out_shape=[jax.ShapeDtypeStruct((bsz, tokens, wide), F32), jax.ShapeDtypeStruct((bsz, tokens, wide), F32)],
        compiler_params=pltpu.CompilerParams(dimension_semantics=("arbitrary",),
                                             vmem_limit_bytes=VMEM_LIMIT_BYTES),
        name="mem_proj",
    )(mem, w_mk.astype(BF16), w_mv.astype(BF16))


def _route(logits_t, bias_col):
    scores = _sigmoid(logits_t)
    sel = scores + bias_col
    t = logits_t.shape[1]
    sc = [scores[j * N_GROUPS:(j + 1) * N_GROUPS, :] for j in range(GROUP_SIZE)]
    se = [sel[j * N_GROUPS:(j + 1) * N_GROUPS, :] for j in range(GROUP_SIZE)]
    top1 = se[0]
    top2 = jnp.full_like(top1, -jnp.inf)
    for j in range(1, GROUP_SIZE):
        top2 = jnp.maximum(top2, jnp.minimum(top1, se[j]))
        top1 = jnp.maximum(top1, se[j])
    gs = top1 + top2
    gidx = lax.broadcasted_iota(jnp.int32, (N_GROUPS, t), 0)
    beaten = jnp.zeros((N_GROUPS, t), F32)
    for g2 in range(N_GROUPS):
        row = gs[g2:g2 + 1, :]
        tie = jnp.where(gidx > g2, 1.0, 0.0)
        beaten = beaten + jnp.where(row > gs, 1.0, jnp.where(row == gs, tie, 0.0))
    keep = beaten < TOPK_GROUPS
    ms = [jnp.where(keep, se[j], -jnp.inf) for j in range(GROUP_SIZE)]
    cnt = [jnp.zeros((N_GROUPS, t), F32) for _ in range(GROUP_SIZE)]
    for j2 in range(GROUP_SIZE):
        for g2 in range(N_GROUPS):
            row = ms[j2][g2:g2 + 1, :]
            for j in range(GROUP_SIZE):
                if j2 < j:
                    tie = jnp.where(gidx >= g2, 1.0, 0.0)
                else:
                    tie = jnp.where(gidx > g2, 1.0, 0.0)
                cnt[j] = cnt[j] + jnp.where(row > ms[j], 1.0, jnp.where(row == ms[j], tie, 0.0))
    w = [jnp.where(cnt[j] < TOP_K, sc[j], 0.0) for j in range(GROUP_SIZE)]
    tot = w[0]
    for j in range(1, GROUP_SIZE):
        tot = tot + w[j]
    denom = jnp.sum(tot, axis=0, keepdims=True) + 1e-20
    return [w[j] / denom * ROUTED_SCALE for j in range(GROUP_SIZE)]


def _postmix_kernel(x_ref, og_ref, om_ref, gate_ref, mk_ref, mv_ref, lng_ref, lnb_ref, wpg_ref, wpm_ref, wout_ref,
                    ln1g_ref, ln1b_ref, wmq_ref, wmo_ref, ln2g_ref, ln2b_ref, wrt_ref, rb_ref,
                    h2_o, gt_o):
    h = _layer_norm(x_ref[...], lng_ref[...], lnb_ref[...])
    bg = _dot(og_ref[...], wpg_ref[...])
    bm = _dot(om_ref[...], wpm_ref[...])
    merged = (_sigmoid(gate_ref[:, 0:D_MODEL].astype(F32)) * bg
              + _sigmoid(gate_ref[:, D_MODEL:2 * D_MODEL].astype(F32)) * bm)
    mix = _dot(merged.astype(BF16), wout_ref[...])
    h1 = _layer_norm(DEEPNORM_ALPHA * h + mix, ln1g_ref[...], ln1b_ref[...])
    q = _dot(h1.astype(BF16), wmq_ref[...]).astype(BF16)
    outs = []
    for hh in range(MEM_HEADS):
        hc = slice(hh * MEM_HEAD_DIM, (hh + 1) * MEM_HEAD_DIM)
        s = _dot_nt(q[:, hc], mk_ref[:, hc].astype(BF16)) * (MEM_HEAD_DIM ** -0.5)
        p = jnp.exp(s - jnp.max(s, axis=-1, keepdims=True))
        p = p / jnp.sum(p, axis=-1, keepdims=True)
        outs.append(_dot(p.astype(BF16), mv_ref[:, hc].astype(BF16)).astype(BF16))
    o = jnp.concatenate(outs, axis=-1)
    h2 = _layer_norm(DEEPNORM_ALPHA * h1 + _dot(o, wmo_ref[...]), ln2g_ref[...], ln2b_ref[...])
    h2_o[...] = h2
    logits_t = _dot_nt(wrt_ref[...], h2, precision=lax.Precision.HIGHEST)
    gates = _route(logits_t, rb_ref[...])
    for j in range(GROUP_SIZE):
        for g in range(N_GROUPS):
            gt_o[g * GROUP_SIZE + j:g * GROUP_SIZE + j + 1, :] = gates[j][g:g + 1, :]


def _postmix(x, og, om, gate, mem_k, mem_v, pw, td):
    bsz, length, _ = x.shape
    nl = length // td
    tokens = mem_k.shape[1]
    wide = MEM_HEADS * MEM_HEAD_DIM
    consts = [pw['ln_in_g'], pw['ln_in_b'], pw['w_proj_gdn'], pw['w_proj_mla'], pw['w_out'], pw['ln1_g'],
              pw['ln1_b'], pw['w_mq'], pw['w_mo'], pw['ln2_g'], pw['ln2_b'], pw['w_router_t'], pw['router_bias']]

    def rows(width):
        return pl.BlockSpec((None, td, width), lambda b, l: (b, l, 0))

    return pl.pallas_call(
        _postmix_kernel,
        grid=(bsz, nl),
        in_specs=[rows(D_MODEL), rows(D_MODEL), rows(D_MODEL), rows(2 * D_MODEL),
                  pl.BlockSpec((None, tokens, wide), lambda b, l: (b, 0, 0)),
                  pl.BlockSpec((None, tokens, wide), lambda b, l: (b, 0, 0))]
                 + [_const_spec(c.shape) for c in consts],
        out_specs=[rows(D_MODEL), pl.BlockSpec((None, N_EXPERTS, td), lambda b, l: (b, 0, l))],
        out_shape=[jax.ShapeDtypeStruct((bsz, length, D_MODEL), F32),
                   jax.ShapeDtypeStruct((bsz, N_EXPERTS, length), F32)],
        compiler_params=pltpu.CompilerParams(dimension_semantics=("arbitrary", "arbitrary"),
                                             vmem_limit_bytes=VMEM_LIMIT_BYTES),
        name="postmix",
    )(x, og, om, gate, mem_k, mem_v, *consts)


def _moe_kernel(h_ref, g_ref, wgu_ref, wd_ref, wsgu_ref, wsd_ref, ln3g_ref, ln3b_ref, y_o, xb, acc, *, eb):
    e = pl.program_id(1)

    @pl.when(e == 0)
    def _():
        x = h_ref[...].astype(BF16)
        xb[...] = x
        sgu = _dot(x, wsgu_ref[...])
        hid = _silu(sgu[:, 0:SHARED_FF]) * sgu[:, SHARED_FF:]
        acc[...] = _dot(hid.astype(BF16), wsd_ref[...])

    x = xb[...]
    gb = g_ref[...].astype(BF16)
    erow = lax.broadcasted_iota(jnp.int32, (N_EXPERTS, EXPERT_FF), 0)
    for i in range(eb):
        gu = _dot(x, wgu_ref[i])
        pick = jnp.where(erow == e * eb + i, 1.0, 0.0).astype(BF16)
        gcol = _dot(gb, pick)
        hid = _silu(gu[:, 0:EXPERT_FF]) * gu[:, EXPERT_FF:] * gcol
        acc[...] += _dot(hid.astype(BF16), wd_ref[i])

    @pl.when(e == pl.num_programs(1) - 1)
    def _():
        y_o[...] = _layer_norm(DEEPNORM_ALPHA * h_ref[...] + acc[...], ln3g_ref[...], ln3b_ref[...])


def _moe(h2, gates, ew, tm, eb):
    n = h2.shape[0]
    consts = [ew['ws_gu'], ew['ws_down'], ew['ln3_g'], ew['ln3_b']]
    return pl.pallas_call(
        functools.partial(_moe_kernel, eb=eb),
        grid=(n // tm, N_EXPERTS // eb),
        in_specs=[pl.BlockSpec((tm, D_MODEL), lambda t, e: (t, 0)),
                  pl.BlockSpec((tm, N_EXPERTS), lambda t, e: (t, 0)),
                  pl.BlockSpec((eb, D_MODEL, 2 * EXPERT_FF), lambda t, e: (e, 0, 0)),
                  pl.BlockSpec((eb, EXPERT_FF, D_MODEL), lambda t, e: (e, 0, 0))]
                 + [_const_spec(c.shape) for c in consts],
        out_specs=pl.BlockSpec((tm, D_MODEL), lambda t, e: (t, 0)),
        out_shape=jax.ShapeDtypeStruct((n, D_MODEL), F32),
        scratch_shapes=[pltpu.VMEM((tm, D_MODEL), BF16), pltpu.VMEM((tm, D_MODEL), F32)],
        compiler_params=pltpu.CompilerParams(dimension_semantics=("arbitrary", "arbitrary"),
                                             vmem_limit_bytes=VMEM_LIMIT_BYTES),
        name="moe",
    )(h2, gates, ew['w_gu'], ew['w_down'], *consts)


MOE_ROUTED_TILE = 1024
MOE_ROUTED_SUB = 512
MOE_ROUTED_EXPERTS = 8
MOE_ROUTED_SLOTS = 96
MOE_VMEM_LIMIT_BYTES = 60 * 1024 * 1024


def _moe_routed_kernel(h_ref, g_ref, wgu_ref, wd_ref, wsgu_ref, wsd_ref, ln3g_ref, ln3b_ref, y_o,
                       xb, rank_scr, p_scr, o_scr, *, eb, rb, ts):
    e = pl.program_id(2)
    t = h_ref.shape[0]
    ns = t // ts

    @pl.when(e == 0)
    def _():
        x = h_ref[...].astype(BF16)
        xb[...] = x
        sgu = _dot(x, wsgu_ref[...])
        hid = _silu(sgu[:, 0:SHARED_FF]) * sgu[:, SHARED_FF:]
        y_o[...] = _dot(hid.astype(BF16), wsd_ref[...])
        routed = jnp.where(g_ref[...] > 0.0, 1.0, 0.0).astype(BF16)
        upto = jnp.where(lax.broadcasted_iota(jnp.int32, (ts, ts), 0) <= lax.broadcasted_iota(jnp.int32, (ts, ts), 1),
                         1.0, 0.0).astype(BF16)
        for s in range(ns):
            rank_scr[:, s * ts:(s + 1) * ts] = _dot(routed[:, s * ts:(s + 1) * ts], upto)

    e0 = pl.multiple_of(e * eb, eb)
    gblk = g_ref[pl.ds(e0, eb), :]
    rblk = rank_scr[pl.ds(e0, eb), :]
    npass = (jnp.max(rblk).astype(jnp.int32) + (rb - 1)) // rb
    slot = lax.broadcasted_iota(jnp.int32, (rb, ts), 0).astype(F32)

    def one_pass(k, carry):
        base = (k * rb).astype(F32)
        for i in range(eb):
            gi = gblk[i:i + 1, :]
            pos = jnp.where(gi > 0.0, rblk[i:i + 1, :] - 1.0 - base, -1.0)
            xgs, gates = [], []
            for s in range(ns):
                cs = slice(s * ts, (s + 1) * ts)
                hit = pos[:, cs] == slot
                pb = jnp.where(hit, 1.0, 0.0).astype(BF16)
                p_scr[s, i * rb:(i + 1) * rb, :] = pb
                gates.append(jnp.sum(jnp.where(hit, gi[:, cs], 0.0), axis=1, keepdims=True))
                xgs.append(_dot(pb, xb[cs, :]).astype(BF16))
            gu = _dot(jnp.concatenate(xgs, axis=0), wgu_ref[i])
            hid = _silu(gu[:, 0:EXPERT_FF]) * gu[:, EXPERT_FF:] * jnp.concatenate(gates, axis=0)
            out = _dot(hid.astype(BF16), wd_ref[i]).astype(BF16)
            for s in range(ns):
                o_scr[s, i * rb:(i + 1) * rb, :] = out[s * rb:(s + 1) * rb, :]
        for s in range(ns):
            y_o[s * ts:(s + 1) * ts, :] += _dot_tn(p_scr[s], o_scr[s])
        return carry

    lax.fori_loop(0, npass, one_pass, 0)

    @pl.when(e == pl.num_programs(2) - 1)
    def _():
        y_o[...] = _layer_norm(DEEPNORM_ALPHA * h_ref[...] + y_o[...], ln3g_ref[...], ln3b_ref[...])


def _moe_routed(h2, gt, ew, tm, ts, eb, rb):
    bsz, length, _ = h2.shape
    consts = [ew['ws_gu'], ew['ws_down'], ew['ln3_g'], ew['ln3_b']]
    return pl.pallas_call(
        functools.partial(_moe_routed_kernel, eb=eb, rb=rb, ts=ts),
        grid=(bsz, length // tm, N_EXPERTS // eb),
        in_specs=[pl.BlockSpec((None, tm, D_MODEL), lambda b, t, e: (b, t, 0), pipeline_mode=pl.Buffered(1)),
                  pl.BlockSpec((None, N_EXPERTS, tm), lambda b, t, e: (b, 0, t)),
                  pl.BlockSpec((eb, D_MODEL, 2 * EXPERT_FF), lambda b, t, e: (e, 0, 0)),
                  pl.BlockSpec((eb, EXPERT_FF, D_MODEL), lambda b, t, e: (e, 0, 0))]
                 + [_const_spec(c.shape) for c in consts],
        out_specs=pl.BlockSpec((None, tm, D_MODEL), lambda b, t, e: (b, t, 0)),
        out_shape=jax.ShapeDtypeStruct((bsz, length, D_MODEL), F32),
        scratch_shapes=[pltpu.VMEM((tm, D_MODEL), BF16), pltpu.VMEM((N_EXPERTS, tm), F32),
                        pltpu.VMEM((tm // ts, eb * rb, ts), BF16), pltpu.VMEM((tm // ts, eb * rb, D_MODEL), BF16)],
        compiler_params=pltpu.CompilerParams(dimension_semantics=("arbitrary", "arbitrary", "arbitrary"),
                                             vmem_limit_bytes=MOE_VMEM_LIMIT_BYTES),
        name="moe_routed",
    )(h2, gt, ew['w_gu'], ew['w_down'], *consts)


def _prep_post_weights(ln_in_g, ln_in_b, w_proj_gdn, w_proj_mla, w_out, ln1_g, ln1_b, w_mq, w_mo, ln2_g, ln2_b,
                       w_router, router_bias):
    def row(v):
        return v.reshape(1, -1).astype(F32)

    wrt = w_router.T.reshape(N_GROUPS, GROUP_SIZE, D_MODEL).transpose(1, 0, 2).reshape(N_EXPERTS, D_MODEL)
    rb = router_bias.reshape(N_GROUPS, GROUP_SIZE).T.reshape(N_EXPERTS, 1)
    return dict(ln_in_g=row(ln_in_g), ln_in_b=row(ln_in_b), w_proj_gdn=w_proj_gdn.astype(BF16),
                w_proj_mla=w_proj_mla.astype(BF16), w_out=w_out.astype(BF16), ln1_g=row(ln1_g), ln1_b=row(ln1_b),
                w_mq=w_mq.astype(BF16), w_mo=w_mo.astype(BF16), ln2_g=row(ln2_g), ln2_b=row(ln2_b),
                w_router_t=wrt.astype(F32), router_bias=rb.astype(F32))


def _prep_expert_weights(w_gate, w_up, w_down, ws_gate, ws_up, ws_down, ln3_g, ln3_b):
    return dict(w_gu=jnp.concatenate([w_gate, w_up], axis=-1).astype(BF16), w_down=w_down.astype(BF16),
                ws_gu=jnp.concatenate([ws_gate, ws_up], axis=-1).astype(BF16), ws_down=ws_down.astype(BF16),
                ln3_g=ln3_g.reshape(1, -1).astype(F32), ln3_b=ln3_b.reshape(1, -1).astype(F32))


def _tail(x, og, om, gate, mem_k, mem_v, pw, ew, td):
    bsz, length, _ = x.shape
    h2, gt = _postmix(x, og, om, gate, mem_k, mem_v, pw, td)
    if length % MOE_ROUTED_TILE == 0:
        return _moe_routed(h2, gt, ew, MOE_ROUTED_TILE, MOE_ROUTED_SUB, MOE_ROUTED_EXPERTS, MOE_ROUTED_SLOTS)
    gates = gt.transpose(0, 2, 1).reshape(bsz * length, N_EXPERTS)
    y = _moe(h2.reshape(bsz * length, D_MODEL), gates, ew, bsz * length, 2)
    return y.reshape(bsz, length, D_MODEL)


def kernel(x_prompt, x_sample, mem_prompt, state_gdn_conv, state_gdn_ssm, cache_mla_ckv, cache_mla_krope,
           cache_mem_k, cache_mem_v, ln_in_g, ln_in_b, w_in, w_conv, a_log, dt_bias, gdn_norm_g, w_proj_gdn,
           q_norm_g, w_uq, kv_norm_g, w_ukv, w_proj_mla, w_out, ln1_g, ln1_b, w_mq, w_mk, w_mv, w_mo,
           ln2_g, ln2_b, w_router, router_bias, w_gate, w_up, w_down, ws_gate, ws_up, ws_down, ln3_g, ln3_b):
    lyr = 0
    bsz, seq, _ = x_prompt.shape
    dbsz, dseq, _ = x_sample.shape
    past = cache_mla_ckv.shape[2]
    mw = _prep_mixer_weights(w_in[lyr], w_conv[lyr], a_log[lyr], dt_bias[lyr], q_norm_g[lyr], w_uq[lyr],
                             kv_norm_g[lyr], w_ukv[lyr])
    pw = _prep_post_weights(ln_in_g, ln_in_b, w_proj_gdn[lyr], w_proj_mla[lyr], w_out[lyr], ln1_g[lyr], ln1_b[lyr],
                            w_mq[lyr], w_mo[lyr], ln2_g[lyr], ln2_b[lyr], w_router[lyr], router_bias[lyr])
    ew = _prep_expert_weights(w_gate[lyr], w_up[lyr], w_down[lyr], ws_gate[lyr], ws_up[lyr], ws_down[lyr],
                              ln3_g[lyr], ln3_b[lyr])
    wide = MEM_HEADS * MEM_HEAD_DIM

    conv0 = jnp.zeros((bsz, GDN_CONV - 1, GDN_CONV_CH), F32)
    s0 = jnp.zeros((bsz, GDN_HEADS, GDN_DK, GDN_DV), F32)
    (qkv, z, gate, qp, kn, krp, v, ckv_p, kr_p, conv_p, gbc, gbr) = _inproj(
        x_prompt, ln_in_g, ln_in_b, mw, conv0, 0, tl=min(256, seq))
    og, ssm_p = _gdn(qkv, gbc, gbr, z, s0, gdn_norm_g[lyr], chunk=CHUNK, tg=min(256, seq), nb=math.gcd(bsz, 4))
    om = _flash_attention(qp, kn, krp, v, tq=min(1024, seq), nh=2)
    mem_k, mem_v = _memproj(mem_prompt, w_mk[lyr], w_mv[lyr])
    y_p = _tail(x_prompt, og, om, gate, mem_k, mem_v, pw, ew, td=min(256, seq))

    (qkv, z, gate, qp, kn, krp, v, ckv_s, kr_s, conv_s, gbc, gbr) = _inproj(
        x_sample, ln_in_g, ln_in_b, mw, state_gdn_conv[lyr], past, tl=dseq)
    chunk_s = CHUNK if dseq % CHUNK == 0 else dseq
    og, ssm_s = _gdn(qkv, gbc, gbr, z, state_gdn_ssm[lyr], gdn_norm_g[lyr], chunk=chunk_s, tg=dseq, nb=math.gcd(dbsz, 2))
    kn_past, v_past = _kvup(cache_mla_ckv[lyr], mw['w_kv'], tr=min(1024, past))
    om = _cached_attention(qp, kn_past, cache_mla_krope[lyr], v_past, kn, krp, v)
    y_s = _tail(x_sample, og, om, gate, cache_mem_k[lyr].reshape(dbsz, -1, wide),
                cache_mem_v[lyr].reshape(dbsz, -1, wide), pw, ew, td=dseq)

    return (y_p, y_s, conv_p[None], ssm_p[None], ckv_p[None], kr_p[None],
            mem_k.reshape(1, bsz, -1, MEM_HEADS, MEM_HEAD_DIM), mem_v.reshape(1, bsz, -1, MEM_HEADS, MEM_HEAD_DIM),
            conv_s[None], ssm_s[None], ckv_s[None], kr_s[None])
```

```python
import functools
import math

import jax
import jax.numpy as jnp
from jax import lax
from jax.experimental import pallas as pl
from jax.experimental.pallas import tpu as pltpu

F32 = jnp.float32
BF16 = jnp.bfloat16

D_MODEL = 1024
CHUNK = 64
GDN_HEADS = 8
GDN_DK = 128
GDN_DV = 128
GDN_CONV = 4
GDN_QK = GDN_HEADS * GDN_DK
GDN_CONV_CH = 2 * GDN_QK + GDN_HEADS * GDN_DV
MLA_HEADS = 8
MLA_Q_LORA = 384
MLA_KV_LORA = 256
MLA_NOPE = 128
MLA_ROPE = 64
MLA_V = 128
MLA_SCALE = (MLA_NOPE + MLA_ROPE) ** -0.5
MLA_SCALE_LOG2 = MLA_SCALE * math.log2(math.e)
MLA_QK_PAD = 256
ROPE_THETA = 10000.0
MEM_HEADS = 4
MEM_HEAD_DIM = 128
N_EXPERTS = 64
TOP_K = 8
N_GROUPS = 8
GROUP_SIZE = N_EXPERTS // N_GROUPS
TOPK_GROUPS = 4
EXPERT_FF = 256
SHARED_FF = 256
ROUTED_SCALE = 2.5
DEPTH = 1
DEEPNORM_ALPHA = (2.0 * DEPTH) ** 0.25
IN_WIDTHS = (GDN_CONV_CH, GDN_HEADS * GDN_DV, GDN_HEADS, GDN_HEADS, MLA_Q_LORA, MLA_KV_LORA, MLA_ROPE,
             D_MODEL, D_MODEL)

VMEM_LIMIT_BYTES = 56 * 1024 * 1024


def _dot(a, b, precision=None):
    return jnp.dot(a, b, preferred_element_type=F32, precision=precision)


def _dot_nt(a, b, precision=None):
    return lax.dot_general(a, b, (((1,), (1,)), ((), ())), preferred_element_type=F32, precision=precision)


def _dot_tn(a, b):
    return lax.dot_general(a, b, (((0,), (0,)), ((), ())), preferred_element_type=F32)


def _layer_norm(x, g, b, eps=1e-5):
    mu = jnp.mean(x, axis=-1, keepdims=True)
    xc = x - mu
    var = jnp.mean(xc * xc, axis=-1, keepdims=True)
    return xc * lax.rsqrt(var + eps) * g + b


def _rms_norm(x, g, eps=1e-6):
    return x * lax.rsqrt(jnp.mean(x * x, axis=-1, keepdims=True) + eps) * g


def _sigmoid(x):
    return 1.0 / (1.0 + jnp.exp(-x))


def _silu(x):
    return x * _sigmoid(x)


def _softplus(x):
    return jnp.maximum(x, 0.0) + jnp.log1p(jnp.exp(-jnp.abs(x)))


def _const_spec(shape):
    nd = len(shape)
    return pl.BlockSpec(shape, lambda *_: (0,) * nd, pipeline_mode=pl.Buffered(1))


def _inproj_kernel(x_ref, lng_ref, lnb_ref, wqkv_ref, wz_ref, wgate_ref, wcq_ref, wckv_ref, wkr_ref, wab_ref,
                   wabt_ref, convw_ref, conv0_ref, arow_ref, acol_ref, qng_ref, wuq_ref, wuqs_ref, kvng_ref,
                   wukv_ref, t1_ref, t2_ref,
                   qkv_o, z_o, gate_o, qp_o, kn_o, krp_o, v_o, ckv_o, kr_o, conv_o, gbc_o, gbr_o,
                   cbuf):
    tl = x_ref.shape[0]
    li = pl.program_id(1)

    h = _layer_norm(x_ref[...], lng_ref[...], lnb_ref[...])
    hb = h.astype(BF16)

    @pl.when(li == 0)
    def _():
        cbuf[0:8, :] = conv0_ref[...]

    cbuf[8:8 + tl, :] = _dot(hb, wqkv_ref[...])
    conv_o[...] = cbuf[tl + 5:tl + 8, :]
    for part in range(3):
        cols = slice(part * GDN_QK, (part + 1) * GDN_QK)
        y = convw_ref[0:1, cols] * cbuf[5:5 + tl, cols]
        for i in range(1, GDN_CONV):
            y = y + convw_ref[i:i + 1, cols] * cbuf[5 + i:5 + i + tl, cols]
        y = _silu(y)
        if part < 2:
            for hh in range(GDN_HEADS):
                hc = slice(hh * GDN_DK, (hh + 1) * GDN_DK)
                yh = y[:, hc]
                r = lax.rsqrt(jnp.sum(yh * yh, axis=-1, keepdims=True) + 1e-6)
                yh = yh * r
                if part == 0:
                    yh = yh * (GDN_DK ** -0.5)
                qkv_o[:, part * GDN_QK + hh * GDN_DK:part * GDN_QK + (hh + 1) * GDN_DK] = yh.astype(BF16)
        else:
            qkv_o[:, cols] = y.astype(BF16)
    cbuf[0:8, :] = cbuf[tl:tl + 8, :]

    z_o[...] = _dot(hb, wz_ref[...]).astype(BF16)
    gate_o[...] = _dot(hb, wgate_ref[...]).astype(BF16)

    ab = _dot(hb, wab_ref[...])
    arow = arow_ref[...]
    g_c = arow[0:1, :] * _softplus(ab[:, 0:GDN_HEADS] + arow[1:2, :])
    b_c = _sigmoid(ab[:, GDN_HEADS:2 * GDN_HEADS])
    gbc_o[:, 0:GDN_HEADS] = g_c
    gbc_o[:, GDN_HEADS:2 * GDN_HEADS] = b_c
    abt = _dot_nt(wabt_ref[...], hb)
    acol = acol_ref[...]
    gbr_o[0:GDN_HEADS, :] = acol[:, 0:1] * _softplus(abt[0:GDN_HEADS, :] + acol[:, 1:2])
    gbr_o[GDN_HEADS:2 * GDN_HEADS, :] = _sigmoid(abt[GDN_HEADS:2 * GDN_HEADS, :])

    t1 = t1_ref[...]
    t2 = t2_ref[...]

    cq = _rms_norm(_dot(hb, wcq_ref[...]), qng_ref[...]).astype(BF16)
    qm = _dot(cq, wuq_ref[...])
    qs = _dot(cq, wuqs_ref[...])
    for hh in range(MLA_HEADS):
        base = hh * MLA_QK_PAD
        qp_o[:, base:base + MLA_NOPE] = (qm[:, base:base + MLA_NOPE] * MLA_SCALE_LOG2).astype(BF16)
        hi = qm[:, base + MLA_NOPE:base + MLA_QK_PAD] * t1 + qs[:, hh * 128:(hh + 1) * 128] * t2
        qp_o[:, base + MLA_NOPE:base + MLA_QK_PAD] = (hi * MLA_SCALE_LOG2).astype(BF16)

    ckv = _rms_norm(_dot(hb, wckv_ref[...]), kvng_ref[...])
    ckv_o[...] = ckv
    kv = _dot(ckv.astype(BF16), wukv_ref[...])
    kn_o[...] = kv[:, 0:MLA_HEADS * MLA_NOPE].astype(BF16)
    v_o[...] = kv[:, MLA_HEADS * MLA_NOPE:].astype(BF16)
    krm = _dot(hb, wkr_ref[...])
    krp = krm[:, 0:128] * t1 + krm[:, 128:256] * t2
    kr_o[...] = krp[:, 0:MLA_ROPE]
    krp_o[...] = krp.astype(BF16)


def _swap_halves(w):
    half = w.shape[-1] // 2
    return jnp.concatenate([w[..., half:], w[..., :half]], axis=-1)


def _rope_tables(pos0, length):
    inv_freq = jnp.power(ROPE_THETA, -jnp.arange(0, MLA_ROPE, 2, dtype=F32) / MLA_ROPE)
    ang = (pos0 + jnp.arange(length)).astype(F32)[:, None] * inv_freq[None, :]
    c, s = jnp.cos(ang), jnp.sin(ang)
    zeros = jnp.zeros((length, 128 - MLA_ROPE), F32)
    return jnp.concatenate([c, c, zeros], axis=-1), jnp.concatenate([-s, s, zeros], axis=-1)


def _prep_mixer_weights(w_in, w_conv, a_log, dt_bias, q_norm_g, w_uq, kv_norm_g, w_ukv):
    splits = []
    off = 0
    for wd in IN_WIDTHS:
        splits.append(w_in[:, off:off + wd])
        off += wd
    w_qkv, w_z, w_a, w_b, w_cq, w_ckv, w_kr, w_gg, w_gm = splits
    z64 = jnp.zeros((D_MODEL, 128 - MLA_ROPE), F32)
    w_krg = jnp.concatenate([w_kr, z64, _swap_halves(w_kr), z64], axis=-1)
    w_ab = jnp.concatenate([w_a, w_b], axis=-1)
    wq = w_uq.reshape(MLA_Q_LORA, MLA_HEADS, MLA_NOPE + MLA_ROPE)
    nope, rope = wq[:, :, :MLA_NOPE], wq[:, :, MLA_NOPE:]
    zq = jnp.zeros((MLA_Q_LORA, MLA_HEADS, MLA_QK_PAD - MLA_NOPE - MLA_ROPE), F32)
    wq_main = jnp.concatenate([nope, rope, zq], axis=-1).reshape(MLA_Q_LORA, MLA_HEADS * MLA_QK_PAD)
    wq_swap = jnp.concatenate([_swap_halves(rope), zq], axis=-1).reshape(MLA_Q_LORA, MLA_HEADS * 128)
    wkv = w_ukv.reshape(MLA_KV_LORA, MLA_HEADS, MLA_NOPE + MLA_V)
    w_kv = jnp.concatenate([wkv[:, :, :MLA_NOPE].reshape(MLA_KV_LORA, -1),
                            wkv[:, :, MLA_NOPE:].reshape(MLA_KV_LORA, -1)], axis=-1)
    neg_a = -jnp.exp(a_log.astype(F32))
    arow = jnp.stack([neg_a, dt_bias.astype(F32)], axis=0)
    return dict(
        w_qkv=w_qkv.astype(BF16), w_z=w_z.astype(BF16),
        w_gate=jnp.concatenate([w_gg, w_gm], axis=-1).astype(BF16),
        w_cq=w_cq.astype(BF16), w_ckv=w_ckv.astype(BF16), w_krg=w_krg.astype(BF16),
        w_ab=w_ab.astype(BF16), w_abt=w_ab.T.astype(BF16),
        w_conv=w_conv.astype(F32), arow=arow, acol=arow.T,
        q_norm_g=q_norm_g.reshape(1, -1).astype(F32), wq_main=wq_main.astype(BF16), wq_swap=wq_swap.astype(BF16),
        kv_norm_g=kv_norm_g.reshape(1, -1).astype(F32), w_kv=w_kv.astype(BF16))


def _inproj(x, ln_g, ln_b, mw, conv_state, pos0, tl):
    bsz, length, _ = x.shape
    nl = length // tl
    t1, t2 = _rope_tables(pos0, length)
    conv0 = jnp.concatenate([jnp.zeros((bsz, 8 - (GDN_CONV - 1), GDN_CONV_CH), F32), conv_state.astype(F32)], axis=1)

    def rows(width, dtype):
        return (jax.ShapeDtypeStruct((bsz, length, width), dtype),
                pl.BlockSpec((None, tl, width), lambda b, l: (b, l, 0)))

    outs = [rows(GDN_CONV_CH, BF16), rows(GDN_HEADS * GDN_DV, BF16), rows(2 * D_MODEL, BF16),
            rows(MLA_HEADS * MLA_QK_PAD, BF16), rows(MLA_HEADS * MLA_NOPE, BF16), rows(128, BF16),
            rows(MLA_HEADS * MLA_V, BF16), rows(MLA_KV_LORA, F32), rows(MLA_ROPE, F32),
            (jax.ShapeDtypeStruct((bsz, GDN_CONV - 1, GDN_CONV_CH), F32),
             pl.BlockSpec((None, GDN_CONV - 1, GDN_CONV_CH), lambda b, l: (b, 0, 0))),
            rows(2 * GDN_HEADS, F32),
            (jax.ShapeDtypeStruct((bsz, 2 * GDN_HEADS, length), F32),
             pl.BlockSpec((None, 2 * GDN_HEADS, tl), lambda b, l: (b, 0, l)))]
    consts = [ln_g.reshape(1, -1), ln_b.reshape(1, -1), mw['w_qkv'], mw['w_z'], mw['w_gate'], mw['w_cq'],
              mw['w_ckv'], mw['w_krg'], mw['w_ab'], mw['w_abt'], mw['w_conv']]
    consts2 = [mw['arow'], mw['acol'], mw['q_norm_g'], mw['wq_main'], mw['wq_swap'], mw['kv_norm_g'], mw['w_kv']]
    in_specs = ([pl.BlockSpec((None, tl, D_MODEL), lambda b, l: (b, l, 0))]
                + [_const_spec(c.shape) for c in consts]
                + [pl.BlockSpec((None, 8, GDN_CONV_CH), lambda b, l: (b, 0, 0))]
                + [_const_spec(c.shape) for c in consts2]
                + [pl.BlockSpec((tl, 128), lambda b, l: (l, 0)), pl.BlockSpec((tl, 128), lambda b, l: (l, 0))])
    return pl.pallas_call(
        _inproj_kernel,
        grid=(bsz, nl),
        in_specs=in_specs,
        out_specs=[o[1] for o in outs],
        out_shape=[o[0] for o in outs],
        scratch_shapes=[pltpu.VMEM((tl + 8, GDN_CONV_CH), F32)],
        compiler_params=pltpu.CompilerParams(dimension_semantics=("arbitrary", "arbitrary"),
                                             vmem_limit_bytes=VMEM_LIMIT_BYTES),
        name="inproj",
    )(x, *consts, conv0, *consts2, t1, t2)


def _gdn_kernel(qkv_ref, gbc_ref, gbr_ref, z_ref, s0_ref, ng_ref, og_o, s_o, s_scr, *, chunk):
    tg = qkv_ref.shape[1]
    nchunks = tg // chunk
    li = pl.program_id(1)

    @pl.when(li == 0)
    def _():
        s_scr[...] = s0_ref[...]

    ri = lax.broadcasted_iota(jnp.int32, (chunk, chunk), 0)
    ci = lax.broadcasted_iota(jnp.int32, (chunk, chunk), 1)
    incl = ri >= ci
    strict = ri > ci
    tri = jnp.where(incl, 1.0, 0.0).astype(F32)
    eye = jnp.where(ri == ci, 1.0, 0.0).astype(F32)
    ng = ng_ref[...]
    n_sq = int(math.log2(chunk)) - 1

    nb = qkv_ref.shape[0]
    chains = [(g, hh) for g in range(nb) for hh in range(GDN_HEADS)]

    def chunk_body(c, carry):
        r0 = pl.multiple_of(c * chunk, chunk)
        rows = pl.ds(r0, chunk)
        gbc = [gbc_ref[g, rows, :] for g in range(nb)]
        gc_col = [_dot(tri, gbc[g], precision=lax.Precision.HIGHEST) for g in range(nb)]
        gc_row = [_dot_nt(gbr_ref[g, c], tri, precision=lax.Precision.HIGHEST) for g in range(nb)]
        q, k, qkk = {}, {}, {}
        for ch in chains:
            g, hh = ch
            q[ch] = qkv_ref[g, rows, hh * GDN_DK:(hh + 1) * GDN_DK]
            k[ch] = qkv_ref[g, rows, GDN_QK + hh * GDN_DK:GDN_QK + (hh + 1) * GDN_DK]
            qkk[ch] = _dot_nt(jnp.concatenate([q[ch], k[ch]], axis=0), k[ch])
        gcc, beta, gamma, g_end, qk, xp, t_inv = {}, {}, {}, {}, {}, {}, {}
        for ch in chains:
            g, hh = ch
            gcc[ch] = gc_col[g][:, hh:hh + 1]
            gcr = gc_row[g][hh:hh + 1, :]
            beta[ch] = gbc[g][:, GDN_HEADS + hh:GDN_HEADS + hh + 1]
            g_end[ch] = gc_col[g][chunk - 1:chunk, hh:hh + 1]
            gamma[ch] = jnp.exp(gcc[ch])
            decay = jnp.where(incl, jnp.exp(jnp.where(incl, gcc[ch] - gcr, 0.0)), 0.0)
            qk[ch] = (decay * qkk[ch][0:chunk, :]).astype(BF16)
            xp[ch] = jnp.where(strict, beta[ch] * decay * qkk[ch][chunk:2 * chunk, :], 0.0)
            t_inv[ch] = eye - xp[ch]
        for _ in range(n_sq):
            for ch in chains:
                xb = xp[ch].astype(BF16)
                xp[ch] = _dot(xb, xb)
            for ch in chains:
                t_inv[ch] = t_inv[ch] + _dot(t_inv[ch].astype(BF16), xp[ch].astype(BF16))
        sol, q_s, sb, s_old, k_dec = {}, {}, {}, {}, {}
        for ch in chains:
            g, hh = ch
            kf = k[ch].astype(F32)
            v = qkv_ref[g, rows, 2 * GDN_QK + hh * GDN_DV:2 * GDN_QK + (hh + 1) * GDN_DV].astype(F32)
            rhs = jnp.concatenate([beta[ch] * v, (beta[ch] * gamma[ch]) * kf], axis=-1).astype(BF16)
            sol[ch] = _dot(t_inv[ch].astype(BF16), rhs)
            k_dec[ch] = (jnp.exp(g_end[ch] - gcc[ch]) * kf).astype(BF16)
            s_old[ch] = s_scr[g, hh]
            sb[ch] = s_old[ch].astype(BF16)
            q_s[ch] = _dot((gamma[ch] * q[ch].astype(F32)).astype(BF16), sb[ch])
        ub = {}
        for ch in chains:
            u = sol[ch][:, :GDN_DV] - _dot(sol[ch][:, GDN_DV:].astype(BF16), sb[ch])
            ub[ch] = u.astype(BF16)
        for ch in chains:
            g, hh = ch
            o = q_s[ch] + _dot(qk[ch], ub[ch])
            s_scr[g, hh] = jnp.exp(g_end[ch]) * s_old[ch] + _dot_tn(k_dec[ch], ub[ch])
            zz = z_ref[g, rows, hh * GDN_DV:(hh + 1) * GDN_DV].astype(F32)
            og_o[g, rows, hh * GDN_DV:(hh + 1) * GDN_DV] = (_rms_norm(o, ng) * _silu(zz)).astype(BF16)
        return carry

    lax.fori_loop(0, nchunks, chunk_body, 0)
    s_o[...] = s_scr[...]


def _gdn(qkv, gbc, gbr, z, s0, norm_g, chunk, tg, nb):
    bsz, length, _ = qkv.shape
    nl = length // tg
    st_shape = (GDN_HEADS, GDN_DK, GDN_DV)
    gbr = gbr.reshape(bsz, 2 * GDN_HEADS, length // chunk, chunk).transpose(0, 2, 1, 3)
    return pl.pallas_call(
        functools.partial(_gdn_kernel, chunk=chunk),
        grid=(bsz // nb, nl),
        in_specs=[pl.BlockSpec((nb, tg, GDN_CONV_CH), lambda b, l: (b, l, 0)),
                  pl.BlockSpec((nb, tg, 2 * GDN_HEADS), lambda b, l: (b, l, 0)),
                  pl.BlockSpec((nb, tg // chunk, 2 * GDN_HEADS, chunk), lambda b, l: (b, l, 0, 0)),
                  pl.BlockSpec((nb, tg, GDN_HEADS * GDN_DV), lambda b, l: (b, l, 0)),
                  pl.BlockSpec((nb,) + st_shape, lambda b, l: (b, 0, 0, 0)),
                  _const_spec((1, GDN_DV))],
        out_specs=[pl.BlockSpec((nb, tg, GDN_HEADS * GDN_DV), lambda b, l: (b, l, 0)),
                   pl.BlockSpec((nb,) + st_shape, lambda b, l: (b, 0, 0, 0))],
        out_shape=[jax.ShapeDtypeStruct((bsz, length, GDN_HEADS * GDN_DV), BF16),
                   jax.ShapeDtypeStruct((bsz,) + st_shape, F32)],
        scratch_shapes=[pltpu.VMEM((nb,) + st_shape, F32)],
        compiler_params=pltpu.CompilerParams(dimension_semantics=("arbitrary", "arbitrary"),
                                             vmem_limit_bytes=VMEM_LIMIT_BYTES),
        name="gdn",
    )(qkv, gbc, gbr, z, s0.astype(F32), norm_g.reshape(1, -1).astype(F32))


def _flash_kernel(q_ref, kn_ref, krp_ref, v_ref, o_ref, kcat, *, tq, nh):
    qi = pl.program_id(2)

    @pl.when(qi == 0)
    def _():
        for j in range(nh):
            kcat[j, :, 0:MLA_NOPE] = kn_ref[:, j * MLA_NOPE:(j + 1) * MLA_NOPE]
            kcat[j, :, MLA_NOPE:MLA_QK_PAD] = krp_ref[...]

    qs = [q_ref[:, j * MLA_QK_PAD:(j + 1) * MLA_QK_PAD] for j in range(nh)]

    def tile(ki, carry, masked):
        rows = pl.ds(pl.multiple_of(ki * tq, tq), tq)
        ss = [_dot_nt(qs[j], kcat[j, rows, :]) for j in range(nh)]
        if masked:
            ri = lax.broadcasted_iota(jnp.int32, (tq, tq), 0) // CHUNK
            ci = lax.broadcasted_iota(jnp.int32, (tq, tq), 1) // CHUNK
            ss = [jnp.where(ci <= ri, s, -jnp.inf) for s in ss]
        ps, stats = [], []
        for j in range(nh):
            m, l, _ = carry[j]
            m_new = jnp.maximum(m, jnp.max(ss[j], axis=-1, keepdims=True))
            p = jnp.exp2(ss[j] - m_new)
            alpha = jnp.exp2(m - m_new)
            stats.append((m_new, alpha * l + jnp.sum(p, axis=-1, keepdims=True), alpha))
            ps.append(p.astype(BF16))
        out = []
        for j in range(nh):
            m_new, l_new, alpha = stats[j]
            acc = alpha * carry[j][2] + _dot(ps[j], v_ref[rows, j * MLA_V:(j + 1) * MLA_V])
            out.append((m_new, l_new, acc))
        return tuple(out)

    init = tuple((jnp.full((tq, 1), -jnp.inf, F32), jnp.zeros((tq, 1), F32), jnp.zeros((tq, MLA_V), F32))
                 for _ in range(nh))
    carry = lax.fori_loop(0, qi, lambda ki, c: tile(ki, c, False), init)
    carry = tile(qi, carry, True)
    for j in range(nh):
        _, l, acc = carry[j]
        o_ref[:, j * MLA_V:(j + 1) * MLA_V] = (acc / l).astype(BF16)


def _flash_attention(qp, kn, krp, v, tq, nh):
    bsz, length, _ = qp.shape
    nq = length // tq
    return pl.pallas_call(
        functools.partial(_flash_kernel, tq=tq, nh=nh),
        grid=(bsz, MLA_HEADS // nh, nq),
        in_specs=[pl.BlockSpec((None, tq, nh * MLA_QK_PAD), lambda b, h, i: (b, i, h)),
                  pl.BlockSpec((None, length, nh * MLA_NOPE), lambda b, h, i: (b, 0, h)),
                  pl.BlockSpec((None, length, 128), lambda b, h, i: (b, 0, 0)),
                  pl.BlockSpec((None, length, nh * MLA_V), lambda b, h, i: (b, 0, h))],
        out_specs=pl.BlockSpec((None, tq, nh * MLA_V), lambda b, h, i: (b, i, h)),
        out_shape=jax.ShapeDtypeStruct((bsz, length, MLA_HEADS * MLA_V), BF16),
        scratch_shapes=[pltpu.VMEM((nh, length, MLA_QK_PAD), BF16)],
        compiler_params=pltpu.CompilerParams(dimension_semantics=("arbitrary", "arbitrary", "arbitrary"),
                                             vmem_limit_bytes=VMEM_LIMIT_BYTES),
        name="mla_flash",
    )(qp, kn, krp, v)


def _kvup_kernel(c_ref, w_ref, kn_o, v_o):
    kv = _dot(c_ref[...].astype(BF16), w_ref[...])
    kn_o[...] = kv[:, 0:MLA_HEADS * MLA_NOPE].astype(BF16)
    v_o[...] = kv[:, MLA_HEADS * MLA_NOPE:].astype(BF16)


def _kvup(ckv, w_kv, tr):
    bsz, length, _ = ckv.shape
    wide = MLA_HEADS * MLA_NOPE
    return pl.pallas_call(
        _kvup_kernel,
        grid=(bsz, length // tr),
        in_specs=[pl.BlockSpec((None, tr, MLA_KV_LORA), lambda b, l: (b, l, 0)), _const_spec(w_kv.shape)],
        out_specs=[pl.BlockSpec((None, tr, wide), lambda b, l: (b, l, 0)),
                   pl.BlockSpec((None, tr, wide), lambda b, l: (b, l, 0))],
        out_shape=[jax.ShapeDtypeStruct((bsz, length, wide), BF16), jax.ShapeDtypeStruct((bsz, length, wide), BF16)],
        compiler_params=pltpu.CompilerParams(dimension_semantics=("arbitrary", "arbitrary"),
                                             vmem_limit_bytes=VMEM_LIMIT_BYTES),
        name="kv_up_cached",
    )(ckv, w_kv)


def _cached_attn_kernel(q_ref, knp_ref, krp_ref, vp_ref, knn_ref, krn_ref, vn_ref, o_ref, *, past):
    lq = q_ref.shape[0]
    q_chunk = (past + lax.broadcasted_iota(jnp.int32, (lq, past), 0)) // CHUNK
    vis_p = lax.broadcasted_iota(jnp.int32, (lq, past), 1) // CHUNK <= q_chunk
    q_chunk_n = (past + lax.broadcasted_iota(jnp.int32, (lq, lq), 0)) // CHUNK
    vis_n = (past + lax.broadcasted_iota(jnp.int32, (lq, lq), 1)) // CHUNK <= q_chunk_n
    kr_past = krp_ref[...].astype(BF16)
    kr_new = krn_ref[...]
    for hh in range(MLA_HEADS):
        base = hh * MLA_QK_PAD
        qn = q_ref[:, base:base + MLA_NOPE]
        qr = q_ref[:, base + MLA_NOPE:base + MLA_QK_PAD]
        hc = slice(hh * MLA_NOPE, (hh + 1) * MLA_NOPE)
        s_p = _dot_nt(qn, knp_ref[:, hc]) + _dot_nt(qr[:, 0:MLA_ROPE], kr_past)
        s_n = _dot_nt(qn, knn_ref[:, hc]) + _dot_nt(qr, kr_new)
        s_p = jnp.where(vis_p, s_p, -jnp.inf)
        s_n = jnp.where(vis_n, s_n, -jnp.inf)
        m = jnp.maximum(jnp.max(s_p, axis=-1, keepdims=True), jnp.max(s_n, axis=-1, keepdims=True))
        p_p = jnp.exp2(s_p - m)
        p_n = jnp.exp2(s_n - m)
        l = jnp.sum(p_p, axis=-1, keepdims=True) + jnp.sum(p_n, axis=-1, keepdims=True)
        vc = slice(hh * MLA_V, (hh + 1) * MLA_V)
        o = _dot(p_p.astype(BF16), vp_ref[:, vc]) + _dot(p_n.astype(BF16), vn_ref[:, vc])
        o_ref[:, vc] = (o / l).astype(BF16)


def _cached_attention(qp, kn_past, kr_past, v_past, kn_new, krp_new, v_new):
    bsz, lq, _ = qp.shape
    past = kn_past.shape[1]
    wide = MLA_HEADS * MLA_NOPE

    def per_b(rows, width):
        return pl.BlockSpec((None, rows, width), lambda b: (b, 0, 0))

    return pl.pallas_call(
        functools.partial(_cached_attn_kernel, past=past),
        grid=(bsz,),
        in_specs=[per_b(lq, MLA_HEADS * MLA_QK_PAD), per_b(past, wide), per_b(past, MLA_ROPE), per_b(past, wide),
                  per_b(lq, wide), per_b(lq, 128), per_b(lq, wide)],
        out_specs=per_b(lq, MLA_HEADS * MLA_V),
        out_shape=jax.ShapeDtypeStruct((bsz, lq, MLA_HEADS * MLA_V), BF16),
        compiler_params=pltpu.CompilerParams(dimension_semantics=("arbitrary",),
                                             vmem_limit_bytes=VMEM_LIMIT_BYTES),
        name="mla_cached",
    )(qp, kn_past, kr_past, v_past, kn_new, krp_new, v_new)


def _memproj_kernel(m_ref, wk_ref, wv_ref, k_o, v_o):
    mb = m_ref[...].astype(BF16)
    k_o[...] = _dot(mb, wk_ref[...])
    v_o[...] = _dot(mb, wv_ref[...])


def _memproj(mem, w_mk, w_mv):
    bsz, tokens, _ = mem.shape
    wide = MEM_HEADS * MEM_HEAD_DIM
    return pl.pallas_call(
        _memproj_kernel,
        grid=(bsz,),
        in_specs=[pl.BlockSpec((None, tokens, D_MODEL), lambda b: (b, 0, 0)),
                  _const_spec(w_mk.shape), _const_spec(w_mv.shape)],
        out_specs=[pl.BlockSpec((None, tokens, wide), lambda b: (b, 0, 0)),
                   pl.BlockSpec((None, tokens, wide), lambda b: (b, 0, 0))],
        out_shape=[jax.ShapeDtypeStruct((bsz, tokens, wide), F32), jax.ShapeDtypeStruct((bsz, tokens, wide), F32)],
        compiler_params=pltpu.CompilerParams(dimension_semantics=("arbitrary",),
                                             vmem_limit_bytes=VMEM_LIMIT_BYTES),
        name="mem_proj",
    )(mem, w_mk.astype(BF16), w_mv.astype(BF16))


def _route(logits_t, bias_col):
    scores = _sigmoid(logits_t)
    sel = scores + bias_col
    t = logits_t.shape[1]
    sc = [scores[j * N_GROUPS:(j + 1) * N_GROUPS, :] for j in range(GROUP_SIZE)]
    se = [sel[j * N_GROUPS:(j + 1) * N_GROUPS, :] for j in range(GROUP_SIZE)]
    top1 = se[0]
    top2 = jnp.full_like(top1, -jnp.inf)
    for j in range(1, GROUP_SIZE):
        top2 = jnp.maximum(top2, jnp.minimum(top1, se[j]))
        top1 = jnp.maximum(top1, se[j])
    gs = top1 + top2
    gidx = lax.broadcasted_iota(jnp.int32, (N_GROUPS, t), 0)
    beaten = jnp.zeros((N_GROUPS, t), F32)
    for g2 in range(N_GROUPS):
        row = gs[g2:g2 + 1, :]
        tie = jnp.where(gidx > g2, 1.0, 0.0)
        beaten = beaten + jnp.where(row > gs, 1.0, jnp.where(row == gs, tie, 0.0))
    keep = beaten < TOPK_GROUPS
    ms = [jnp.where(keep, se[j], -jnp.inf) for j in range(GROUP_SIZE)]
    cnt = [jnp.zeros((N_GROUPS, t), F32) for _ in range(GROUP_SIZE)]
    for j2 in range(GROUP_SIZE):
        for g2 in range(N_GROUPS):
            row = ms[j2][g2:g2 + 1, :]
            for j in range(GROUP_SIZE):
                if j2 < j:
                    tie = jnp.where(gidx >= g2, 1.0, 0.0)
                else:
                    tie = jnp.where(gidx > g2, 1.0, 0.0)
                cnt[j] = cnt[j] + jnp.where(row > ms[j], 1.0, jnp.where(row == ms[j], tie, 0.0))
    w = [jnp.where(cnt[j] < TOP_K, sc[j], 0.0) for j in range(GROUP_SIZE)]
    tot = w[0]
    for j in range(1, GROUP_SIZE):
        tot = tot + w[j]
    denom = jnp.sum(tot, axis=0, keepdims=True) + 1e-20
    return [w[j] / denom * ROUTED_SCALE for j in range(GROUP_SIZE)]


def _postmix_kernel(x_ref, og_ref, om_ref, gate_ref, mk_ref, mv_ref, lng_ref, lnb_ref, wpg_ref, wpm_ref, wout_ref,
                    ln1g_ref, ln1b_ref, wmq_ref, wmo_ref, ln2g_ref, ln2b_ref, wrt_ref, rb_ref,
                    h2_o, gt_o, *, parts):
    pr = x_ref.shape[0] // parts
    rs = [slice(i * pr, (i + 1) * pr) for i in range(parts)]
    mk = mk_ref[...].astype(BF16)
    mv = mv_ref[...].astype(BF16)
    h = [_layer_norm(x_ref[r, :], lng_ref[...], lnb_ref[...]) for r in rs]
    bg = [_dot(og_ref[r, :], wpg_ref[...]) for r in rs]
    bm = [_dot(om_ref[r, :], wpm_ref[...]) for r in rs]
    merged = [(_sigmoid(gate_ref[r, 0:D_MODEL].astype(F32)) * bg[i]
               + _sigmoid(gate_ref[r, D_MODEL:2 * D_MODEL].astype(F32)) * bm[i]).astype(BF16)
              for i, r in enumerate(rs)]
    mix = [_dot(m, wout_ref[...]) for m in merged]
    h1 = [_layer_norm(DEEPNORM_ALPHA * h[i] + mix[i], ln1g_ref[...], ln1b_ref[...]) for i in range(parts)]
    q = [_dot(v.astype(BF16), wmq_ref[...]).astype(BF16) for v in h1]
    outs = [[] for _ in range(parts)]
    for hh in range(MEM_HEADS):
        hc = slice(hh * MEM_HEAD_DIM, (hh + 1) * MEM_HEAD_DIM)
        s = [_dot_nt(q[i][:, hc], mk[:, hc]) * (MEM_HEAD_DIM ** -0.5) for i in range(parts)]
        p = [jnp.exp(v - jnp.max(v, axis=-1, keepdims=True)) for v in s]
        p = [(v / jnp.sum(v, axis=-1, keepdims=True)).astype(BF16) for v in p]
        for i in range(parts):
            outs[i].append(_dot(p[i], mv[:, hc]).astype(BF16))
    o = [_dot(jnp.concatenate(v, axis=-1), wmo_ref[...]) for v in outs]
    h2 = [_layer_norm(DEEPNORM_ALPHA * h1[i] + o[i], ln2g_ref[...], ln2b_ref[...]) for i in range(parts)]
    logits_t = [_dot_nt(wrt_ref[...], v, precision=lax.Precision.HIGHEST) for v in h2]
    for i, r in enumerate(rs):
        h2_o[r, :] = h2[i]
        gates = _route(logits_t[i], rb_ref[...])
        for j in range(GROUP_SIZE):
            for g in range(N_GROUPS):
                gt_o[g * GROUP_SIZE + j:g * GROUP_SIZE + j + 1, r] = gates[j][g:g + 1, :]


def _postmix(x, og, om, gate, mem_k, mem_v, pw, td):
    bsz, length, _ = x.shape
    nl = length // td
    tokens = mem_k.shape[1]
    wide = MEM_HEADS * MEM_HEAD_DIM
    consts = [pw['ln_in_g'], pw['ln_in_b'], pw['w_proj_gdn'], pw['w_proj_mla'], pw['w_out'], pw['ln1_g'],
              pw['ln1_b'], pw['w_mq'], pw['w_mo'], pw['ln2_g'], pw['ln2_b'], pw['w_router_t'], pw['router_bias']]

    def rows(width):
        return pl.BlockSpec((None, td, width), lambda b, l: (b, l, 0))

    return pl.pallas_call(
        functools.partial(_postmix_kernel, parts=max(1, td // 256)),
        grid=(bsz, nl),
        in_specs=[rows(D_MODEL), rows(D_MODEL), rows(D_MODEL), rows(2 * D_MODEL),
                  pl.BlockSpec((None, tokens, wide), lambda b, l: (b, 0, 0)),
                  pl.BlockSpec((None, tokens, wide), lambda b, l: (b, 0, 0))]
                 + [_const_spec(c.shape) for c in consts],
        out_specs=[rows(D_MODEL), pl.BlockSpec((None, N_EXPERTS, td), lambda b, l: (b, 0, l))],
        out_shape=[jax.ShapeDtypeStruct((bsz, length, D_MODEL), F32),
                   jax.ShapeDtypeStruct((bsz, N_EXPERTS, length), F32)],
        compiler_params=pltpu.CompilerParams(dimension_semantics=("arbitrary", "arbitrary"),
                                             vmem_limit_bytes=VMEM_LIMIT_BYTES),
        name="postmix",
    )(x, og, om, gate, mem_k, mem_v, *consts)


def _moe_kernel(h_ref, g_ref, wgu_ref, wd_ref, wsgu_ref, wsd_ref, ln3g_ref, ln3b_ref, y_o, xb, acc, *, eb):
    e = pl.program_id(1)

    @pl.when(e == 0)
    def _():
        x = h_ref[...].astype(BF16)
        xb[...] = x
        sgu = _dot(x, wsgu_ref[...])
        hid = _silu(sgu[:, 0:SHARED_FF]) * sgu[:, SHARED_FF:]
        acc[...] = _dot(hid.astype(BF16), wsd_ref[...])

    x = xb[...]
    gb = g_ref[...].astype(BF16)
    erow = lax.broadcasted_iota(jnp.int32, (N_EXPERTS, EXPERT_FF), 0)
    for i in range(eb):
        gu = _dot(x, wgu_ref[i])
        pick = jnp.where(erow == e * eb + i, 1.0, 0.0).astype(BF16)
        gcol = _dot(gb, pick)
        hid = _silu(gu[:, 0:EXPERT_FF]) * gu[:, EXPERT_FF:] * gcol
        acc[...] += _dot(hid.astype(BF16), wd_ref[i])

    @pl.when(e == pl.num_programs(1) - 1)
    def _():
        y_o[...] = _layer_norm(DEEPNORM_ALPHA * h_ref[...] + acc[...], ln3g_ref[...], ln3b_ref[...])


def _moe(h2, gates, ew, tm, eb):
    n = h2.shape[0]
    consts = [ew['ws_gu'], ew['ws_down'], ew['ln3_g'], ew['ln3_b']]
    return pl.pallas_call(
        functools.partial(_moe_kernel, eb=eb),
        grid=(n // tm, N_EXPERTS // eb),
        in_specs=[pl.BlockSpec((tm, D_MODEL), lambda t, e: (t, 0)),
                  pl.BlockSpec((tm, N_EXPERTS), lambda t, e: (t, 0)),
                  pl.BlockSpec((eb, D_MODEL, 2 * EXPERT_FF), lambda t, e: (e, 0, 0)),
                  pl.BlockSpec((eb, EXPERT_FF, D_MODEL), lambda t, e: (e, 0, 0))]
                 + [_const_spec(c.shape) for c in consts],
        out_specs=pl.BlockSpec((tm, D_MODEL), lambda t, e: (t, 0)),
        out_shape=jax.ShapeDtypeStruct((n, D_MODEL), F32),
        scratch_shapes=[pltpu.VMEM((tm, D_MODEL), BF16), pltpu.VMEM((tm, D_MODEL), F32)],
        compiler_params=pltpu.CompilerParams(dimension_semantics=("arbitrary", "arbitrary"),
                                             vmem_limit_bytes=VMEM_LIMIT_BYTES),
        name="moe",
    )(h2, gates, ew['w_gu'], ew['w_down'], *consts)


MOE_ROUTED_TILE = 1024
MOE_ROUTED_SUB = 512
MOE_ROUTED_EXPERTS = 8
MOE_ROUTED_SLOTS = 96
MOE_VMEM_LIMIT_BYTES = 60 * 1024 * 1024


def _moe_routed_kernel(h_ref, g_ref, wgu_ref, wd_ref, wsgu_ref, wsd_ref, ln3g_ref, ln3b_ref, y_o,
                       xb, rank_scr, p_scr, o_scr, *, eb, rb, ts):
    e = pl.program_id(2)
    t = h_ref.shape[0]
    ns = t // ts

    @pl.when(e == 0)
    def _():
        x = h_ref[...].astype(BF16)
        xb[...] = x
        sgu = _dot(x, wsgu_ref[...])
        hid = _silu(sgu[:, 0:SHARED_FF]) * sgu[:, SHARED_FF:]
        y_o[...] = _dot(hid.astype(BF16), wsd_ref[...])
        routed = jnp.where(g_ref[...] > 0.0, 1.0, 0.0).astype(BF16)
        upto = jnp.where(lax.broadcasted_iota(jnp.int32, (ts, ts), 0) <= lax.broadcasted_iota(jnp.int32, (ts, ts), 1),
                         1.0, 0.0).astype(BF16)
        for s in range(ns):
            rank_scr[:, s * ts:(s + 1) * ts] = _dot(routed[:, s * ts:(s + 1) * ts], upto)

    e0 = pl.multiple_of(e * eb, eb)
    gblk = g_ref[pl.ds(e0, eb), :]
    rblk = rank_scr[pl.ds(e0, eb), :]
    npass = (jnp.max(rblk).astype(jnp.int32) + (rb - 1)) // rb
    slot = lax.broadcasted_iota(jnp.int32, (rb, ts), 0).astype(F32)

    def one_pass(k, carry):
        base = (k * rb).astype(F32)
        for i in range(eb):
            gi = gblk[i:i + 1, :]
            pos = jnp.where(gi > 0.0, rblk[i:i + 1, :] - 1.0 - base, -1.0)
            xgs, gates = [], []
            for s in range(ns):
                cs = slice(s * ts, (s + 1) * ts)
                hit = pos[:, cs] == slot
                pb = jnp.where(hit, 1.0, 0.0).astype(BF16)
                p_scr[s, i * rb:(i + 1) * rb, :] = pb
                gates.append(jnp.sum(jnp.where(hit, gi[:, cs], 0.0), axis=1, keepdims=True))
                xgs.append(_dot(pb, xb[cs, :]).astype(BF16))
            gu = _dot(jnp.concatenate(xgs, axis=0), wgu_ref[i])
            hid = _silu(gu[:, 0:EXPERT_FF]) * gu[:, EXPERT_FF:] * jnp.concatenate(gates, axis=0)
            out = _dot(hid.astype(BF16), wd_ref[i]).astype(BF16)
            for s in range(ns):
                o_scr[s, i * rb:(i + 1) * rb, :] = out[s * rb:(s + 1) * rb, :]
        for s in range(ns):
            y_o[s * ts:(s + 1) * ts, :] += _dot_tn(p_scr[s], o_scr[s])
        return carry

    lax.fori_loop(0, npass, one_pass, 0)

    @pl.when(e == pl.num_programs(2) - 1)
    def _():
        y_o[...] = _layer_norm(DEEPNORM_ALPHA * h_ref[...] + y_o[...], ln3g_ref[...], ln3b_ref[...])


def _moe_routed(h2, gt, ew, tm, ts, eb, rb):
    bsz, length, _ = h2.shape
    consts = [ew['ws_gu'], ew['ws_down'], ew['ln3_g'], ew['ln3_b']]
    return pl.pallas_call(
        functools.partial(_moe_routed_kernel, eb=eb, rb=rb, ts=ts),
        grid=(bsz, length // tm, N_EXPERTS // eb),
        in_specs=[pl.BlockSpec((None, tm, D_MODEL), lambda b, t, e: (b, t, 0), pipeline_mode=pl.Buffered(1)),
                  pl.BlockSpec((None, N_EXPERTS, tm), lambda b, t, e: (b, 0, t)),
                  pl.BlockSpec((eb, D_MODEL, 2 * EXPERT_FF), lambda b, t, e: (e, 0, 0)),
                  pl.BlockSpec((eb, EXPERT_FF, D_MODEL), lambda b, t, e: (e, 0, 0))]
                 + [_const_spec(c.shape) for c in consts],
        out_specs=pl.BlockSpec((None, tm, D_MODEL), lambda b, t, e: (b, t, 0)),
        out_shape=jax.ShapeDtypeStruct((bsz, length, D_MODEL), F32),
        scratch_shapes=[pltpu.VMEM((tm, D_MODEL), BF16), pltpu.VMEM((N_EXPERTS, tm), F32),
                        pltpu.VMEM((tm // ts, eb * rb, ts), BF16), pltpu.VMEM((tm // ts, eb * rb, D_MODEL), BF16)],
        compiler_params=pltpu.CompilerParams(dimension_semantics=("arbitrary", "arbitrary", "arbitrary"),
                                             vmem_limit_bytes=MOE_VMEM_LIMIT_BYTES),
        name="moe_routed",
    )(h2, gt, ew['w_gu'], ew['w_down'], *consts)


def _prep_post_weights(ln_in_g, ln_in_b, w_proj_gdn, w_proj_mla, w_out, ln1_g, ln1_b, w_mq, w_mo, ln2_g, ln2_b,
                       w_router, router_bias):
    def row(v):
        return v.reshape(1, -1).astype(F32)

    wrt = w_router.T.reshape(N_GROUPS, GROUP_SIZE, D_MODEL).transpose(1, 0, 2).reshape(N_EXPERTS, D_MODEL)
    rb = router_bias.reshape(N_GROUPS, GROUP_SIZE).T.reshape(N_EXPERTS, 1)
    return dict(ln_in_g=row(ln_in_g), ln_in_b=row(ln_in_b), w_proj_gdn=w_proj_gdn.astype(BF16),
                w_proj_mla=w_proj_mla.astype(BF16), w_out=w_out.astype(BF16), ln1_g=row(ln1_g), ln1_b=row(ln1_b),
                w_mq=w_mq.astype(BF16), w_mo=w_mo.astype(BF16), ln2_g=row(ln2_g), ln2_b=row(ln2_b),
                w_router_t=wrt.astype(F32), router_bias=rb.astype(F32))


def _prep_expert_weights(w_gate, w_up, w_down, ws_gate, ws_up, ws_down, ln3_g, ln3_b):
    return dict(w_gu=jnp.concatenate([w_gate, w_up], axis=-1).astype(BF16), w_down=w_down.astype(BF16),
                ws_gu=jnp.concatenate([ws_gate, ws_up], axis=-1).astype(BF16), ws_down=ws_down.astype(BF16),
                ln3_g=ln3_g.reshape(1, -1).astype(F32), ln3_b=ln3_b.reshape(1, -1).astype(F32))


def _tail(x, og, om, gate, mem_k, mem_v, pw, ew, td):
    bsz, length, _ = x.shape
    h2, gt = _postmix(x, og, om, gate, mem_k, mem_v, pw, td)
    if length % MOE_ROUTED_TILE == 0:
        return _moe_routed(h2, gt, ew, MOE_ROUTED_TILE, MOE_ROUTED_SUB, MOE_ROUTED_EXPERTS, MOE_ROUTED_SLOTS)
    gates = gt.transpose(0, 2, 1).reshape(bsz * length, N_EXPERTS)
    y = _moe(h2.reshape(bsz * length, D_MODEL), gates, ew, bsz * length, 2)
    return y.reshape(bsz, length, D_MODEL)


def kernel(x_prompt, x_sample, mem_prompt, state_gdn_conv, state_gdn_ssm, cache_mla_ckv, cache_mla_krope,
           cache_mem_k, cache_mem_v, ln_in_g, ln_in_b, w_in, w_conv, a_log, dt_bias, gdn_norm_g, w_proj_gdn,
           q_norm_g, w_uq, kv_norm_g, w_ukv, w_proj_mla, w_out, ln1_g, ln1_b, w_mq, w_mk, w_mv, w_mo,
           ln2_g, ln2_b, w_router, router_bias, w_gate, w_up, w_down, ws_gate, ws_up, ws_down, ln3_g, ln3_b):
    lyr = 0
    bsz, seq, _ = x_prompt.shape
    dbsz, dseq, _ = x_sample.shape
    past = cache_mla_ckv.shape[2]
    mw = _prep_mixer_weights(w_in[lyr], w_conv[lyr], a_log[lyr], dt_bias[lyr], q_norm_g[lyr], w_uq[lyr],
                             kv_norm_g[lyr], w_ukv[lyr])
    pw = _prep_post_weights(ln_in_g, ln_in_b, w_proj_gdn[lyr], w_proj_mla[lyr], w_out[lyr], ln1_g[lyr], ln1_b[lyr],
                            w_mq[lyr], w_mo[lyr], ln2_g[lyr], ln2_b[lyr], w_router[lyr], router_bias[lyr])
    ew = _prep_expert_weights(w_gate[lyr], w_up[lyr], w_down[lyr], ws_gate[lyr], ws_up[lyr], ws_down[lyr],
                              ln3_g[lyr], ln3_b[lyr])
    wide = MEM_HEADS * MEM_HEAD_DIM

    conv0 = jnp.zeros((bsz, GDN_CONV - 1, GDN_CONV_CH), F32)
    s0 = jnp.zeros((bsz, GDN_HEADS, GDN_DK, GDN_DV), F32)
    (qkv, z, gate, qp, kn, krp, v, ckv_p, kr_p, conv_p, gbc, gbr) = _inproj(
        x_prompt, ln_in_g, ln_in_b, mw, conv0, 0, tl=min(256, seq))
    og, ssm_p = _gdn(qkv, gbc, gbr, z, s0, gdn_norm_g[lyr], chunk=CHUNK, tg=min(256, seq), nb=math.gcd(bsz, 4))
    om = _flash_attention(qp, kn, krp, v, tq=min(1024, seq), nh=2)
    mem_k, mem_v = _memproj(mem_prompt, w_mk[lyr], w_mv[lyr])
    y_p = _tail(x_prompt, og, om, gate, mem_k, mem_v, pw, ew, td=min(1024, seq))

    (qkv, z, gate, qp, kn, krp, v, ckv_s, kr_s, conv_s, gbc, gbr) = _inproj(
        x_sample, ln_in_g, ln_in_b, mw, state_gdn_conv[lyr], past, tl=dseq)
    chunk_s = CHUNK if dseq % CHUNK == 0 else dseq
    og, ssm_s = _gdn(qkv, gbc, gbr, z, state_gdn_ssm[lyr], gdn_norm_g[lyr], chunk=chunk_s, tg=dseq, nb=math.gcd(dbsz, 2))
    kn_past, v_past = _kvup(cache_mla_ckv[lyr], mw['w_kv'], tr=min(1024, past))
    om = _cached_attention(qp, kn_past, cache_mla_krope[lyr], v_past, kn, krp, v)
    y_s = _tail(x_sample, og, om, gate, cache_mem_k[lyr].reshape(dbsz, -1, wide),
                cache_mem_v[lyr].reshape(dbsz, -1, wide), pw, ew, td=dseq)

    return (y_p, y_s, conv_p[None], ssm_p[None], ckv_p[None], kr_p[None],
            mem_k.reshape(1, bsz, -1, MEM_HEADS, MEM_HEAD_DIM), mem_v.reshape(1, bsz, -1, MEM_HEADS, MEM_HEAD_DIM),
            conv_s[None], ssm_s[None], ckv_s[None], kr_s[None])
```

```python
import functools
import math

import jax
import jax.numpy as jnp
from jax import lax
from jax.experimental import pallas as pl
from jax.experimental.pallas import tpu as pltpu

F32 = jnp.float32
BF16 = jnp.bfloat16

D_MODEL = 1024
CHUNK = 64
GDN_HEADS = 8
GDN_DK = 128
GDN_DV = 128
GDN_CONV = 4
GDN_QK = GDN_HEADS * GDN_DK
GDN_CONV_CH = 2 * GDN_QK + GDN_HEADS * GDN_DV
MLA_HEADS = 8
MLA_Q_LORA = 384
MLA_KV_LORA = 256
MLA_NOPE = 128
MLA_ROPE = 64
MLA_V = 128
MLA_SCALE = (MLA_NOPE + MLA_ROPE) ** -0.5
MLA_SCALE_LOG2 = MLA_SCALE * math.log2(math.e)
MLA_QK_PAD = 256
ROPE_THETA = 10000.0
MEM_HEADS = 4
MEM_HEAD_DIM = 128
N_EXPERTS = 64
TOP_K = 8
N_GROUPS = 8
GROUP_SIZE = N_EXPERTS // N_GROUPS
TOPK_GROUPS = 4
EXPERT_FF = 256
SHARED_FF = 256
ROUTED_SCALE = 2.5
DEPTH = 1
DEEPNORM_ALPHA = (2.0 * DEPTH) ** 0.25
IN_WIDTHS = (GDN_CONV_CH, GDN_HEADS * GDN_DV, GDN_HEADS, GDN_HEADS, MLA_Q_LORA, MLA_KV_LORA, MLA_ROPE,
             D_MODEL, D_MODEL)

VMEM_LIMIT_BYTES = 56 * 1024 * 1024
VMEM_LIMIT_LARGE_BYTES = 60 * 1024 * 1024


def _dot(a, b, precision=None):
    return jnp.dot(a, b, preferred_element_type=F32, precision=precision)


def _dot_nt(a, b, precision=None):
    return lax.dot_general(a, b, (((1,), (1,)), ((), ())), preferred_element_type=F32, precision=precision)


def _dot_tn(a, b):
    return lax.dot_general(a, b, (((0,), (0,)), ((), ())), preferred_element_type=F32)


def _layer_norm(x, g, b, eps=1e-5):
    mu = jnp.mean(x, axis=-1, keepdims=True)
    xc = x - mu
    var = jnp.mean(xc * xc, axis=-1, keepdims=True)
    return xc * lax.rsqrt(var + eps) * g + b


def _rms_norm(x, g, eps=1e-6):
    return x * lax.rsqrt(jnp.mean(x * x, axis=-1, keepdims=True) + eps) * g


def _sigmoid(x):
    return 1.0 / (1.0 + jnp.exp(-x))


def _silu(x):
    return x * _sigmoid(x)


def _softplus(x):
    return jnp.maximum(x, 0.0) + jnp.log1p(jnp.exp(-jnp.abs(x)))


def _const_spec(shape):
    nd = len(shape)
    return pl.BlockSpec(shape, lambda *_: (0,) * nd, pipeline_mode=pl.Buffered(1))


def _inproj_kernel(x_ref, lng_ref, lnb_ref, wqkv_ref, wz_ref, wgate_ref, wcq_ref, wckv_ref, wkr_ref, wab_ref,
                   wabt_ref, convw_ref, conv0_ref, arow_ref, acol_ref, qng_ref, wuq_ref, wuqs_ref, kvng_ref,
                   wuk_ref, wuvt_ref, t1_ref, t2_ref,
                   qkv_o, z_o, gate_o, qp_o, kn_o, krp_o, vt_o, ckv_o, kr_o, conv_o, gbc_o, gbr_o,
                   cbuf):
    tl = x_ref.shape[0]
    li = pl.program_id(1)

    h = _layer_norm(x_ref[...], lng_ref[...], lnb_ref[...])
    hb = h.astype(BF16)

    @pl.when(li == 0)
    def _():
        cbuf[0:8, :] = conv0_ref[...]

    cbuf[8:8 + tl, :] = _dot(hb, wqkv_ref[...])
    conv_o[...] = cbuf[tl + 5:tl + 8, :]
    for part in range(3):
        cols = slice(part * GDN_QK, (part + 1) * GDN_QK)
        y = convw_ref[0:1, cols] * cbuf[5:5 + tl, cols]
        for i in range(1, GDN_CONV):
            y = y + convw_ref[i:i + 1, cols] * cbuf[5 + i:5 + i + tl, cols]
        y = _silu(y)
        if part < 2:
            for hh in range(GDN_HEADS):
                hc = slice(hh * GDN_DK, (hh + 1) * GDN_DK)
                yh = y[:, hc]
                r = lax.rsqrt(jnp.sum(yh * yh, axis=-1, keepdims=True) + 1e-6)
                yh = yh * r
                if part == 0:
                    yh = yh * (GDN_DK ** -0.5)
                qkv_o[:, part * GDN_QK + hh * GDN_DK:part * GDN_QK + (hh + 1) * GDN_DK] = yh.astype(BF16)
        else:
            qkv_o[:, cols] = y.astype(BF16)
    cbuf[0:8, :] = cbuf[tl:tl + 8, :]

    z_o[...] = _dot(hb, wz_ref[...]).astype(BF16)
    gate_o[...] = _dot(hb, wgate_ref[...]).astype(BF16)

    ab = _dot(hb, wab_ref[...])
    arow = arow_ref[...]
    g_c = arow[0:1, :] * _softplus(ab[:, 0:GDN_HEADS] + arow[1:2, :])
    b_c = _sigmoid(ab[:, GDN_HEADS:2 * GDN_HEADS])
    gbc_o[:, 0:GDN_HEADS] = g_c
    gbc_o[:, GDN_HEADS:2 * GDN_HEADS] = b_c
    abt = _dot_nt(wabt_ref[...], hb)
    acol = acol_ref[...]
    gbr_o[0:GDN_HEADS, :] = acol[:, 0:1] * _softplus(abt[0:GDN_HEADS, :] + acol[:, 1:2])
    gbr_o[GDN_HEADS:2 * GDN_HEADS, :] = _sigmoid(abt[GDN_HEADS:2 * GDN_HEADS, :])

    t1 = t1_ref[...]
    t2 = t2_ref[...]

    cq = _rms_norm(_dot(hb, wcq_ref[...]), qng_ref[...]).astype(BF16)
    qm = _dot(cq, wuq_ref[...])
    qs = _dot(cq, wuqs_ref[...])
    for hh in range(MLA_HEADS):
        base = hh * MLA_QK_PAD
        qp_o[:, base:base + MLA_NOPE] = (qm[:, base:base + MLA_NOPE] * MLA_SCALE_LOG2).astype(BF16)
        hi = qm[:, base + MLA_NOPE:base + MLA_QK_PAD] * t1 + qs[:, hh * 128:(hh + 1) * 128] * t2
        qp_o[:, base + MLA_NOPE:base + MLA_QK_PAD] = (hi * MLA_SCALE_LOG2).astype(BF16)

    ckv = _rms_norm(_dot(hb, wckv_ref[...]), kvng_ref[...])
    ckv_o[...] = ckv
    ckvb = ckv.astype(BF16)
    kn_o[...] = _dot(ckvb, wuk_ref[...]).astype(BF16)
    vt_o[...] = _dot_nt(wuvt_ref[...], ckvb).astype(BF16)
    krm = _dot(hb, wkr_ref[...])
    krp = krm[:, 0:128] * t1 + krm[:, 128:256] * t2
    kr_o[...] = krp[:, 0:MLA_ROPE]
    krp_o[...] = krp.astype(BF16)


def _swap_halves(w):
    half = w.shape[-1] // 2
    return jnp.concatenate([w[..., half:], w[..., :half]], axis=-1)


def _rope_tables(pos0, length):
    inv_freq = jnp.power(ROPE_THETA, -jnp.arange(0, MLA_ROPE, 2, dtype=F32) / MLA_ROPE)
    ang = (pos0 + jnp.arange(length)).astype(F32)[:, None] * inv_freq[None, :]
    c, s = jnp.cos(ang), jnp.sin(ang)
    zeros = jnp.zeros((length, 128 - MLA_ROPE), F32)
    return jnp.concatenate([c, c, zeros], axis=-1), jnp.concatenate([-s, s, zeros], axis=-1)


def _prep_mixer_weights(w_in, w_conv, a_log, dt_bias, q_norm_g, w_uq, kv_norm_g, w_ukv):
    splits = []
    off = 0
    for wd in IN_WIDTHS:
        splits.append(w_in[:, off:off + wd])
        off += wd
    w_qkv, w_z, w_a, w_b, w_cq, w_ckv, w_kr, w_gg, w_gm = splits
    z64 = jnp.zeros((D_MODEL, 128 - MLA_ROPE), F32)
    w_krg = jnp.concatenate([w_kr, z64, _swap_halves(w_kr), z64], axis=-1)
    w_ab = jnp.concatenate([w_a, w_b], axis=-1)
    wq = w_uq.reshape(MLA_Q_LORA, MLA_HEADS, MLA_NOPE + MLA_ROPE)
    nope, rope = wq[:, :, :MLA_NOPE], wq[:, :, MLA_NOPE:]
    zq = jnp.zeros((MLA_Q_LORA, MLA_HEADS, MLA_QK_PAD - MLA_NOPE - MLA_ROPE), F32)
    wq_main = jnp.concatenate([nope, rope, zq], axis=-1).reshape(MLA_Q_LORA, MLA_HEADS * MLA_QK_PAD)
    wq_swap = jnp.concatenate([_swap_halves(rope), zq], axis=-1).reshape(MLA_Q_LORA, MLA_HEADS * 128)
    wkv = w_ukv.reshape(MLA_KV_LORA, MLA_HEADS, MLA_NOPE + MLA_V)
    w_kv = jnp.concatenate([wkv[:, :, :MLA_NOPE].reshape(MLA_KV_LORA, -1),
                            wkv[:, :, MLA_NOPE:].reshape(MLA_KV_LORA, -1)], axis=-1)
    neg_a = -jnp.exp(a_log.astype(F32))
    arow = jnp.stack([neg_a, dt_bias.astype(F32)], axis=0)
    return dict(
        w_qkv=w_qkv.astype(BF16), w_z=w_z.astype(BF16),
        w_gate=jnp.concatenate([w_gg, w_gm], axis=-1).astype(BF16),
        w_cq=w_cq.astype(BF16), w_ckv=w_ckv.astype(BF16), w_krg=w_krg.astype(BF16),
        w_ab=w_ab.astype(BF16), w_abt=w_ab.T.astype(BF16),
        w_conv=w_conv.astype(F32), arow=arow, acol=arow.T,
        q_norm_g=q_norm_g.reshape(1, -1).astype(F32), wq_main=wq_main.astype(BF16), wq_swap=wq_swap.astype(BF16),
        kv_norm_g=kv_norm_g.reshape(1, -1).astype(F32), w_kv=w_kv.astype(BF16),
        w_k=wkv[:, :, :MLA_NOPE].reshape(MLA_KV_LORA, -1).astype(BF16),
        w_vt=wkv[:, :, MLA_NOPE:].reshape(MLA_KV_LORA, -1).T.astype(BF16))


def _inproj(x, ln_g, ln_b, mw, conv_state, pos0, tl):
    bsz, length, _ = x.shape
    nl = length // tl
    t1, t2 = _rope_tables(pos0, length)
    conv0 = jnp.concatenate([jnp.zeros((bsz, 8 - (GDN_CONV - 1), GDN_CONV_CH), F32), conv_state.astype(F32)], axis=1)

    def rows(width, dtype):
        return (jax.ShapeDtypeStruct((bsz, length, width), dtype),
                pl.BlockSpec((None, tl, width), lambda b, l: (b, l, 0)))

    outs = [rows(GDN_CONV_CH, BF16), rows(GDN_HEADS * GDN_DV, BF16), rows(2 * D_MODEL, BF16),
            rows(MLA_HEADS * MLA_QK_PAD, BF16), rows(MLA_HEADS * MLA_NOPE, BF16), rows(128, BF16),
            (jax.ShapeDtypeStruct((bsz, MLA_HEADS * MLA_V, length), BF16),
             pl.BlockSpec((None, MLA_HEADS * MLA_V, tl), lambda b, l: (b, 0, l))),
            rows(MLA_KV_LORA, F32), rows(MLA_ROPE, F32),
            (jax.ShapeDtypeStruct((bsz, GDN_CONV - 1, GDN_CONV_CH), F32),
             pl.BlockSpec((None, GDN_CONV - 1, GDN_CONV_CH), lambda b, l: (b, 0, 0))),
            rows(2 * GDN_HEADS, F32),
            (jax.ShapeDtypeStruct((bsz, 2 * GDN_HEADS, length), F32),
             pl.BlockSpec((None, 2 * GDN_HEADS, tl), lambda b, l: (b, 0, l)))]
    consts = [ln_g.reshape(1, -1), ln_b.reshape(1, -1), mw['w_qkv'], mw['w_z'], mw['w_gate'], mw['w_cq'],
              mw['w_ckv'], mw['w_krg'], mw['w_ab'], mw['w_abt'], mw['w_conv']]
    consts2 = [mw['arow'], mw['acol'], mw['q_norm_g'], mw['wq_main'], mw['wq_swap'], mw['kv_norm_g'], mw['w_k'],
               mw['w_vt']]
    in_specs = ([pl.BlockSpec((None, tl, D_MODEL), lambda b, l: (b, l, 0))]
                + [_const_spec(c.shape) for c in consts]
                + [pl.BlockSpec((None, 8, GDN_CONV_CH), lambda b, l: (b, 0, 0))]
                + [_const_spec(c.shape) for c in consts2]
                + [pl.BlockSpec((tl, 128), lambda b, l: (l, 0)), pl.BlockSpec((tl, 128), lambda b, l: (l, 0))])
    return pl.pallas_call(
        _inproj_kernel,
        grid=(bsz, nl),
        in_specs=in_specs,
        out_specs=[o[1] for o in outs],
        out_shape=[o[0] for o in outs],
        scratch_shapes=[pltpu.VMEM((tl + 8, GDN_CONV_CH), F32)],
        compiler_params=pltpu.CompilerParams(dimension_semantics=("arbitrary", "arbitrary"),
                                             vmem_limit_bytes=VMEM_LIMIT_BYTES),
        name="inproj",
    )(x, *consts, conv0, *consts2, t1, t2)


def _gdn_kernel(qkv_ref, gbc_ref, gbr_ref, z_ref, s0_ref, ng_ref, og_o, s_o, s_scr, *, chunk):
    tg = qkv_ref.shape[1]
    nchunks = tg // chunk
    li = pl.program_id(1)

    @pl.when(li == 0)
    def _():
        s_scr[...] = s0_ref[...]

    ri = lax.broadcasted_iota(jnp.int32, (chunk, chunk), 0)
    ci = lax.broadcasted_iota(jnp.int32, (chunk, chunk), 1)
    incl = ri >= ci
    strict = ri > ci
    tri = jnp.where(incl, 1.0, 0.0).astype(F32)
    eye = jnp.where(ri == ci, 1.0, 0.0).astype(F32)
    ng = ng_ref[...]
    n_sq = int(math.log2(chunk)) - 1

    nb = qkv_ref.shape[0]
    chains = [(g, hh) for g in range(nb) for hh in range(GDN_HEADS)]

    def chunk_body(c, carry):
        r0 = pl.multiple_of(c * chunk, chunk)
        rows = pl.ds(r0, chunk)
        gbc = [gbc_ref[g, rows, :] for g in range(nb)]
        gc_col = [_dot(tri, gbc[g], precision=lax.Precision.HIGHEST) for g in range(nb)]
        gc_row = [_dot_nt(gbr_ref[g, c], tri, precision=lax.Precision.HIGHEST) for g in range(nb)]
        q, k, qkk = {}, {}, {}
        for ch in chains:
            g, hh = ch
            q[ch] = qkv_ref[g, rows, hh * GDN_DK:(hh + 1) * GDN_DK]
            k[ch] = qkv_ref[g, rows, GDN_QK + hh * GDN_DK:GDN_QK + (hh + 1) * GDN_DK]
            qkk[ch] = _dot_nt(jnp.concatenate([q[ch], k[ch]], axis=0), k[ch])
        gcc, beta, gamma, g_end, qk, xp, t_inv = {}, {}, {}, {}, {}, {}, {}
        for ch in chains:
            g, hh = ch
            gcc[ch] = gc_col[g][:, hh:hh + 1]
            gcr = gc_row[g][hh:hh + 1, :]
            beta[ch] = gbc[g][:, GDN_HEADS + hh:GDN_HEADS + hh + 1]
            g_end[ch] = gc_col[g][chunk - 1:chunk, hh:hh + 1]
            gamma[ch] = jnp.exp(gcc[ch])
            decay = jnp.where(incl, jnp.exp(jnp.where(incl, gcc[ch] - gcr, 0.0)), 0.0)
            qk[ch] = (decay * qkk[ch][0:chunk, :]).astype(BF16)
            xp[ch] = jnp.where(strict, beta[ch] * decay * qkk[ch][chunk:2 * chunk, :], 0.0)
            t_inv[ch] = eye - xp[ch]
        for _ in range(n_sq):
            for ch in chains:
                xb = xp[ch].astype(BF16)
                xp[ch] = _dot(xb, xb)
            for ch in chains:
                t_inv[ch] = t_inv[ch] + _dot(t_inv[ch].astype(BF16), xp[ch].astype(BF16))
        sol, q_s, sb, s_old, k_dec = {}, {}, {}, {}, {}
        for ch in chains:
            g, hh = ch
            kf = k[ch].astype(F32)
            v = qkv_ref[g, rows, 2 * GDN_QK + hh * GDN_DV:2 * GDN_QK + (hh + 1) * GDN_DV].astype(F32)
            rhs = jnp.concatenate([beta[ch] * v, (beta[ch] * gamma[ch]) * kf], axis=-1).astype(BF16)
            sol[ch] = _dot(t_inv[ch].astype(BF16), rhs)
            k_dec[ch] = (jnp.exp(g_end[ch] - gcc[ch]) * kf).astype(BF16)
            s_old[ch] = s_scr[g, hh]
            sb[ch] = s_old[ch].astype(BF16)
            q_s[ch] = _dot((gamma[ch] * q[ch].astype(F32)).astype(BF16), sb[ch])
        ub = {}
        for ch in chains:
            u = sol[ch][:, :GDN_DV] - _dot(sol[ch][:, GDN_DV:].astype(BF16), sb[ch])
            ub[ch] = u.astype(BF16)
        for ch in chains:
            g, hh = ch
            o = q_s[ch] + _dot(qk[ch], ub[ch])
            s_scr[g, hh] = jnp.exp(g_end[ch]) * s_old[ch] + _dot_tn(k_dec[ch], ub[ch])
            zz = z_ref[g, rows, hh * GDN_DV:(hh + 1) * GDN_DV].astype(F32)
            og_o[g, rows, hh * GDN_DV:(hh + 1) * GDN_DV] = (_rms_norm(o, ng) * _silu(zz)).astype(BF16)
        return carry

    lax.fori_loop(0, nchunks, chunk_body, 0)
    s_o[...] = s_scr[...]


def _gdn(qkv, gbc, gbr, z, s0, norm_g, chunk, tg, nb):
    bsz, length, _ = qkv.shape
    nl = length // tg
    st_shape = (GDN_HEADS, GDN_DK, GDN_DV)
    gbr = gbr.reshape(bsz, 2 * GDN_HEADS, length // chunk, chunk).transpose(0, 2, 1, 3)
    return pl.pallas_call(
        functools.partial(_gdn_kernel, chunk=chunk),
        grid=(bsz // nb, nl),
        in_specs=[pl.BlockSpec((nb, tg, GDN_CONV_CH), lambda b, l: (b, l, 0)),
                  pl.BlockSpec((nb, tg, 2 * GDN_HEADS), lambda b, l: (b, l, 0)),
                  pl.BlockSpec((nb, tg // chunk, 2 * GDN_HEADS, chunk), lambda b, l: (b, l, 0, 0)),
                  pl.BlockSpec((nb, tg, GDN_HEADS * GDN_DV), lambda b, l: (b, l, 0)),
                  pl.BlockSpec((nb,) + st_shape, lambda b, l: (b, 0, 0, 0)),
                  _const_spec((1, GDN_DV))],
        out_specs=[pl.BlockSpec((nb, tg, GDN_HEADS * GDN_DV), lambda b, l: (b, l, 0)),
                   pl.BlockSpec((nb,) + st_shape, lambda b, l: (b, 0, 0, 0))],
        out_shape=[jax.ShapeDtypeStruct((bsz, length, GDN_HEADS * GDN_DV), BF16),
                   jax.ShapeDtypeStruct((bsz,) + st_shape, F32)],
        scratch_shapes=[pltpu.VMEM((nb,) + st_shape, F32)],
        compiler_params=pltpu.CompilerParams(dimension_semantics=("arbitrary", "arbitrary"),
                                             vmem_limit_bytes=VMEM_LIMIT_BYTES),
        name="gdn",
    )(qkv, gbc, gbr, z, s0.astype(F32), norm_g.reshape(1, -1).astype(F32))


def _flash_kernel(q_ref, kn_ref, krp_ref, vt_ref, o_ref, kcat, *, tq, nh):
    qi = pl.program_id(2)

    @pl.when(qi == 0)
    def _():
        for j in range(nh):
            kcat[j, :, 0:MLA_NOPE] = kn_ref[:, j * MLA_NOPE:(j + 1) * MLA_NOPE]
            kcat[j, :, MLA_NOPE:MLA_QK_PAD] = krp_ref[...]

    qs = [q_ref[:, j * MLA_QK_PAD:(j + 1) * MLA_QK_PAD] for j in range(nh)]

    def tile(ki, carry, masked):
        rows = pl.ds(pl.multiple_of(ki * tq, tq), tq)
        ss = [_dot_nt(kcat[j, rows, :], qs[j]) for j in range(nh)]
        if masked:
            kc = lax.broadcasted_iota(jnp.int32, (tq, tq), 0) // CHUNK
            qc = lax.broadcasted_iota(jnp.int32, (tq, tq), 1) // CHUNK
            ss = [jnp.where(kc <= qc, s, -jnp.inf) for s in ss]
        ps, stats = [], []
        for j in range(nh):
            m, l, _ = carry[j]
            m_new = jnp.maximum(m, jnp.max(ss[j], axis=0, keepdims=True))
            p = jnp.exp2(ss[j] - m_new)
            alpha = jnp.exp2(m - m_new)
            stats.append((m_new, alpha * l + jnp.sum(p, axis=0, keepdims=True), alpha))
            ps.append(p.astype(BF16))
        out = []
        for j in range(nh):
            m_new, l_new, alpha = stats[j]
            acc = alpha * carry[j][2] + _dot(vt_ref[j * MLA_V:(j + 1) * MLA_V, rows], ps[j])
            out.append((m_new, l_new, acc))
        return tuple(out)

    init = tuple((jnp.full((1, tq), -jnp.inf, F32), jnp.zeros((1, tq), F32), jnp.zeros((MLA_V, tq), F32))
                 for _ in range(nh))
    carry = lax.fori_loop(0, qi, lambda ki, c: tile(ki, c, False), init)
    carry = tile(qi, carry, True)
    for j in range(nh):
        _, l, acc = carry[j]
        o_ref[:, j * MLA_V:(j + 1) * MLA_V] = (acc / l).T.astype(BF16)


def _flash_attention(qp, kn, krp, vt, tq, nh):
    bsz, length, _ = qp.shape
    nq = length // tq
    return pl.pallas_call(
        functools.partial(_flash_kernel, tq=tq, nh=nh),
        grid=(bsz, MLA_HEADS // nh, nq),
        in_specs=[pl.BlockSpec((None, tq, nh * MLA_QK_PAD), lambda b, h, i: (b, i, h)),
                  pl.BlockSpec((None, length, nh * MLA_NOPE), lambda b, h, i: (b, 0, h)),
                  pl.BlockSpec((None, length, 128), lambda b, h, i: (b, 0, 0)),
                  pl.BlockSpec((None, nh * MLA_V, length), lambda b, h, i: (b, h, 0))],
        out_specs=pl.BlockSpec((None, tq, nh * MLA_V), lambda b, h, i: (b, i, h)),
        out_shape=jax.ShapeDtypeStruct((bsz, length, MLA_HEADS * MLA_V), BF16),
        scratch_shapes=[pltpu.VMEM((nh, length, MLA_QK_PAD), BF16)],
        compiler_params=pltpu.CompilerParams(dimension_semantics=("arbitrary", "arbitrary", "arbitrary"),
                                             vmem_limit_bytes=VMEM_LIMIT_BYTES),
        name="mla_flash",
    )(qp, kn, krp, vt)


def _kvup_kernel(c_ref, w_ref, kn_o, v_o):
    kv = _dot(c_ref[...].astype(BF16), w_ref[...])
    kn_o[...] = kv[:, 0:MLA_HEADS * MLA_NOPE].astype(BF16)
    v_o[...] = kv[:, MLA_HEADS * MLA_NOPE:].astype(BF16)


def _kvup(ckv, w_kv, tr):
    bsz, length, _ = ckv.shape
    wide = MLA_HEADS * MLA_NOPE
    return pl.pallas_call(
        _kvup_kernel,
        grid=(bsz, length // tr),
        in_specs=[pl.BlockSpec((None, tr, MLA_KV_LORA), lambda b, l: (b, l, 0)), _const_spec(w_kv.shape)],
        out_specs=[pl.BlockSpec((None, tr, wide), lambda b, l: (b, l, 0)),
                   pl.BlockSpec((None, tr, wide), lambda b, l: (b, l, 0))],
        out_shape=[jax.ShapeDtypeStruct((bsz, length, wide), BF16), jax.ShapeDtypeStruct((bsz, length, wide), BF16)],
        compiler_params=pltpu.CompilerParams(dimension_semantics=("arbitrary", "arbitrary"),
                                             vmem_limit_bytes=VMEM_LIMIT_BYTES),
        name="kv_up_cached",
    )(ckv, w_kv)


def _cached_attn_kernel(q_ref, knp_ref, krp_ref, vp_ref, knn_ref, krn_ref, vtn_ref, o_ref, *, past):
    lq = q_ref.shape[0]
    q_chunk = (past + lax.broadcasted_iota(jnp.int32, (lq, past), 0)) // CHUNK
    vis_p = lax.broadcasted_iota(jnp.int32, (lq, past), 1) // CHUNK <= q_chunk
    q_chunk_n = (past + lax.broadcasted_iota(jnp.int32, (lq, lq), 0)) // CHUNK
    vis_n = (past + lax.broadcasted_iota(jnp.int32, (lq, lq), 1)) // CHUNK <= q_chunk_n
    kr_past = krp_ref[...].astype(BF16)
    kr_new = krn_ref[...]
    for hh in range(MLA_HEADS):
        base = hh * MLA_QK_PAD
        qn = q_ref[:, base:base + MLA_NOPE]
        qr = q_ref[:, base + MLA_NOPE:base + MLA_QK_PAD]
        hc = slice(hh * MLA_NOPE, (hh + 1) * MLA_NOPE)
        s_p = _dot_nt(qn, knp_ref[:, hc]) + _dot_nt(qr[:, 0:MLA_ROPE], kr_past)
        s_n = _dot_nt(qn, knn_ref[:, hc]) + _dot_nt(qr, kr_new)
        s_p = jnp.where(vis_p, s_p, -jnp.inf)
        s_n = jnp.where(vis_n, s_n, -jnp.inf)
        m = jnp.maximum(jnp.max(s_p, axis=-1, keepdims=True), jnp.max(s_n, axis=-1, keepdims=True))
        p_p = jnp.exp2(s_p - m)
        p_n = jnp.exp2(s_n - m)
        l = jnp.sum(p_p, axis=-1, keepdims=True) + jnp.sum(p_n, axis=-1, keepdims=True)
        vc = slice(hh * MLA_V, (hh + 1) * MLA_V)
        o = _dot(p_p.astype(BF16), vp_ref[:, vc]) + _dot_nt(p_n.astype(BF16), vtn_ref[vc, :])
        o_ref[:, vc] = (o / l).astype(BF16)


def _cached_attention(qp, kn_past, kr_past, v_past, kn_new, krp_new, vt_new):
    bsz, lq, _ = qp.shape
    past = kn_past.shape[1]
    wide = MLA_HEADS * MLA_NOPE

    def per_b(rows, width):
        return pl.BlockSpec((None, rows, width), lambda b: (b, 0, 0))

    return pl.pallas_call(
        functools.partial(_cached_attn_kernel, past=past),
        grid=(bsz,),
        in_specs=[per_b(lq, MLA_HEADS * MLA_QK_PAD), per_b(past, wide), per_b(past, MLA_ROPE), per_b(past, wide),
                  per_b(lq, wide), per_b(lq, 128), per_b(wide, lq)],
        out_specs=per_b(lq, MLA_HEADS * MLA_V),
        out_shape=jax.ShapeDtypeStruct((bsz, lq, MLA_HEADS * MLA_V), BF16),
        compiler_params=pltpu.CompilerParams(dimension_semantics=("arbitrary",),
                                             vmem_limit_bytes=VMEM_LIMIT_BYTES),
        name="mla_cached",
    )(qp, kn_past, kr_past, v_past, kn_new, krp_new, vt_new)


def _memproj_kernel(m_ref, wk_ref, wv_ref, k_o, v_o):
    mb = m_ref[...].astype(BF16)
    k_o[...] = _dot(mb, wk_ref[...])
    v_o[...] = _dot(mb, wv_ref[...])


def _memproj(mem, w_mk, w_mv):
    bsz, tokens, _ = mem.shape
    wide = MEM_HEADS * MEM_HEAD_DIM
    return pl.pallas_call(
        _memproj_kernel,
        grid=(bsz,),
        in_specs=[pl.BlockSpec((None, tokens, D_MODEL), lambda b: (b, 0, 0)),
                  _const_spec(w_mk.shape), _const_spec(w_mv.shape)],
        out_specs=[pl.BlockSpec((None, tokens, wide), lambda b: (b, 0, 0)),
                   pl.BlockSpec((None, tokens, wide), lambda b: (b, 0, 0))],
        out_shape=[jax.ShapeDtypeStruct((bsz, tokens, wide), F32), jax.ShapeDtypeStruct((bsz, tokens, wide), F32)],
        compiler_params=pltpu.CompilerParams(dimension_semantics=("arbitrary",),
                                             vmem_limit_bytes=VMEM_LIMIT_BYTES),
        name="mem_proj",
    )(mem, w_mk.astype(BF16), w_mv.astype(BF16))


def _route(logits_t, bias_col):
    scores = _sigmoid(logits_t)
    sel = scores + bias_col
    t = logits_t.shape[1]
    sc = [scores[j * N_GROUPS:(j + 1) * N_GROUPS, :] for j in range(GROUP_SIZE)]
    se = [sel[j * N_GROUPS:(j + 1) * N_GROUPS, :] for j in range(GROUP_SIZE)]
    top1 = se[0]
    top2 = jnp.full_like(top1, -jnp.inf)
    for j in range(1, GROUP_SIZE):
        top2 = jnp.maximum(top2, jnp.minimum(top1, se[j]))
        top1 = jnp.maximum(top1, se[j])
    gs = top1 + top2
    gidx = lax.broadcasted_iota(jnp.int32, (N_GROUPS, t), 0)
    beaten = jnp.zeros((N_GROUPS, t), F32)
    for g2 in range(N_GROUPS):
        row = gs[g2:g2 + 1, :]
        tie = jnp.where(gidx > g2, 1.0, 0.0)
        beaten = beaten + jnp.where(row > gs, 1.0, jnp.where(row == gs, tie, 0.0))
    keep = beaten < TOPK_GROUPS
    ms = [jnp.where(keep, se[j], -jnp.inf) for j in range(GROUP_SIZE)]
    cnt = [jnp.zeros((N_GROUPS, t), F32) for _ in range(GROUP_SIZE)]
    for j2 in range(GROUP_SIZE):
        for g2 in range(N_GROUPS):
            row = ms[j2][g2:g2 + 1, :]
            for j in range(GROUP_SIZE):
                if j2 < j:
                    tie = jnp.where(gidx >= g2, 1.0, 0.0)
                else:
                    tie = jnp.where(gidx > g2, 1.0, 0.0)
                cnt[j] = cnt[j] + jnp.where(row > ms[j], 1.0, jnp.where(row == ms[j], tie, 0.0))
    w = [jnp.where(cnt[j] < TOP_K, sc[j], 0.0) for j in range(GROUP_SIZE)]
    tot = w[0]
    for j in range(1, GROUP_SIZE):
        tot = tot + w[j]
    denom = jnp.sum(tot, axis=0, keepdims=True) + 1e-20
    return [w[j] / denom * ROUTED_SCALE for j in range(GROUP_SIZE)]


def _postmix_kernel(x_ref, og_ref, om_ref, gate_ref, mk_ref, mv_ref, lng_ref, lnb_ref, wpg_ref, wpm_ref, wout_ref,
                    ln1g_ref, ln1b_ref, wmq_ref, wmo_ref, ln2g_ref, ln2b_ref, wrt_ref, rb_ref,
                    h2_o, gt_o, *, parts):
    pr = x_ref.shape[0] // parts
    rs = [slice(i * pr, (i + 1) * pr) for i in range(parts)]
    mk = mk_ref[...].astype(BF16)
    mv = mv_ref[...].astype(BF16)
    h = [_layer_norm(x_ref[r, :], lng_ref[...], lnb_ref[...]) for r in rs]
    bg = [_dot(og_ref[r, :], wpg_ref[...]) for r in rs]
    bm = [_dot(om_ref[r, :], wpm_ref[...]) for r in rs]
    merged = [(_sigmoid(gate_ref[r, 0:D_MODEL].astype(F32)) * bg[i]
               + _sigmoid(gate_ref[r, D_MODEL:2 * D_MODEL].astype(F32)) * bm[i]).astype(BF16)
              for i, r in enumerate(rs)]
    mix = [_dot(m, wout_ref[...]) for m in merged]
    h1 = [_layer_norm(DEEPNORM_ALPHA * h[i] + mix[i], ln1g_ref[...], ln1b_ref[...]) for i in range(parts)]
    q = [_dot(v.astype(BF16), wmq_ref[...]).astype(BF16) for v in h1]
    outs = [[] for _ in range(parts)]
    for hh in range(MEM_HEADS):
        hc = slice(hh * MEM_HEAD_DIM, (hh + 1) * MEM_HEAD_DIM)
        s = [_dot_nt(q[i][:, hc], mk[:, hc]) * (MEM_HEAD_DIM ** -0.5) for i in range(parts)]
        p = [jnp.exp(v - jnp.max(v, axis=-1, keepdims=True)) for v in s]
        p = [(v / jnp.sum(v, axis=-1, keepdims=True)).astype(BF16) for v in p]
        for i in range(parts):
            outs[i].append(_dot(p[i], mv[:, hc]).astype(BF16))
    o = [_dot(jnp.concatenate(v, axis=-1), wmo_ref[...]) for v in outs]
    h2 = [_layer_norm(DEEPNORM_ALPHA * h1[i] + o[i], ln2g_ref[...], ln2b_ref[...]) for i in range(parts)]
    logits_t = [_dot_nt(wrt_ref[...], v, precision=lax.Precision.HIGHEST) for v in h2]
    for i, r in enumerate(rs):
        h2_o[r, :] = h2[i]
        gates = _route(logits_t[i], rb_ref[...])
        for j in range(GROUP_SIZE):
            for g in range(N_GROUPS):
                gt_o[g * GROUP_SIZE + j:g * GROUP_SIZE + j + 1, r] = gates[j][g:g + 1, :]


def _postmix(x, og, om, gate, mem_k, mem_v, pw, td):
    bsz, length, _ = x.shape
    nl = length // td
    tokens = mem_k.shape[1]
    wide = MEM_HEADS * MEM_HEAD_DIM
    consts = [pw['ln_in_g'], pw['ln_in_b'], pw['w_proj_gdn'], pw['w_proj_mla'], pw['w_out'], pw['ln1_g'],
              pw['ln1_b'], pw['w_mq'], pw['w_mo'], pw['ln2_g'], pw['ln2_b'], pw['w_router_t'], pw['router_bias']]

    def rows(width):
        return pl.BlockSpec((None, td, width), lambda b, l: (b, l, 0))

    return pl.pallas_call(
        functools.partial(_postmix_kernel, parts=max(1, td // 256)),
        grid=(bsz, nl),
        in_specs=[rows(D_MODEL), rows(D_MODEL), rows(D_MODEL), rows(2 * D_MODEL),
                  pl.BlockSpec((None, tokens, wide), lambda b, l: (b, 0, 0)),
                  pl.BlockSpec((None, tokens, wide), lambda b, l: (b, 0, 0))]
                 + [_const_spec(c.shape) for c in consts],
        out_specs=[rows(D_MODEL), pl.BlockSpec((None, N_EXPERTS, td), lambda b, l: (b, 0, l))],
        out_shape=[jax.ShapeDtypeStruct((bsz, length, D_MODEL), F32),
                   jax.ShapeDtypeStruct((bsz, N_EXPERTS, length), F32)],
        compiler_params=pltpu.CompilerParams(dimension_semantics=("arbitrary", "arbitrary"),
                                             vmem_limit_bytes=VMEM_LIMIT_BYTES),
        name="postmix",
    )(x, og, om, gate, mem_k, mem_v, *consts)


def _moe_kernel(h_ref, g_ref, wgu_ref, wd_ref, wsgu_ref, wsd_ref, ln3g_ref, ln3b_ref, y_o, xb, acc, *, eb):
    e = pl.program_id(1)

    @pl.when(e == 0)
    def _():
        x = h_ref[...].astype(BF16)
        xb[...] = x
        sgu = _dot(x, wsgu_ref[...])
        hid = _silu(sgu[:, 0:SHARED_FF]) * sgu[:, SHARED_FF:]
        acc[...] = _dot(hid.astype(BF16), wsd_ref[...])

    x = xb[...]
    gb = g_ref[...].astype(BF16)
    erow = lax.broadcasted_iota(jnp.int32, (N_EXPERTS, EXPERT_FF), 0)
    for i in range(eb):
        gu = _dot(x, wgu_ref[i])
        pick = jnp.where(erow == e * eb + i, 1.0, 0.0).astype(BF16)
        gcol = _dot(gb, pick)
        hid = _silu(gu[:, 0:EXPERT_FF]) * gu[:, EXPERT_FF:] * gcol
        acc[...] += _dot(hid.astype(BF16), wd_ref[i])

    @pl.when(e == pl.num_programs(1) - 1)
    def _():
        y_o[...] = _layer_norm(DEEPNORM_ALPHA * h_ref[...] + acc[...], ln3g_ref[...], ln3b_ref[...])


def _moe(h2, gates, ew, tm, eb):
    n = h2.shape[0]
    consts = [ew['ws_gu'], ew['ws_down'], ew['ln3_g'], ew['ln3_b']]
    return pl.pallas_call(
        functools.partial(_moe_kernel, eb=eb),
        grid=(n // tm, N_EXPERTS // eb),
        in_specs=[pl.BlockSpec((tm, D_MODEL), lambda t, e: (t, 0)),
                  pl.BlockSpec((tm, N_EXPERTS), lambda t, e: (t, 0)),
                  pl.BlockSpec((eb, D_MODEL, 2 * EXPERT_FF), lambda t, e: (e, 0, 0)),
                  pl.BlockSpec((eb, EXPERT_FF, D_MODEL), lambda t, e: (e, 0, 0))]
                 + [_const_spec(c.shape) for c in consts],
        out_specs=pl.BlockSpec((tm, D_MODEL), lambda t, e: (t, 0)),
        out_shape=jax.ShapeDtypeStruct((n, D_MODEL), F32),
        scratch_shapes=[pltpu.VMEM((tm, D_MODEL), BF16), pltpu.VMEM((tm, D_MODEL), F32)],
        compiler_params=pltpu.CompilerParams(dimension_semantics=("arbitrary", "arbitrary"),
                                             vmem_limit_bytes=VMEM_LIMIT_BYTES),
        name="moe",
    )(h2, gates, ew['w_gu'], ew['w_down'], *consts)


MOE_ROUTED_TILE = 1024
MOE_ROUTED_SUB = 512
MOE_ROUTED_EXPERTS = 8
MOE_ROUTED_SLOTS = 96


def _moe_routed_kernel(h_ref, g_ref, wgu_ref, wd_ref, wsgu_ref, wsd_ref, ln3g_ref, ln3b_ref, y_o,
                       xb, rank_scr, p_scr, o_scr, *, eb, rb, ts):
    e = pl.program_id(2)
    t = h_ref.shape[0]
    ns = t // ts

    @pl.when(e == 0)
    def _():
        x = h_ref[...].astype(BF16)
        xb[...] = x
        sgu = _dot(x, wsgu_ref[...])
        hid = _silu(sgu[:, 0:SHARED_FF]) * sgu[:, SHARED_FF:]
        y_o[...] = _dot(hid.astype(BF16), wsd_ref[...])
        routed = jnp.where(g_ref[...] > 0.0, 1.0, 0.0).astype(BF16)
        upto = jnp.where(lax.broadcasted_iota(jnp.int32, (ts, ts), 0) <= lax.broadcasted_iota(jnp.int32, (ts, ts), 1),
                         1.0, 0.0).astype(BF16)
        for s in range(ns):
            rank_scr[:, s * ts:(s + 1) * ts] = _dot(routed[:, s * ts:(s + 1) * ts], upto)

    e0 = pl.multiple_of(e * eb, eb)
    gblk = g_ref[pl.ds(e0, eb), :]
    rblk = rank_scr[pl.ds(e0, eb), :]
    npass = (jnp.max(rblk).astype(jnp.int32) + (rb - 1)) // rb
    slot = lax.broadcasted_iota(jnp.int32, (rb, ts), 0).astype(F32)

    def one_pass(k, carry):
        base = (k * rb).astype(F32)
        xg, gate = [], []
        for i in range(eb):
            gi = gblk[i:i + 1, :]
            pos = jnp.where(gi > 0.0, rblk[i:i + 1, :] - 1.0 - base, -1.0)
            xgs, gates = [], []
            for s in range(ns):
                cs = slice(s * ts, (s + 1) * ts)
                hit = pos[:, cs] == slot
                pb = jnp.where(hit, 1.0, 0.0).astype(BF16)
                p_scr[s, i * rb:(i + 1) * rb, :] = pb
                gates.append(jnp.sum(jnp.where(hit, gi[:, cs], 0.0), axis=1, keepdims=True))
                xgs.append(_dot(pb, xb[cs, :]).astype(BF16))
            xg.append(jnp.concatenate(xgs, axis=0))
            gate.append(jnp.concatenate(gates, axis=0))
        gu = [_dot(xg[i], wgu_ref[i]) for i in range(eb)]
        hid = [(_silu(gu[i][:, 0:EXPERT_FF]) * gu[i][:, EXPERT_FF:] * gate[i]).astype(BF16) for i in range(eb)]
        for i in range(eb):
            out = _dot(hid[i], wd_ref[i]).astype(BF16)
            for s in range(ns):
                o_scr[s, i * rb:(i + 1) * rb, :] = out[s * rb:(s + 1) * rb, :]
        for s in range(ns):
            y_o[s * ts:(s + 1) * ts, :] += _dot_tn(p_scr[s], o_scr[s])
        return carry

    lax.fori_loop(0, npass, one_pass, 0)

    @pl.when(e == pl.num_programs(2) - 1)
    def _():
        y_o[...] = _layer_norm(DEEPNORM_ALPHA * h_ref[...] + y_o[...], ln3g_ref[...], ln3b_ref[...])


def _moe_routed(h2, gt, ew, tm, ts, eb, rb):
    bsz, length, _ = h2.shape
    consts = [ew['ws_gu'], ew['ws_down'], ew['ln3_g'], ew['ln3_b']]
    return pl.pallas_call(
        functools.partial(_moe_routed_kernel, eb=eb, rb=rb, ts=ts),
        grid=(bsz, length // tm, N_EXPERTS // eb),
        in_specs=[pl.BlockSpec((None, tm, D_MODEL), lambda b, t, e: (b, t, 0), pipeline_mode=pl.Buffered(1)),
                  pl.BlockSpec((None, N_EXPERTS, tm), lambda b, t, e: (b, 0, t)),
                  pl.BlockSpec((eb, D_MODEL, 2 * EXPERT_FF), lambda b, t, e: (e, 0, 0)),
                  pl.BlockSpec((eb, EXPERT_FF, D_MODEL), lambda b, t, e: (e, 0, 0))]
                 + [_const_spec(c.shape) for c in consts],
        out_specs=pl.BlockSpec((None, tm, D_MODEL), lambda b, t, e: (b, t, 0)),
        out_shape=jax.ShapeDtypeStruct((bsz, length, D_MODEL), F32),
        scratch_shapes=[pltpu.VMEM((tm, D_MODEL), BF16), pltpu.VMEM((N_EXPERTS, tm), F32),
                        pltpu.VMEM((tm // ts, eb * rb, ts), BF16), pltpu.VMEM((tm // ts, eb * rb, D_MODEL), BF16)],
        compiler_params=pltpu.CompilerParams(dimension_semantics=("arbitrary", "arbitrary", "arbitrary"),
                                             vmem_limit_bytes=VMEM_LIMIT_LARGE_BYTES),
        name="moe_routed",
    )(h2, gt, ew['w_gu'], ew['w_down'], *consts)


def _prep_post_weights(ln_in_g, ln_in_b, w_proj_gdn, w_proj_mla, w_out, ln1_g, ln1_b, w_mq, w_mo, ln2_g, ln2_b,
                       w_router, router_bias):
    def row(v):
        return v.reshape(1, -1).astype(F32)

    wrt = w_router.T.reshape(N_GROUPS, GROUP_SIZE, D_MODEL).transpose(1, 0, 2).reshape(N_EXPERTS, D_MODEL)
    rb = router_bias.reshape(N_GROUPS, GROUP_SIZE).T.reshape(N_EXPERTS, 1)
    return dict(ln_in_g=row(ln_in_g), ln_in_b=row(ln_in_b), w_proj_gdn=w_proj_gdn.astype(BF16),
                w_proj_mla=w_proj_mla.astype(BF16), w_out=w_out.astype(BF16), ln1_g=row(ln1_g), ln1_b=row(ln1_b),
                w_mq=w_mq.astype(BF16), w_mo=w_mo.astype(BF16), ln2_g=row(ln2_g), ln2_b=row(ln2_b),
                w_router_t=wrt.astype(F32), router_bias=rb.astype(F32))


def _prep_expert_weights(w_gate, w_up, w_down, ws_gate, ws_up, ws_down, ln3_g, ln3_b):
    return dict(w_gu=jnp.concatenate([w_gate, w_up], axis=-1).astype(BF16), w_down=w_down.astype(BF16),
                ws_gu=jnp.concatenate([ws_gate, ws_up], axis=-1).astype(BF16), ws_down=ws_down.astype(BF16),
                ln3_g=ln3_g.reshape(1, -1).astype(F32), ln3_b=ln3_b.reshape(1, -1).astype(F32))


def _tail(x, og, om, gate, mem_k, mem_v, pw, ew, td):
    bsz, length, _ = x.shape
    h2, gt = _postmix(x, og, om, gate, mem_k, mem_v, pw, td)
    if length % MOE_ROUTED_TILE == 0:
        return _moe_routed(h2, gt, ew, MOE_ROUTED_TILE, MOE_ROUTED_SUB, MOE_ROUTED_EXPERTS, MOE_ROUTED_SLOTS)
    gates = gt.transpose(0, 2, 1).reshape(bsz * length, N_EXPERTS)
    y = _moe(h2.reshape(bsz * length, D_MODEL), gates, ew, bsz * length, 2)
    return y.reshape(bsz, length, D_MODEL)


def kernel(x_prompt, x_sample, mem_prompt, state_gdn_conv, state_gdn_ssm, cache_mla_ckv, cache_mla_krope,
           cache_mem_k, cache_mem_v, ln_in_g, ln_in_b, w_in, w_conv, a_log, dt_bias, gdn_norm_g, w_proj_gdn,
           q_norm_g, w_uq, kv_norm_g, w_ukv, w_proj_mla, w_out, ln1_g, ln1_b, w_mq, w_mk, w_mv, w_mo,
           ln2_g, ln2_b, w_router, router_bias, w_gate, w_up, w_down, ws_gate, ws_up, ws_down, ln3_g, ln3_b):
    lyr = 0
    bsz, seq, _ = x_prompt.shape
    dbsz, dseq, _ = x_sample.shape
    past = cache_mla_ckv.shape[2]
    mw = _prep_mixer_weights(w_in[lyr], w_conv[lyr], a_log[lyr], dt_bias[lyr], q_norm_g[lyr], w_uq[lyr],
                             kv_norm_g[lyr], w_ukv[lyr])
    pw = _prep_post_weights(ln_in_g, ln_in_b, w_proj_gdn[lyr], w_proj_mla[lyr], w_out[lyr], ln1_g[lyr], ln1_b[lyr],
                            w_mq[lyr], w_mo[lyr], ln2_g[lyr], ln2_b[lyr], w_router[lyr], router_bias[lyr])
    ew = _prep_expert_weights(w_gate[lyr], w_up[lyr], w_down[lyr], ws_gate[lyr], ws_up[lyr], ws_down[lyr],
                              ln3_g[lyr], ln3_b[lyr])
    wide = MEM_HEADS * MEM_HEAD_DIM

    conv0 = jnp.zeros((bsz, GDN_CONV - 1, GDN_CONV_CH), F32)
    s0 = jnp.zeros((bsz, GDN_HEADS, GDN_DK, GDN_DV), F32)
    (qkv, z, gate, qp, kn, krp, v, ckv_p, kr_p, conv_p, gbc, gbr) = _inproj(
        x_prompt, ln_in_g, ln_in_b, mw, conv0, 0, tl=min(256, seq))
    og, ssm_p = _gdn(qkv, gbc, gbr, z, s0, gdn_norm_g[lyr], chunk=CHUNK, tg=min(256, seq), nb=math.gcd(bsz, 4))
    om = _flash_attention(qp, kn, krp, v, tq=min(1024, seq), nh=2)
    mem_k, mem_v = _memproj(mem_prompt, w_mk[lyr], w_mv[lyr])
    y_p = _tail(x_prompt, og, om, gate, mem_k, mem_v, pw, ew, td=min(1024, seq))

    (qkv, z, gate, qp, kn, krp, v, ckv_s, kr_s, conv_s, gbc, gbr) = _inproj(
        x_sample, ln_in_g, ln_in_b, mw, state_gdn_conv[lyr], past, tl=dseq)
    chunk_s = CHUNK if dseq % CHUNK == 0 else dseq
    og, ssm_s = _gdn(qkv, gbc, gbr, z, state_gdn_ssm[lyr], gdn_norm_g[lyr], chunk=chunk_s, tg=dseq, nb=math.gcd(dbsz, 2))
    kn_past, v_past = _kvup(cache_mla_ckv[lyr], mw['w_kv'], tr=min(1024, past))
    om = _cached_attention(qp, kn_past, cache_mla_krope[lyr], v_past, kn, krp, v)
    y_s = _tail(x_sample, og, om, gate, cache_mem_k[lyr].reshape(dbsz, -1, wide),
                cache_mem_v[lyr].reshape(dbsz, -1, wide), pw, ew, td=dseq)

    return (y_p, y_s, conv_p[None], ssm_p[None], ckv_p[None], kr_p[None],
            mem_k.reshape(1, bsz, -1, MEM_HEADS, MEM_HEAD_DIM), mem_v.reshape(1, bsz, -1, MEM_HEADS, MEM_HEAD_DIM),
            conv_s[None], ssm_s[None], ckv_s[None], kr_s[None])
```

```python
import functools
import math

import jax
import jax.numpy as jnp
import numpy as np
from jax import lax
from jax.experimental import pallas as pl
from jax.experimental.pallas import tpu as pltpu

F32 = jnp.float32
BF16 = jnp.bfloat16

D_MODEL = 1024
CHUNK = 64
GDN_HEADS = 8
GDN_DK = 128
GDN_DV = 128
GDN_CONV = 4
GDN_QK = GDN_HEADS * GDN_DK
GDN_CONV_CH = 2 * GDN_QK + GDN_HEADS * GDN_DV
MLA_HEADS = 8
MLA_Q_LORA = 384
MLA_KV_LORA = 256
MLA_NOPE = 128
MLA_ROPE = 64
MLA_V = 128
MLA_SCALE = (MLA_NOPE + MLA_ROPE) ** -0.5
MLA_SCALE_LOG2 = MLA_SCALE * math.log2(math.e)
MLA_QK_PAD = 256
ROPE_THETA = 10000.0
MEM_HEADS = 4
MEM_HEAD_DIM = 128
N_EXPERTS = 64
TOP_K = 8
N_GROUPS = 8
GROUP_SIZE = N_EXPERTS // N_GROUPS
TOPK_GROUPS = 4
EXPERT_FF = 256
SHARED_FF = 256
ROUTED_SCALE = 2.5
DEPTH = 1
DEEPNORM_ALPHA = (2.0 * DEPTH) ** 0.25
IN_WIDTHS = (GDN_CONV_CH, GDN_HEADS * GDN_DV, GDN_HEADS, GDN_HEADS, MLA_Q_LORA, MLA_KV_LORA, MLA_ROPE,
             D_MODEL, D_MODEL)

VMEM_LIMIT_BYTES = 56 * 1024 * 1024
VMEM_LIMIT_LARGE_BYTES = 60 * 1024 * 1024


def _dot(a, b, precision=None):
    return jnp.dot(a, b, preferred_element_type=F32, precision=precision)


def _dot_nt(a, b, precision=None):
    return lax.dot_general(a, b, (((1,), (1,)), ((), ())), preferred_element_type=F32, precision=precision)


def _dot_tn(a, b):
    return lax.dot_general(a, b, (((0,), (0,)), ((), ())), preferred_element_type=F32)


def _layer_norm(x, g, b, eps=1e-5):
    mu = jnp.mean(x, axis=-1, keepdims=True)
    xc = x - mu
    var = jnp.mean(xc * xc, axis=-1, keepdims=True)
    return xc * lax.rsqrt(var + eps) * g + b


def _rms_norm(x, g, eps=1e-6):
    return x * lax.rsqrt(jnp.mean(x * x, axis=-1, keepdims=True) + eps) * g


def _sigmoid(x):
    return 1.0 / (1.0 + jnp.exp(-x))


def _silu(x):
    return x * _sigmoid(x)


def _softplus(x):
    return jnp.maximum(x, 0.0) + jnp.log1p(jnp.exp(-jnp.abs(x)))


def _const_spec(shape):
    nd = len(shape)
    return pl.BlockSpec(shape, lambda *_: (0,) * nd, pipeline_mode=pl.Buffered(1))


def _inproj_kernel(x_ref, lng_ref, lnb_ref, wqkv_ref, wz_ref, wgate_ref, wcq_ref, wckv_ref, wkr_ref, wab_ref,
                   wabt_ref, convw_ref, conv0_ref, arow_ref, acol_ref, qng_ref, wuq_ref, wuqs_ref, kvng_ref,
                   wuk_ref, wuvt_ref, t1_ref, t2_ref,
                   qkv_o, z_o, gate_o, qp_o, kn_o, krp_o, vt_o, ckv_o, kr_o, conv_o, gbc_o, gbr_o,
                   cbuf):
    tl = x_ref.shape[0]
    li = pl.program_id(1)

    h = _layer_norm(x_ref[...], lng_ref[...], lnb_ref[...])
    hb = h.astype(BF16)

    @pl.when(li == 0)
    def _():
        cbuf[0:8, :] = conv0_ref[...]

    cbuf[8:8 + tl, :] = _dot(hb, wqkv_ref[...])
    conv_o[...] = cbuf[tl + 5:tl + 8, :]
    for part in range(3):
        cols = slice(part * GDN_QK, (part + 1) * GDN_QK)
        y = convw_ref[0:1, cols] * cbuf[5:5 + tl, cols]
        for i in range(1, GDN_CONV):
            y = y + convw_ref[i:i + 1, cols] * cbuf[5 + i:5 + i + tl, cols]
        y = _silu(y)
        if part < 2:
            for hh in range(GDN_HEADS):
                hc = slice(hh * GDN_DK, (hh + 1) * GDN_DK)
                yh = y[:, hc]
                r = lax.rsqrt(jnp.sum(yh * yh, axis=-1, keepdims=True) + 1e-6)
                yh = yh * r
                if part == 0:
                    yh = yh * (GDN_DK ** -0.5)
                qkv_o[:, part * GDN_QK + hh * GDN_DK:part * GDN_QK + (hh + 1) * GDN_DK] = yh.astype(BF16)
        else:
            qkv_o[:, cols] = y.astype(BF16)
    cbuf[0:8, :] = cbuf[tl:tl + 8, :]

    z_o[...] = _dot(hb, wz_ref[...]).astype(BF16)
    gate_o[...] = _dot(hb, wgate_ref[...]).astype(BF16)

    ab = _dot(hb, wab_ref[...])
    arow = arow_ref[...]
    g_c = arow[0:1, :] * _softplus(ab[:, 0:GDN_HEADS] + arow[1:2, :])
    b_c = _sigmoid(ab[:, GDN_HEADS:2 * GDN_HEADS])
    gbc_o[:, 0:GDN_HEADS] = g_c
    gbc_o[:, GDN_HEADS:2 * GDN_HEADS] = b_c
    abt = _dot_nt(wabt_ref[...], hb)
    acol = acol_ref[...]
    gbr_o[0:GDN_HEADS, :] = acol[:, 0:1] * _softplus(abt[0:GDN_HEADS, :] + acol[:, 1:2])
    gbr_o[GDN_HEADS:2 * GDN_HEADS, :] = _sigmoid(abt[GDN_HEADS:2 * GDN_HEADS, :])

    t1 = t1_ref[...]
    t2 = t2_ref[...]

    cq = _rms_norm(_dot(hb, wcq_ref[...]), qng_ref[...]).astype(BF16)
    qm = _dot(cq, wuq_ref[...])
    qs = _dot(cq, wuqs_ref[...])
    for hh in range(MLA_HEADS):
        base = hh * MLA_QK_PAD
        qp_o[:, base:base + MLA_NOPE] = (qm[:, base:base + MLA_NOPE] * MLA_SCALE_LOG2).astype(BF16)
        hi = qm[:, base + MLA_NOPE:base + MLA_QK_PAD] * t1 + qs[:, hh * 128:(hh + 1) * 128] * t2
        qp_o[:, base + MLA_NOPE:base + MLA_QK_PAD] = (hi * MLA_SCALE_LOG2).astype(BF16)

    ckv = _rms_norm(_dot(hb, wckv_ref[...]), kvng_ref[...])
    ckv_o[...] = ckv
    ckvb = ckv.astype(BF16)
    kn_o[...] = _dot(ckvb, wuk_ref[...]).astype(BF16)
    vt_o[...] = _dot_nt(wuvt_ref[...], ckvb).astype(BF16)
    krm = _dot(hb, wkr_ref[...])
    krp = krm[:, 0:128] * t1 + krm[:, 128:256] * t2
    kr_o[...] = krp[:, 0:MLA_ROPE]
    krp_o[...] = krp.astype(BF16)


def _swap_halves(w):
    half = w.shape[-1] // 2
    return jnp.concatenate([w[..., half:], w[..., :half]], axis=-1)


def _rope_tables(pos0, length):
    inv_freq = np.power(ROPE_THETA, -np.arange(0, MLA_ROPE, 2, dtype=np.float64) / MLA_ROPE)
    ang = (pos0 + np.arange(length)).astype(np.float64)[:, None] * inv_freq[None, :]
    c, s = np.cos(ang).astype(np.float32), np.sin(ang).astype(np.float32)
    zeros = np.zeros((length, 128 - MLA_ROPE), np.float32)
    return (jnp.asarray(np.concatenate([c, c, zeros], axis=-1)),
            jnp.asarray(np.concatenate([-s, s, zeros], axis=-1)))


def _prep_mixer_weights(w_in, w_conv, a_log, dt_bias, q_norm_g, w_uq, kv_norm_g, w_ukv):
    splits = []
    off = 0
    for wd in IN_WIDTHS:
        splits.append(w_in[:, off:off + wd])
        off += wd
    w_qkv, w_z, w_a, w_b, w_cq, w_ckv, w_kr, w_gg, w_gm = splits
    z64 = jnp.zeros((D_MODEL, 128 - MLA_ROPE), F32)
    w_krg = jnp.concatenate([w_kr, z64, _swap_halves(w_kr), z64], axis=-1)
    w_ab = jnp.concatenate([w_a, w_b], axis=-1)
    wq = w_uq.reshape(MLA_Q_LORA, MLA_HEADS, MLA_NOPE + MLA_ROPE)
    nope, rope = wq[:, :, :MLA_NOPE], wq[:, :, MLA_NOPE:]
    zq = jnp.zeros((MLA_Q_LORA, MLA_HEADS, MLA_QK_PAD - MLA_NOPE - MLA_ROPE), F32)
    wq_main = jnp.concatenate([nope, rope, zq], axis=-1).reshape(MLA_Q_LORA, MLA_HEADS * MLA_QK_PAD)
    wq_swap = jnp.concatenate([_swap_halves(rope), zq], axis=-1).reshape(MLA_Q_LORA, MLA_HEADS * 128)
    wkv = w_ukv.reshape(MLA_KV_LORA, MLA_HEADS, MLA_NOPE + MLA_V)
    w_kv = jnp.concatenate([wkv[:, :, :MLA_NOPE].reshape(MLA_KV_LORA, -1),
                            wkv[:, :, MLA_NOPE:].reshape(MLA_KV_LORA, -1)], axis=-1)
    neg_a = -jnp.exp(a_log.astype(F32))
    arow = jnp.stack([neg_a, dt_bias.astype(F32)], axis=0)
    return dict(
        w_qkv=w_qkv.astype(BF16), w_z=w_z.astype(BF16),
        w_gate=jnp.concatenate([w_gg, w_gm], axis=-1).astype(BF16),
        w_cq=w_cq.astype(BF16), w_ckv=w_ckv.astype(BF16), w_krg=w_krg.astype(BF16),
        w_ab=w_ab.astype(BF16), w_abt=w_ab.T.astype(BF16),
        w_conv=w_conv.astype(F32), arow=arow, acol=arow.T,
        q_norm_g=q_norm_g.reshape(1, -1).astype(F32), wq_main=wq_main.astype(BF16), wq_swap=wq_swap.astype(BF16),
        kv_norm_g=kv_norm_g.reshape(1, -1).astype(F32), w_kv=w_kv.astype(BF16),
        w_k=wkv[:, :, :MLA_NOPE].reshape(MLA_KV_LORA, -1).astype(BF16),
        w_vt=wkv[:, :, MLA_NOPE:].reshape(MLA_KV_LORA, -1).T.astype(BF16))


def _inproj(x, ln_g, ln_b, mw, conv_state, pos0, tl):
    bsz, length, _ = x.shape
    nl = length // tl
    t1, t2 = _rope_tables(pos0, length)
    conv0 = jnp.concatenate([jnp.zeros((bsz, 8 - (GDN_CONV - 1), GDN_CONV_CH), F32), conv_state.astype(F32)], axis=1)

    def rows(width, dtype):
        return (jax.ShapeDtypeStruct((bsz, length, width), dtype),
                pl.BlockSpec((None, tl, width), lambda b, l: (b, l, 0)))

    outs = [rows(GDN_CONV_CH, BF16), rows(GDN_HEADS * GDN_DV, BF16), rows(2 * D_MODEL, BF16),
            rows(MLA_HEADS * MLA_QK_PAD, BF16), rows(MLA_HEADS * MLA_NOPE, BF16), rows(128, BF16),
            (jax.ShapeDtypeStruct((bsz, MLA_HEADS * MLA_V, length), BF16),
             pl.BlockSpec((None, MLA_HEADS * MLA_V, tl), lambda b, l: (b, 0, l))),
            rows(MLA_KV_LORA, F32), rows(MLA_ROPE, F32),
            (jax.ShapeDtypeStruct((bsz, GDN_CONV - 1, GDN_CONV_CH), F32),
             pl.BlockSpec((None, GDN_CONV - 1, GDN_CONV_CH), lambda b, l: (b, 0, 0))),
            rows(2 * GDN_HEADS, F32),
            (jax.ShapeDtypeStruct((bsz, 2 * GDN_HEADS, length), F32),
             pl.BlockSpec((None, 2 * GDN_HEADS, tl), lambda b, l: (b, 0, l)))]
    consts = [ln_g.reshape(1, -1), ln_b.reshape(1, -1), mw['w_qkv'], mw['w_z'], mw['w_gate'], mw['w_cq'],
              mw['w_ckv'], mw['w_krg'], mw['w_ab'], mw['w_abt'], mw['w_conv']]
    consts2 = [mw['arow'], mw['acol'], mw['q_norm_g'], mw['wq_main'], mw['wq_swap'], mw['kv_norm_g'], mw['w_k'],
               mw['w_vt']]
    in_specs = ([pl.BlockSpec((None, tl, D_MODEL), lambda b, l: (b, l, 0))]
                + [_const_spec(c.shape) for c in consts]
                + [pl.BlockSpec((None, 8, GDN_CONV_CH), lambda b, l: (b, 0, 0))]
                + [_const_spec(c.shape) for c in consts2]
                + [pl.BlockSpec((tl, 128), lambda b, l: (l, 0)), pl.BlockSpec((tl, 128), lambda b, l: (l, 0))])
    return pl.pallas_call(
        _inproj_kernel,
        grid=(bsz, nl),
        in_specs=in_specs,
        out_specs=[o[1] for o in outs],
        out_shape=[o[0] for o in outs],
        scratch_shapes=[pltpu.VMEM((tl + 8, GDN_CONV_CH), F32)],
        compiler_params=pltpu.CompilerParams(dimension_semantics=("arbitrary", "arbitrary"),
                                             vmem_limit_bytes=VMEM_LIMIT_BYTES),
        name="inproj",
    )(x, *consts, conv0, *consts2, t1, t2)


def _gdn_kernel(qkv_ref, gbc_ref, gbr_ref, z_ref, s0_ref, ng_ref, og_o, s_o, s_scr, *, chunk):
    tg = qkv_ref.shape[1]
    nchunks = tg // chunk
    li = pl.program_id(1)

    @pl.when(li == 0)
    def _():
        s_scr[...] = s0_ref[...]

    ri = lax.broadcasted_iota(jnp.int32, (chunk, chunk), 0)
    ci = lax.broadcasted_iota(jnp.int32, (chunk, chunk), 1)
    incl = ri >= ci
    strict = ri > ci
    tri = jnp.where(incl, 1.0, 0.0).astype(F32)
    eye = jnp.where(ri == ci, 1.0, 0.0).astype(F32)
    ng = ng_ref[...]
    n_sq = int(math.log2(chunk)) - 1

    nb = qkv_ref.shape[0]
    chains = [(g, hh) for g in range(nb) for hh in range(GDN_HEADS)]

    def chunk_body(c, carry):
        r0 = pl.multiple_of(c * chunk, chunk)
        rows = pl.ds(r0, chunk)
        gbc = [gbc_ref[g, rows, :] for g in range(nb)]
        gc_col = [_dot(tri, gbc[g], precision=lax.Precision.HIGHEST) for g in range(nb)]
        gc_row = [_dot_nt(gbr_ref[g, c], tri, precision=lax.Precision.HIGHEST) for g in range(nb)]
        q, k, qkk = {}, {}, {}
        for ch in chains:
            g, hh = ch
            q[ch] = qkv_ref[g, rows, hh * GDN_DK:(hh + 1) * GDN_DK]
            k[ch] = qkv_ref[g, rows, GDN_QK + hh * GDN_DK:GDN_QK + (hh + 1) * GDN_DK]
            qkk[ch] = _dot_nt(jnp.concatenate([q[ch], k[ch]], axis=0), k[ch])
        gcc, beta, gamma, g_end, qk, xp, t_inv = {}, {}, {}, {}, {}, {}, {}
        for ch in chains:
            g, hh = ch
            gcc[ch] = gc_col[g][:, hh:hh + 1]
            gcr = gc_row[g][hh:hh + 1, :]
            beta[ch] = gbc[g][:, GDN_HEADS + hh:GDN_HEADS + hh + 1]
            g_end[ch] = gc_col[g][chunk - 1:chunk, hh:hh + 1]
            gamma[ch] = jnp.exp(gcc[ch])
            decay = jnp.where(incl, jnp.exp(jnp.where(incl, gcc[ch] - gcr, 0.0)), 0.0)
            qk[ch] = (decay * qkk[ch][0:chunk, :]).astype(BF16)
            xp[ch] = jnp.where(strict, beta[ch] * decay * qkk[ch][chunk:2 * chunk, :], 0.0)
            t_inv[ch] = eye - xp[ch]
        for _ in range(n_sq):
            for ch in chains:
                xb = xp[ch].astype(BF16)
                xp[ch] = _dot(xb, xb)
            for ch in chains:
                t_inv[ch] = t_inv[ch] + _dot(t_inv[ch].astype(BF16), xp[ch].astype(BF16))
        sol, q_s, sb, s_old, k_dec = {}, {}, {}, {}, {}
        for ch in chains:
            g, hh = ch
            kf = k[ch].astype(F32)
            v = qkv_ref[g, rows, 2 * GDN_QK + hh * GDN_DV:2 * GDN_QK + (hh + 1) * GDN_DV].astype(F32)
            rhs = jnp.concatenate([beta[ch] * v, (beta[ch] * gamma[ch]) * kf], axis=-1).astype(BF16)
            sol[ch] = _dot(t_inv[ch].astype(BF16), rhs)
            k_dec[ch] = (jnp.exp(g_end[ch] - gcc[ch]) * kf).astype(BF16)
            s_old[ch] = s_scr[g, hh]
            sb[ch] = s_old[ch].astype(BF16)
            q_s[ch] = _dot((gamma[ch] * q[ch].astype(F32)).astype(BF16), sb[ch])
        ub = {}
        for ch in chains:
            u = sol[ch][:, :GDN_DV] - _dot(sol[ch][:, GDN_DV:].astype(BF16), sb[ch])
            ub[ch] = u.astype(BF16)
        for ch in chains:
            g, hh = ch
            o = q_s[ch] + _dot(qk[ch], ub[ch])
            s_scr[g, hh] = jnp.exp(g_end[ch]) * s_old[ch] + _dot_tn(k_dec[ch], ub[ch])
            zz = z_ref[g, rows, hh * GDN_DV:(hh + 1) * GDN_DV].astype(F32)
            og_o[g, rows, hh * GDN_DV:(hh + 1) * GDN_DV] = (_rms_norm(o, ng) * _silu(zz)).astype(BF16)
        return carry

    lax.fori_loop(0, nchunks, chunk_body, 0)
    s_o[...] = s_scr[...]


def _gdn(qkv, gbc, gbr, z, s0, norm_g, chunk, tg, nb):
    bsz, length, _ = qkv.shape
    nl = length // tg
    st_shape = (GDN_HEADS, GDN_DK, GDN_DV)
    gbr = gbr.reshape(bsz, 2 * GDN_HEADS, length // chunk, chunk).transpose(0, 2, 1, 3)
    return pl.pallas_call(
        functools.partial(_gdn_kernel, chunk=chunk),
        grid=(bsz // nb, nl),
        in_specs=[pl.BlockSpec((nb, tg, GDN_CONV_CH), lambda b, l: (b, l, 0)),
                  pl.BlockSpec((nb, tg, 2 * GDN_HEADS), lambda b, l: (b, l, 0)),
                  pl.BlockSpec((nb, tg // chunk, 2 * GDN_HEADS, chunk), lambda b, l: (b, l, 0, 0)),
                  pl.BlockSpec((nb, tg, GDN_HEADS * GDN_DV), lambda b, l: (b, l, 0)),
                  pl.BlockSpec((nb,) + st_shape, lambda b, l: (b, 0, 0, 0)),
                  _const_spec((1, GDN_DV))],
        out_specs=[pl.BlockSpec((nb, tg, GDN_HEADS * GDN_DV), lambda b, l: (b, l, 0)),
                   pl.BlockSpec((nb,) + st_shape, lambda b, l: (b, 0, 0, 0))],
        out_shape=[jax.ShapeDtypeStruct((bsz, length, GDN_HEADS * GDN_DV), BF16),
                   jax.ShapeDtypeStruct((bsz,) + st_shape, F32)],
        scratch_shapes=[pltpu.VMEM((nb,) + st_shape, F32)],
        compiler_params=pltpu.CompilerParams(dimension_semantics=("arbitrary", "arbitrary"),
                                             vmem_limit_bytes=VMEM_LIMIT_BYTES),
        name="gdn",
    )(qkv, gbc, gbr, z, s0.astype(F32), norm_g.reshape(1, -1).astype(F32))


def _flash_kernel(q_ref, kn_ref, krp_ref, vt_ref, o_ref, kcat, *, tq, nh):
    qi = pl.program_id(2)

    @pl.when(qi == 0)
    def _():
        for j in range(nh):
            kcat[j, :, 0:MLA_NOPE] = kn_ref[:, j * MLA_NOPE:(j + 1) * MLA_NOPE]
            kcat[j, :, MLA_NOPE:MLA_QK_PAD] = krp_ref[...]

    qs = [q_ref[:, j * MLA_QK_PAD:(j + 1) * MLA_QK_PAD] for j in range(nh)]

    def tile(ki, carry, masked):
        rows = pl.ds(pl.multiple_of(ki * tq, tq), tq)
        ss = [_dot_nt(kcat[j, rows, :], qs[j]) for j in range(nh)]
        if masked:
            kc = lax.broadcasted_iota(jnp.int32, (tq, tq), 0) // CHUNK
            qc = lax.broadcasted_iota(jnp.int32, (tq, tq), 1) // CHUNK
            ss = [jnp.where(kc <= qc, s, -jnp.inf) for s in ss]
        out = []
        for j in range(nh):
            m, l, acc = carry[j]
            m_new = jnp.maximum(m, jnp.max(ss[j], axis=0, keepdims=True))
            p = jnp.exp2(ss[j] - m_new)
            alpha = jnp.exp2(m - m_new)
            l_new = alpha * l + jnp.sum(p, axis=0, keepdims=True)
            acc = alpha * acc + _dot(vt_ref[j * MLA_V:(j + 1) * MLA_V, rows], p.astype(BF16))
            out.append((m_new, l_new, acc))
        return tuple(out)

    init = tuple((jnp.full((1, tq), -jnp.inf, F32), jnp.zeros((1, tq), F32), jnp.zeros((MLA_V, tq), F32))
                 for _ in range(nh))
    carry = lax.fori_loop(0, qi, lambda ki, c: tile(ki, c, False), init)
    carry = tile(qi, carry, True)
    for j in range(nh):
        _, l, acc = carry[j]
        o_ref[:, j * MLA_V:(j + 1) * MLA_V] = (acc / l).T.astype(BF16)


def _flash_attention(qp, kn, krp, vt, tq, nh):
    bsz, length, _ = qp.shape
    nq = length // tq
    return pl.pallas_call(
        functools.partial(_flash_kernel, tq=tq, nh=nh),
        grid=(bsz, MLA_HEADS // nh, nq),
        in_specs=[pl.BlockSpec((None, tq, nh * MLA_QK_PAD), lambda b, h, i: (b, i, h)),
                  pl.BlockSpec((None, length, nh * MLA_NOPE), lambda b, h, i: (b, 0, h)),
                  pl.BlockSpec((None, length, 128), lambda b, h, i: (b, 0, 0)),
                  pl.BlockSpec((None, nh * MLA_V, length), lambda b, h, i: (b, h, 0))],
        out_specs=pl.BlockSpec((None, tq, nh * MLA_V), lambda b, h, i: (b, i, h)),
        out_shape=jax.ShapeDtypeStruct((bsz, length, MLA_HEADS * MLA_V), BF16),
        scratch_shapes=[pltpu.VMEM((nh, length, MLA_QK_PAD), BF16)],
        compiler_params=pltpu.CompilerParams(dimension_semantics=("arbitrary", "arbitrary", "arbitrary"),
                                             vmem_limit_bytes=VMEM_LIMIT_BYTES),
        name="mla_flash",
    )(qp, kn, krp, vt)


def _kvup_kernel(c_ref, w_ref, kn_o, v_o):
    kv = _dot(c_ref[...].astype(BF16), w_ref[...])
    kn_o[...] = kv[:, 0:MLA_HEADS * MLA_NOPE].astype(BF16)
    v_o[...] = kv[:, MLA_HEADS * MLA_NOPE:].astype(BF16)


def _kvup(ckv, w_kv, tr):
    bsz, length, _ = ckv.shape
    wide = MLA_HEADS * MLA_NOPE
    return pl.pallas_call(
        _kvup_kernel,
        grid=(bsz, length // tr),
        in_specs=[pl.BlockSpec((None, tr, MLA_KV_LORA), lambda b, l: (b, l, 0)), _const_spec(w_kv.shape)],
        out_specs=[pl.BlockSpec((None, tr, wide), lambda b, l: (b, l, 0)),
                   pl.BlockSpec((None, tr, wide), lambda b, l: (b, l, 0))],
        out_shape=[jax.ShapeDtypeStruct((bsz, length, wide), BF16), jax.ShapeDtypeStruct((bsz, length, wide), BF16)],
        compiler_params=pltpu.CompilerParams(dimension_semantics=("arbitrary", "arbitrary"),
                                             vmem_limit_bytes=VMEM_LIMIT_BYTES),
        name="kv_up_cached",
    )(ckv, w_kv)


def _cached_attn_kernel(q_ref, knp_ref, krp_ref, vp_ref, knn_ref, krn_ref, vtn_ref, o_ref, *, past):
    lq = q_ref.shape[0]
    q_chunk = (past + lax.broadcasted_iota(jnp.int32, (lq, past), 0)) // CHUNK
    vis_p = lax.broadcasted_iota(jnp.int32, (lq, past), 1) // CHUNK <= q_chunk
    q_chunk_n = (past + lax.broadcasted_iota(jnp.int32, (lq, lq), 0)) // CHUNK
    vis_n = (past + lax.broadcasted_iota(jnp.int32, (lq, lq), 1)) // CHUNK <= q_chunk_n
    kr_past = krp_ref[...].astype(BF16)
    kr_new = krn_ref[...]
    for hh in range(MLA_HEADS):
        base = hh * MLA_QK_PAD
        qn = q_ref[:, base:base + MLA_NOPE]
        qr = q_ref[:, base + MLA_NOPE:base + MLA_QK_PAD]
        hc = slice(hh * MLA_NOPE, (hh + 1) * MLA_NOPE)
        s_p = _dot_nt(qn, knp_ref[:, hc]) + _dot_nt(qr[:, 0:MLA_ROPE], kr_past)
        s_n = _dot_nt(qn, knn_ref[:, hc]) + _dot_nt(qr, kr_new)
        s_p = jnp.where(vis_p, s_p, -jnp.inf)
        s_n = jnp.where(vis_n, s_n, -jnp.inf)
        m = jnp.maximum(jnp.max(s_p, axis=-1, keepdims=True), jnp.max(s_n, axis=-1, keepdims=True))
        p_p = jnp.exp2(s_p - m)
        p_n = jnp.exp2(s_n - m)
        l = jnp.sum(p_p, axis=-1, keepdims=True) + jnp.sum(p_n, axis=-1, keepdims=True)
        vc = slice(hh * MLA_V, (hh + 1) * MLA_V)
        o = _dot(p_p.astype(BF16), vp_ref[:, vc]) + _dot_nt(p_n.astype(BF16), vtn_ref[vc, :])
        o_ref[:, vc] = (o / l).astype(BF16)


def _cached_attention(qp, kn_past, kr_past, v_past, kn_new, krp_new, vt_new):
    bsz, lq, _ = qp.shape
    past = kn_past.shape[1]
    wide = MLA_HEADS * MLA_NOPE

    def per_b(rows, width):
        return pl.BlockSpec((None, rows, width), lambda b: (b, 0, 0))

    return pl.pallas_call(
        functools.partial(_cached_attn_kernel, past=past),
        grid=(bsz,),
        in_specs=[per_b(lq, MLA_HEADS * MLA_QK_PAD), per_b(past, wide), per_b(past, MLA_ROPE), per_b(past, wide),
                  per_b(lq, wide), per_b(lq, 128), per_b(wide, lq)],
        out_specs=per_b(lq, MLA_HEADS * MLA_V),
        out_shape=jax.ShapeDtypeStruct((bsz, lq, MLA_HEADS * MLA_V), BF16),
        compiler_params=pltpu.CompilerParams(dimension_semantics=("arbitrary",),
                                             vmem_limit_bytes=VMEM_LIMIT_BYTES),
        name="mla_cached",
    )(qp, kn_past, kr_past, v_past, kn_new, krp_new, vt_new)


def _memproj_kernel(m_ref, wk_ref, wv_ref, k_o, v_o):
    mb = m_ref[...].astype(BF16)
    k_o[...] = _dot(mb, wk_ref[...])
    v_o[...] = _dot(mb, wv_ref[...])


def _memproj(mem, w_mk, w_mv):
    bsz, tokens, _ = mem.shape
    wide = MEM_HEADS * MEM_HEAD_DIM
    return pl.pallas_call(
        _memproj_kernel,
        grid=(bsz,),
        in_specs=[pl.BlockSpec((None, tokens, D_MODEL), lambda b: (b, 0, 0)),
                  _const_spec(w_mk.shape), _const_spec(w_mv.shape)],
        out_specs=[pl.BlockSpec((None, tokens, wide), lambda b: (b, 0, 0)),
                   pl.BlockSpec((None, tokens, wide), lambda b: (b, 0, 0))],
        out_shape=[jax.ShapeDtypeStruct((bsz, tokens, wide), F32), jax.ShapeDtypeStruct((bsz, tokens, wide), F32)],
        compiler_params=pltpu.CompilerParams(dimension_semantics=("arbitrary",),
                                             vmem_limit_bytes=VMEM_LIMIT_BYTES),
        name="mem_proj",
    )(mem, w_mk.astype(BF16), w_mv.astype(BF16))


def _route(logits_t, bias_col):
    scores = _sigmoid(logits_t)
    sel = scores + bias_col
    t = logits_t.shape[1]
    sc = [scores[j * N_GROUPS:(j + 1) * N_GROUPS, :] for j in range(GROUP_SIZE)]
    se = [sel[j * N_GROUPS:(j + 1) * N_GROUPS, :] for j in range(GROUP_SIZE)]
    top1 = se[0]
    top2 = jnp.full_like(top1, -jnp.inf)
    for j in range(1, GROUP_SIZE):
        top2 = jnp.maximum(top2, jnp.minimum(top1, se[j]))
        top1 = jnp.maximum(top1, se[j])
    gs = top1 + top2
    gidx = lax.broadcasted_iota(jnp.int32, (N_GROUPS, t), 0)
    beaten = jnp.zeros((N_GROUPS, t), F32)
    for g2 in range(N_GROUPS):
        row = gs[g2:g2 + 1, :]
        tie = jnp.where(gidx > g2, 1.0, 0.0)
        beaten = beaten + jnp.where(row > gs, 1.0, jnp.where(row == gs, tie, 0.0))
    keep = beaten < TOPK_GROUPS
    ms = [jnp.where(keep, se[j], -jnp.inf) for j in range(GROUP_SIZE)]
    cnt = [jnp.zeros((N_GROUPS, t), F32) for _ in range(GROUP_SIZE)]
    for j2 in range(GROUP_SIZE):
        for g2 in range(N_GROUPS):
            row = ms[j2][g2:g2 + 1, :]
            for j in range(GROUP_SIZE):
                if j2 < j:
                    tie = jnp.where(gidx >= g2, 1.0, 0.0)
                else:
                    tie = jnp.where(gidx > g2, 1.0, 0.0)
                cnt[j] = cnt[j] + jnp.where(row > ms[j], 1.0, jnp.where(row == ms[j], tie, 0.0))
    w = [jnp.where(cnt[j] < TOP_K, sc[j], 0.0) for j in range(GROUP_SIZE)]
    tot = w[0]
    for j in range(1, GROUP_SIZE):
        tot = tot + w[j]
    denom = jnp.sum(tot, axis=0, keepdims=True) + 1e-20
    return [w[j] / denom * ROUTED_SCALE for j in range(GROUP_SIZE)]


def _postmix_kernel(x_ref, og_ref, om_ref, gate_ref, mk_ref, mv_ref, lng_ref, lnb_ref, wpg_ref, wpm_ref, wout_ref,
                    ln1g_ref, ln1b_ref, wmq_ref, wmo_ref, ln2g_ref, ln2b_ref, wrt_ref, rb_ref,
                    h2_o, gt_o, *, parts):
    pr = x_ref.shape[0] // parts
    rs = [slice(i * pr, (i + 1) * pr) for i in range(parts)]
    mk = mk_ref[...].astype(BF16)
    mv = mv_ref[...].astype(BF16)
    h = [_layer_norm(x_ref[r, :], lng_ref[...], lnb_ref[...]) for r in rs]
    bg = [_dot(og_ref[r, :], wpg_ref[...]) for r in rs]
    bm = [_dot(om_ref[r, :], wpm_ref[...]) for r in rs]
    merged = [(_sigmoid(gate_ref[r, 0:D_MODEL].astype(F32)) * bg[i]
               + _sigmoid(gate_ref[r, D_MODEL:2 * D_MODEL].astype(F32)) * bm[i]).astype(BF16)
              for i, r in enumerate(rs)]
    mix = [_dot(m, wout_ref[...]) for m in merged]
    h1 = [_layer_norm(DEEPNORM_ALPHA * h[i] + mix[i], ln1g_ref[...], ln1b_ref[...]) for i in range(parts)]
    q = [_dot(v.astype(BF16), wmq_ref[...]).astype(BF16) for v in h1]
    outs = [[] for _ in range(parts)]
    for hh in range(MEM_HEADS):
        hc = slice(hh * MEM_HEAD_DIM, (hh + 1) * MEM_HEAD_DIM)
        s = [_dot_nt(q[i][:, hc], mk[:, hc]) * (MEM_HEAD_DIM ** -0.5) for i in range(parts)]
        p = [jnp.exp(v - jnp.max(v, axis=-1, keepdims=True)) for v in s]
        p = [(v / jnp.sum(v, axis=-1, keepdims=True)).astype(BF16) for v in p]
        for i in range(parts):
            outs[i].append(_dot(p[i], mv[:, hc]).astype(BF16))
    o = [_dot(jnp.concatenate(v, axis=-1), wmo_ref[...]) for v in outs]
    h2 = [_layer_norm(DEEPNORM_ALPHA * h1[i] + o[i], ln2g_ref[...], ln2b_ref[...]) for i in range(parts)]
    logits_t = [_dot_nt(wrt_ref[...], v, precision=lax.Precision.HIGHEST) for v in h2]
    for i, r in enumerate(rs):
        h2_o[r, :] = h2[i]
        gates = _route(logits_t[i], rb_ref[...])
        for j in range(GROUP_SIZE):
            for g in range(N_GROUPS):
                gt_o[g * GROUP_SIZE + j:g * GROUP_SIZE + j + 1, r] = gates[j][g:g + 1, :]


def _postmix(x, og, om, gate, mem_k, mem_v, pw, td):
    bsz, length, _ = x.shape
    nl = length // td
    tokens = mem_k.shape[1]
    wide = MEM_HEADS * MEM_HEAD_DIM
    consts = [pw['ln_in_g'], pw['ln_in_b'], pw['w_proj_gdn'], pw['w_proj_mla'], pw['w_out'], pw['ln1_g'],
              pw['ln1_b'], pw['w_mq'], pw['w_mo'], pw['ln2_g'], pw['ln2_b'], pw['w_router_t'], pw['router_bias']]

    def rows(width):
        return pl.BlockSpec((None, td, width), lambda b, l: (b, l, 0))

    return pl.pallas_call(
        functools.partial(_postmix_kernel, parts=max(1, td // 256)),
        grid=(bsz, nl),
        in_specs=[rows(D_MODEL), rows(D_MODEL), rows(D_MODEL), rows(2 * D_MODEL),
                  pl.BlockSpec((None, tokens, wide), lambda b, l: (b, 0, 0)),
                  pl.BlockSpec((None, tokens, wide), lambda b, l: (b, 0, 0))]
                 + [_const_spec(c.shape) for c in consts],
        out_specs=[rows(D_MODEL), pl.BlockSpec((None, N_EXPERTS, td), lambda b, l: (b, 0, l))],
        out_shape=[jax.ShapeDtypeStruct((bsz, length, D_MODEL), F32),
                   jax.ShapeDtypeStruct((bsz, N_EXPERTS, length), F32)],
        compiler_params=pltpu.CompilerParams(dimension_semantics=("arbitrary", "arbitrary"),
                                             vmem_limit_bytes=VMEM_LIMIT_BYTES),
        name="postmix",
    )(x, og, om, gate, mem_k, mem_v, *consts)


def _moe_kernel(h_ref, g_ref, wgu_ref, wd_ref, wsgu_ref, wsd_ref, ln3g_ref, ln3b_ref, y_o, xb, acc, *, eb):
    e = pl.program_id(1)

    @pl.when(e == 0)
    def _():
        x = h_ref[...].astype(BF16)
        xb[...] = x
        sgu = _dot(x, wsgu_ref[...])
        hid = _silu(sgu[:, 0:SHARED_FF]) * sgu[:, SHARED_FF:]
        acc[...] = _dot(hid.astype(BF16), wsd_ref[...])

    x = xb[...]
    gb = g_ref[...].astype(BF16)
    erow = lax.broadcasted_iota(jnp.int32, (N_EXPERTS, EXPERT_FF), 0)
    for i in range(eb):
        gu = _dot(x, wgu_ref[i])
        pick = jnp.where(erow == e * eb + i, 1.0, 0.0).astype(BF16)
        gcol = _dot(gb, pick)
        hid = _silu(gu[:, 0:EXPERT_FF]) * gu[:, EXPERT_FF:] * gcol
        acc[...] += _dot(hid.astype(BF16), wd_ref[i])

    @pl.when(e == pl.num_programs(1) - 1)
    def _():
        y_o[...] = _layer_norm(DEEPNORM_ALPHA * h_ref[...] + acc[...], ln3g_ref[...], ln3b_ref[...])


def _moe(h2, gates, ew, tm, eb):
    n = h2.shape[0]
    consts = [ew['ws_gu'], ew['ws_down'], ew['ln3_g'], ew['ln3_b']]
    return pl.pallas_call(
        functools.partial(_moe_kernel, eb=eb),
        grid=(n // tm, N_EXPERTS // eb),
        in_specs=[pl.BlockSpec((tm, D_MODEL), lambda t, e: (t, 0)),
                  pl.BlockSpec((tm, N_EXPERTS), lambda t, e: (t, 0)),
                  pl.BlockSpec((eb, D_MODEL, 2 * EXPERT_FF), lambda t, e: (e, 0, 0)),
                  pl.BlockSpec((eb, EXPERT_FF, D_MODEL), lambda t, e: (e, 0, 0))]
                 + [_const_spec(c.shape) for c in consts],
        out_specs=pl.BlockSpec((tm, D_MODEL), lambda t, e: (t, 0)),
        out_shape=jax.ShapeDtypeStruct((n, D_MODEL), F32),
        scratch_shapes=[pltpu.VMEM((tm, D_MODEL), BF16), pltpu.VMEM((tm, D_MODEL), F32)],
        compiler_params=pltpu.CompilerParams(dimension_semantics=("arbitrary", "arbitrary"),
                                             vmem_limit_bytes=VMEM_LIMIT_BYTES),
        name="moe",
    )(h2, gates, ew['w_gu'], ew['w_down'], *consts)


MOE_ROUTED_TILE = 1024
MOE_ROUTED_SUB = 512
MOE_ROUTED_EXPERTS = 8
MOE_ROUTED_SLOTS = 96


def _moe_routed_kernel(h_ref, g_ref, wgu_ref, wd_ref, wsgu_ref, wsd_ref, ln3g_ref, ln3b_ref, y_o,
                       xb, rank_scr, p_scr, o_scr, *, eb, rb, ts):
    e = pl.program_id(2)
    t = h_ref.shape[0]
    ns = t // ts

    @pl.when(e == 0)
    def _():
        x = h_ref[...].astype(BF16)
        xb[...] = x
        sgu = _dot(x, wsgu_ref[...])
        hid = _silu(sgu[:, 0:SHARED_FF]) * sgu[:, SHARED_FF:]
        y_o[...] = _dot(hid.astype(BF16), wsd_ref[...])
        routed = jnp.where(g_ref[...] > 0.0, 1.0, 0.0).astype(BF16)
        upto = jnp.where(lax.broadcasted_iota(jnp.int32, (ts, ts), 0) <= lax.broadcasted_iota(jnp.int32, (ts, ts), 1),
                         1.0, 0.0).astype(BF16)
        for s in range(ns):
            rank_scr[:, s * ts:(s + 1) * ts] = _dot(routed[:, s * ts:(s + 1) * ts], upto)

    e0 = pl.multiple_of(e * eb, eb)
    gblk = g_ref[pl.ds(e0, eb), :]
    rblk = rank_scr[pl.ds(e0, eb), :]
    npass = (jnp.max(rblk).astype(jnp.int32) + (rb - 1)) // rb
    slot = lax.broadcasted_iota(jnp.int32, (rb, ts), 0).astype(F32)

    def one_pass(k, carry):
        base = (k * rb).astype(F32)
        xg, gate = [], []
        for i in range(eb):
            gi = gblk[i:i + 1, :]
            pos = jnp.where(gi > 0.0, rblk[i:i + 1, :] - 1.0 - base, -1.0)
            xgs, gates = [], []
            for s in range(ns):
                cs = slice(s * ts, (s + 1) * ts)
                hit = pos[:, cs] == slot
                pb = jnp.where(hit, 1.0, 0.0).astype(BF16)
                p_scr[s, i * rb:(i + 1) * rb, :] = pb
                gates.append(jnp.sum(jnp.where(hit, gi[:, cs], 0.0), axis=1, keepdims=True))
                xgs.append(_dot(pb, xb[cs, :]).astype(BF16))
            xg.append(jnp.concatenate(xgs, axis=0))
            gate.append(jnp.concatenate(gates, axis=0))
        gu = [_dot(xg[i], wgu_ref[i]) for i in range(eb)]
        hid = [(_silu(gu[i][:, 0:EXPERT_FF]) * gu[i][:, EXPERT_FF:] * gate[i]).astype(BF16) for i in range(eb)]
        for i in range(eb):
            out = _dot(hid[i], wd_ref[i]).astype(BF16)
            for s in range(ns):
                o_scr[s, i * rb:(i + 1) * rb, :] = out[s * rb:(s + 1) * rb, :]
        for s in range(ns):
            y_o[s * ts:(s + 1) * ts, :] += _dot_tn(p_scr[s], o_scr[s])
        return carry

    lax.fori_loop(0, npass, one_pass, 0)

    @pl.when(e == pl.num_programs(2) - 1)
    def _():
        y_o[...] = _layer_norm(DEEPNORM_ALPHA * h_ref[...] + y_o[...], ln3g_ref[...], ln3b_ref[...])


def _moe_routed(h2, gt, ew, tm, ts, eb, rb):
    bsz, length, _ = h2.shape
    consts = [ew['ws_gu'], ew['ws_down'], ew['ln3_g'], ew['ln3_b']]
    return pl.pallas_call(
        functools.partial(_moe_routed_kernel, eb=eb, rb=rb, ts=ts),
        grid=(bsz, length // tm, N_EXPERTS // eb),
        in_specs=[pl.BlockSpec((None, tm, D_MODEL), lambda b, t, e: (b, t, 0), pipeline_mode=pl.Buffered(1)),
                  pl.BlockSpec((None, N_EXPERTS, tm), lambda b, t, e: (b, 0, t)),
                  pl.BlockSpec((eb, D_MODEL, 2 * EXPERT_FF), lambda b, t, e: (e, 0, 0)),
                  pl.BlockSpec((eb, EXPERT_FF, D_MODEL), lambda b, t, e: (e, 0, 0))]
                 + [_const_spec(c.shape) for c in consts],
        out_specs=pl.BlockSpec((None, tm, D_MODEL), lambda b, t, e: (b, t, 0)),
        out_shape=jax.ShapeDtypeStruct((bsz, length, D_MODEL), F32),
        scratch_shapes=[pltpu.VMEM((tm, D_MODEL), BF16), pltpu.VMEM((N_EXPERTS, tm), F32),
                        pltpu.VMEM((tm // ts, eb * rb, ts), BF16), pltpu.VMEM((tm // ts, eb * rb, D_MODEL), BF16)],
        compiler_params=pltpu.CompilerParams(dimension_semantics=("arbitrary", "arbitrary", "arbitrary"),
                                             vmem_limit_bytes=VMEM_LIMIT_LARGE_BYTES),
        name="moe_routed",
    )(h2, gt, ew['w_gu'], ew['w_down'], *consts)


def _prep_post_weights(ln_in_g, ln_in_b, w_proj_gdn, w_proj_mla, w_out, ln1_g, ln1_b, w_mq, w_mo, ln2_g, ln2_b,
                       w_router, router_bias):
    def row(v):
        return v.reshape(1, -1).astype(F32)

    wrt = w_router.T.reshape(N_GROUPS, GROUP_SIZE, D_MODEL).transpose(1, 0, 2).reshape(N_EXPERTS, D_MODEL)
    rb = router_bias.reshape(N_GROUPS, GROUP_SIZE).T.reshape(N_EXPERTS, 1)
    return dict(ln_in_g=row(ln_in_g), ln_in_b=row(ln_in_b), w_proj_gdn=w_proj_gdn.astype(BF16),
                w_proj_mla=w_proj_mla.astype(BF16), w_out=w_out.astype(BF16), ln1_g=row(ln1_g), ln1_b=row(ln1_b),
                w_mq=w_mq.astype(BF16), w_mo=w_mo.astype(BF16), ln2_g=row(ln2_g), ln2_b=row(ln2_b),
                w_router_t=wrt.astype(F32), router_bias=rb.astype(F32))


def _prep_expert_weights(w_gate, w_up, w_down, ws_gate, ws_up, ws_down, ln3_g, ln3_b):
    return dict(w_gu=jnp.concatenate([w_gate, w_up], axis=-1).astype(BF16), w_down=w_down.astype(BF16),
                ws_gu=jnp.concatenate([ws_gate, ws_up], axis=-1).astype(BF16), ws_down=ws_down.astype(BF16),
                ln3_g=ln3_g.reshape(1, -1).astype(F32), ln3_b=ln3_b.reshape(1, -1).astype(F32))


def _tail(x, og, om, gate, mem_k, mem_v, pw, ew, td):
    bsz, length, _ = x.shape
    h2, gt = _postmix(x, og, om, gate, mem_k, mem_v, pw, td)
    if length % MOE_ROUTED_TILE == 0:
        return _moe_routed(h2, gt, ew, MOE_ROUTED_TILE, MOE_ROUTED_SUB, MOE_ROUTED_EXPERTS, MOE_ROUTED_SLOTS)
    gates = gt.transpose(0, 2, 1).reshape(bsz * length, N_EXPERTS)
    y = _moe(h2.reshape(bsz * length, D_MODEL), gates, ew, bsz * length, 2)
    return y.reshape(bsz, length, D_MODEL)


def kernel(x_prompt, x_sample, mem_prompt, state_gdn_conv, state_gdn_ssm, cache_mla_ckv, cache_mla_krope,
           cache_mem_k, cache_mem_v, ln_in_g, ln_in_b, w_in, w_conv, a_log, dt_bias, gdn_norm_g, w_proj_gdn,
           q_norm_g, w_uq, kv_norm_g, w_ukv, w_proj_mla, w_out, ln1_g, ln1_b, w_mq, w_mk, w_mv, w_mo,
           ln2_g, ln2_b, w_router, router_bias, w_gate, w_up, w_down, ws_gate, ws_up, ws_down, ln3_g, ln3_b):
    lyr = 0
    bsz, seq, _ = x_prompt.shape
    dbsz, dseq, _ = x_sample.shape
    past = cache_mla_ckv.shape[2]
    mw = _prep_mixer_weights(w_in[lyr], w_conv[lyr], a_log[lyr], dt_bias[lyr], q_norm_g[lyr], w_uq[lyr],
                             kv_norm_g[lyr], w_ukv[lyr])
    pw = _prep_post_weights(ln_in_g, ln_in_b, w_proj_gdn[lyr], w_proj_mla[lyr], w_out[lyr], ln1_g[lyr], ln1_b[lyr],
                            w_mq[lyr], w_mo[lyr], ln2_g[lyr], ln2_b[lyr], w_router[lyr], router_bias[lyr])
    ew = _prep_expert_weights(w_gate[lyr], w_up[lyr], w_down[lyr], ws_gate[lyr], ws_up[lyr], ws_down[lyr],
                              ln3_g[lyr], ln3_b[lyr])
    wide = MEM_HEADS * MEM_HEAD_DIM

    conv0 = jnp.zeros((bsz, GDN_CONV - 1, GDN_CONV_CH), F32)
    s0 = jnp.zeros((bsz, GDN_HEADS, GDN_DK, GDN_DV), F32)
    (qkv, z, gate, qp, kn, krp, v, ckv_p, kr_p, conv_p, gbc, gbr) = _inproj(
        x_prompt, ln_in_g, ln_in_b, mw, conv0, 0, tl=min(256, seq))
    og, ssm_p = _gdn(qkv, gbc, gbr, z, s0, gdn_norm_g[lyr], chunk=CHUNK, tg=min(256, seq), nb=math.gcd(bsz, 4))
    om = _flash_attention(qp, kn, krp, v, tq=min(1024, seq), nh=2)
    mem_k, mem_v = _memproj(mem_prompt, w_mk[lyr], w_mv[lyr])
    y_p = _tail(x_prompt, og, om, gate, mem_k, mem_v, pw, ew, td=min(1024, seq))

    (qkv, z, gate, qp, kn, krp, v, ckv_s, kr_s, conv_s, gbc, gbr) = _inproj(
        x_sample, ln_in_g, ln_in_b, mw, state_gdn_conv[lyr], past, tl=dseq)
    chunk_s = CHUNK if dseq % CHUNK == 0 else dseq
    og, ssm_s = _gdn(qkv, gbc, gbr, z, state_gdn_ssm[lyr], gdn_norm_g[lyr], chunk=chunk_s, tg=dseq, nb=math.gcd(dbsz, 2))
    kn_past, v_past = _kvup(cache_mla_ckv[lyr], mw['w_kv'], tr=min(1024, past))
    om = _cached_attention(qp, kn_past, cache_mla_krope[lyr], v_past, kn, krp, v)
    y_s = _tail(x_sample, og, om, gate, cache_mem_k[lyr].reshape(dbsz, -1, wide),
                cache_mem_v[lyr].reshape(dbsz, -1, wide), pw, ew, td=dseq)

    return (y_p, y_s, conv_p[None], ssm_p[None], ckv_p[None], kr_p[None],
            mem_k.reshape(1, bsz, -1, MEM_HEADS, MEM_HEAD_DIM), mem_v.reshape(1, bsz, -1, MEM_HEADS, MEM_HEAD_DIM),
            conv_s[None], ssm_s[None], ckv_s[None], kr_s[None])
```

```python
import functools
import math

import jax
import jax.numpy as jnp
import numpy as np
from jax import lax
from jax.experimental import pallas as pl
from jax.experimental.pallas import tpu as pltpu

F32 = jnp.float32
BF16 = jnp.bfloat16

D_MODEL = 1024
CHUNK = 64
GDN_HEADS = 8
GDN_DK = 128
GDN_DV = 128
GDN_CONV = 4
GDN_QK = GDN_HEADS * GDN_DK
GDN_CONV_CH = 2 * GDN_QK + GDN_HEADS * GDN_DV
MLA_HEADS = 8
MLA_Q_LORA = 384
MLA_KV_LORA = 256
MLA_NOPE = 128
MLA_ROPE = 64
MLA_V = 128
MLA_SCALE = (MLA_NOPE + MLA_ROPE) ** -0.5
MLA_SCALE_LOG2 = MLA_SCALE * math.log2(math.e)
MLA_QK_PAD = 256
ROPE_THETA = 10000.0
MEM_HEADS = 4
MEM_HEAD_DIM = 128
N_EXPERTS = 64
TOP_K = 8
N_GROUPS = 8
GROUP_SIZE = N_EXPERTS // N_GROUPS
TOPK_GROUPS = 4
EXPERT_FF = 256
SHARED_FF = 256
ROUTED_SCALE = 2.5
DEPTH = 1
DEEPNORM_ALPHA = (2.0 * DEPTH) ** 0.25
IN_WIDTHS = (GDN_CONV_CH, GDN_HEADS * GDN_DV, GDN_HEADS, GDN_HEADS, MLA_Q_LORA, MLA_KV_LORA, MLA_ROPE,
             D_MODEL, D_MODEL)

VMEM_LIMIT_BYTES = 56 * 1024 * 1024
VMEM_LIMIT_LARGE_BYTES = 60 * 1024 * 1024


def _dot(a, b, precision=None):
    return jnp.dot(a, b, preferred_element_type=F32, precision=precision)


def _dot_nt(a, b, precision=None):
    return lax.dot_general(a, b, (((1,), (1,)), ((), ())), preferred_element_type=F32, precision=precision)


def _dot_tn(a, b):
    return lax.dot_general(a, b, (((0,), (0,)), ((), ())), preferred_element_type=F32)


def _layer_norm(x, g, b, eps=1e-5):
    mu = jnp.mean(x, axis=-1, keepdims=True)
    xc = x - mu
    var = jnp.mean(xc * xc, axis=-1, keepdims=True)
    return xc * lax.rsqrt(var + eps) * g + b


def _rms_norm(x, g, eps=1e-6):
    return x * lax.rsqrt(jnp.mean(x * x, axis=-1, keepdims=True) + eps) * g


def _sigmoid(x):
    return 1.0 / (1.0 + jnp.exp(-x))


def _silu(x):
    return x * _sigmoid(x)


def _softplus(x):
    return jnp.maximum(x, 0.0) + jnp.log1p(jnp.exp(-jnp.abs(x)))


def _const_spec(shape):
    nd = len(shape)
    return pl.BlockSpec(shape, lambda *_: (0,) * nd, pipeline_mode=pl.Buffered(1))


def _inproj_kernel(x_ref, lng_ref, lnb_ref, wqkv_ref, wz_ref, wgate_ref, wcq_ref, wckv_ref, wkr_ref, wab_ref,
                   wabt_ref, convw_ref, conv0_ref, arow_ref, acol_ref, qng_ref, wuq_ref, wuqs_ref, kvng_ref,
                   wuk_ref, wuvt_ref, t1_ref, t2_ref,
                   qkv_o, z_o, gate_o, qp_o, kn_o, krp_o, vt_o, ckv_o, kr_o, conv_o, gbc_o, gbr_o,
                   cbuf):
    tl = x_ref.shape[0]
    li = pl.program_id(1)

    h = _layer_norm(x_ref[...], lng_ref[...], lnb_ref[...])
    hb = h.astype(BF16)

    @pl.when(li == 0)
    def _():
        cbuf[0:8, :] = conv0_ref[...]

    cbuf[8:8 + tl, :] = _dot(hb, wqkv_ref[...])
    conv_o[...] = cbuf[tl + 5:tl + 8, :]
    for part in range(3):
        cols = slice(part * GDN_QK, (part + 1) * GDN_QK)
        y = convw_ref[0:1, cols] * cbuf[5:5 + tl, cols]
        for i in range(1, GDN_CONV):
            y = y + convw_ref[i:i + 1, cols] * cbuf[5 + i:5 + i + tl, cols]
        y = _silu(y)
        if part < 2:
            for hh in range(GDN_HEADS):
                hc = slice(hh * GDN_DK, (hh + 1) * GDN_DK)
                yh = y[:, hc]
                r = lax.rsqrt(jnp.sum(yh * yh, axis=-1, keepdims=True) + 1e-6)
                yh = yh * r
                if part == 0:
                    yh = yh * (GDN_DK ** -0.5)
                qkv_o[:, part * GDN_QK + hh * GDN_DK:part * GDN_QK + (hh + 1) * GDN_DK] = yh.astype(BF16)
        else:
            qkv_o[:, cols] = y.astype(BF16)
    cbuf[0:8, :] = cbuf[tl:tl + 8, :]

    z_o[...] = _dot(hb, wz_ref[...]).astype(BF16)
    gate_o[...] = _dot(hb, wgate_ref[...]).astype(BF16)

    ab = _dot(hb, wab_ref[...])
    arow = arow_ref[...]
    g_c = arow[0:1, :] * _softplus(ab[:, 0:GDN_HEADS] + arow[1:2, :])
    b_c = _sigmoid(ab[:, GDN_HEADS:2 * GDN_HEADS])
    gbc_o[:, 0:GDN_HEADS] = g_c
    gbc_o[:, GDN_HEADS:2 * GDN_HEADS] = b_c
    abt = _dot_nt(wabt_ref[...], hb)
    acol = acol_ref[...]
    gbr_o[0:GDN_HEADS, :] = acol[:, 0:1] * _softplus(abt[0:GDN_HEADS, :] + acol[:, 1:2])
    gbr_o[GDN_HEADS:2 * GDN_HEADS, :] = _sigmoid(abt[GDN_HEADS:2 * GDN_HEADS, :])

    t1 = t1_ref[...]
    t2 = t2_ref[...]

    cq = _rms_norm(_dot(hb, wcq_ref[...]), qng_ref[...]).astype(BF16)
    qm = _dot(cq, wuq_ref[...])
    qs = _dot(cq, wuqs_ref[...])
    for hh in range(MLA_HEADS):
        base = hh * MLA_QK_PAD
        qp_o[:, base:base + MLA_NOPE] = (qm[:, base:base + MLA_NOPE] * MLA_SCALE_LOG2).astype(BF16)
        hi = qm[:, base + MLA_NOPE:base + MLA_QK_PAD] * t1 + qs[:, hh * 128:(hh + 1) * 128] * t2
        qp_o[:, base + MLA_NOPE:base + MLA_QK_PAD] = (hi * MLA_SCALE_LOG2).astype(BF16)

    ckv = _rms_norm(_dot(hb, wckv_ref[...]), kvng_ref[...])
    ckv_o[...] = ckv
    ckvb = ckv.astype(BF16)
    kn_o[...] = _dot(ckvb, wuk_ref[...]).astype(BF16)
    vt_o[...] = _dot_nt(wuvt_ref[...], ckvb).astype(BF16)
    krm = _dot(hb, wkr_ref[...])
    krp = krm[:, 0:128] * t1 + krm[:, 128:256] * t2
    kr_o[...] = krp[:, 0:MLA_ROPE]
    krp_o[...] = krp.astype(BF16)


def _swap_halves(w):
    half = w.shape[-1] // 2
    return jnp.concatenate([w[..., half:], w[..., :half]], axis=-1)


def _rope_tables(pos0, length):
    inv_freq = np.power(ROPE_THETA, -np.arange(0, MLA_ROPE, 2, dtype=np.float64) / MLA_ROPE)
    ang = (pos0 + np.arange(length)).astype(np.float64)[:, None] * inv_freq[None, :]
    c, s = np.cos(ang).astype(np.float32), np.sin(ang).astype(np.float32)
    zeros = np.zeros((length, 128 - MLA_ROPE), np.float32)
    return (jnp.asarray(np.concatenate([c, c, zeros], axis=-1)),
            jnp.asarray(np.concatenate([-s, s, zeros], axis=-1)))


def _prep_mixer_weights(w_in, w_conv, a_log, dt_bias, q_norm_g, w_uq, kv_norm_g, w_ukv):
    splits = []
    off = 0
    for wd in IN_WIDTHS:
        splits.append(w_in[:, off:off + wd])
        off += wd
    w_qkv, w_z, w_a, w_b, w_cq, w_ckv, w_kr, w_gg, w_gm = splits
    z64 = jnp.zeros((D_MODEL, 128 - MLA_ROPE), F32)
    w_krg = jnp.concatenate([w_kr, z64, _swap_halves(w_kr), z64], axis=-1)
    w_ab = jnp.concatenate([w_a, w_b], axis=-1)
    wq = w_uq.reshape(MLA_Q_LORA, MLA_HEADS, MLA_NOPE + MLA_ROPE)
    nope, rope = wq[:, :, :MLA_NOPE], wq[:, :, MLA_NOPE:]
    zq = jnp.zeros((MLA_Q_LORA, MLA_HEADS, MLA_QK_PAD - MLA_NOPE - MLA_ROPE), F32)
    wq_main = jnp.concatenate([nope, rope, zq], axis=-1).reshape(MLA_Q_LORA, MLA_HEADS * MLA_QK_PAD)
    wq_swap = jnp.concatenate([_swap_halves(rope), zq], axis=-1).reshape(MLA_Q_LORA, MLA_HEADS * 128)
    wkv = w_ukv.reshape(MLA_KV_LORA, MLA_HEADS, MLA_NOPE + MLA_V)
    w_kv = jnp.concatenate([wkv[:, :, :MLA_NOPE].reshape(MLA_KV_LORA, -1),
                            wkv[:, :, MLA_NOPE:].reshape(MLA_KV_LORA, -1)], axis=-1)
    neg_a = -jnp.exp(a_log.astype(F32))
    arow = jnp.stack([neg_a, dt_bias.astype(F32)], axis=0)
    return dict(
        w_qkv=w_qkv.astype(BF16), w_z=w_z.astype(BF16),
        w_gate=jnp.concatenate([w_gg, w_gm], axis=-1).astype(BF16),
        w_cq=w_cq.astype(BF16), w_ckv=w_ckv.astype(BF16), w_krg=w_krg.astype(BF16),
        w_ab=w_ab.astype(BF16), w_abt=w_ab.T.astype(BF16),
        w_conv=w_conv.astype(F32), arow=arow, acol=arow.T,
        q_norm_g=q_norm_g.reshape(1, -1).astype(F32), wq_main=wq_main.astype(BF16), wq_swap=wq_swap.astype(BF16),
        kv_norm_g=kv_norm_g.reshape(1, -1).astype(F32), w_kv=w_kv.astype(BF16),
        w_k=wkv[:, :, :MLA_NOPE].reshape(MLA_KV_LORA, -1).astype(BF16),
        w_vt=wkv[:, :, MLA_NOPE:].reshape(MLA_KV_LORA, -1).T.astype(BF16))


def _inproj(x, ln_g, ln_b, mw, conv_state, pos0, tl):
    bsz, length, _ = x.shape
    nl = length // tl
    t1, t2 = _rope_tables(pos0, length)
    conv0 = jnp.concatenate([jnp.zeros((bsz, 8 - (GDN_CONV - 1), GDN_CONV_CH), F32), conv_state.astype(F32)], axis=1)

    def rows(width, dtype):
        return (jax.ShapeDtypeStruct((bsz, length, width), dtype),
                pl.BlockSpec((None, tl, width), lambda b, l: (b, l, 0)))

    outs = [rows(GDN_CONV_CH, BF16), rows(GDN_HEADS * GDN_DV, BF16), rows(2 * D_MODEL, BF16),
            rows(MLA_HEADS * MLA_QK_PAD, BF16), rows(MLA_HEADS * MLA_NOPE, BF16), rows(128, BF16),
            (jax.ShapeDtypeStruct((bsz, MLA_HEADS * MLA_V, length), BF16),
             pl.BlockSpec((None, MLA_HEADS * MLA_V, tl), lambda b, l: (b, 0, l))),
            rows(MLA_KV_LORA, F32), rows(MLA_ROPE, F32),
            (jax.ShapeDtypeStruct((bsz, GDN_CONV - 1, GDN_CONV_CH), F32),
             pl.BlockSpec((None, GDN_CONV - 1, GDN_CONV_CH), lambda b, l: (b, 0, 0))),
            rows(2 * GDN_HEADS, F32),
            (jax.ShapeDtypeStruct((bsz, 2 * GDN_HEADS, length), F32),
             pl.BlockSpec((None, 2 * GDN_HEADS, tl), lambda b, l: (b, 0, l)))]
    consts = [ln_g.reshape(1, -1), ln_b.reshape(1, -1), mw['w_qkv'], mw['w_z'], mw['w_gate'], mw['w_cq'],
              mw['w_ckv'], mw['w_krg'], mw['w_ab'], mw['w_abt'], mw['w_conv']]
    consts2 = [mw['arow'], mw['acol'], mw['q_norm_g'], mw['wq_main'], mw['wq_swap'], mw['kv_norm_g'], mw['w_k'],
               mw['w_vt']]
    in_specs = ([pl.BlockSpec((None, tl, D_MODEL), lambda b, l: (b, l, 0))]
                + [_const_spec(c.shape) for c in consts]
                + [pl.BlockSpec((None, 8, GDN_CONV_CH), lambda b, l: (b, 0, 0))]
                + [_const_spec(c.shape) for c in consts2]
                + [pl.BlockSpec((tl, 128), lambda b, l: (l, 0)), pl.BlockSpec((tl, 128), lambda b, l: (l, 0))])
    return pl.pallas_call(
        _inproj_kernel,
        grid=(bsz, nl),
        in_specs=in_specs,
        out_specs=[o[1] for o in outs],
        out_shape=[o[0] for o in outs],
        scratch_shapes=[pltpu.VMEM((tl + 8, GDN_CONV_CH), F32)],
        compiler_params=pltpu.CompilerParams(dimension_semantics=("arbitrary", "arbitrary"),
                                             vmem_limit_bytes=VMEM_LIMIT_BYTES),
        name="inproj",
    )(x, *consts, conv0, *consts2, t1, t2)


def _gdn_kernel(qkv_ref, gbc_ref, gbr_ref, z_ref, s0_ref, ng_ref, og_o, s_o, s_scr, *, chunk):
    tg = qkv_ref.shape[1]
    nchunks = tg // chunk
    li = pl.program_id(1)

    @pl.when(li == 0)
    def _():
        s_scr[...] = s0_ref[...]

    ri = lax.broadcasted_iota(jnp.int32, (chunk, chunk), 0)
    ci = lax.broadcasted_iota(jnp.int32, (chunk, chunk), 1)
    incl = ri >= ci
    strict = ri > ci
    tri = jnp.where(incl, 1.0, 0.0).astype(F32)
    eye = jnp.where(ri == ci, 1.0, 0.0).astype(F32)
    ng = ng_ref[...]
    n_sq = int(math.log2(chunk)) - 1

    nb = qkv_ref.shape[0]
    chains = [(g, hh) for g in range(nb) for hh in range(GDN_HEADS)]

    def chunk_body(c, carry):
        r0 = pl.multiple_of(c * chunk, chunk)
        rows = pl.ds(r0, chunk)
        gbc = [gbc_ref[g, rows, :] for g in range(nb)]
        gc_col = [_dot(tri, gbc[g], precision=lax.Precision.HIGHEST) for g in range(nb)]
        gc_row = [_dot_nt(gbr_ref[g, c], tri, precision=lax.Precision.HIGHEST) for g in range(nb)]
        q, k, qkk = {}, {}, {}
        for ch in chains:
            g, hh = ch
            q[ch] = qkv_ref[g, rows, hh * GDN_DK:(hh + 1) * GDN_DK]
            k[ch] = qkv_ref[g, rows, GDN_QK + hh * GDN_DK:GDN_QK + (hh + 1) * GDN_DK]
            qkk[ch] = _dot_nt(jnp.concatenate([q[ch], k[ch]], axis=0), k[ch])
        gcc, beta, gamma, g_end, qk, xp, t_inv = {}, {}, {}, {}, {}, {}, {}
        for ch in chains:
            g, hh = ch
            gcc[ch] = gc_col[g][:, hh:hh + 1]
            gcr = gc_row[g][hh:hh + 1, :]
            beta[ch] = gbc[g][:, GDN_HEADS + hh:GDN_HEADS + hh + 1]
            g_end[ch] = gc_col[g][chunk - 1:chunk, hh:hh + 1]
            gamma[ch] = jnp.exp(gcc[ch])
            decay = jnp.where(incl, jnp.exp(jnp.where(incl, gcc[ch] - gcr, 0.0)), 0.0)
            qk[ch] = (decay * qkk[ch][0:chunk, :]).astype(BF16)
            xp[ch] = jnp.where(strict, beta[ch] * decay * qkk[ch][chunk:2 * chunk, :], 0.0)
            t_inv[ch] = eye - xp[ch]
        for _ in range(n_sq):
            for ch in chains:
                xb = xp[ch].astype(BF16)
                xp[ch] = _dot(xb, xb)
            for ch in chains:
                t_inv[ch] = t_inv[ch] + _dot(t_inv[ch].astype(BF16), xp[ch].astype(BF16))
        sol, q_s, sb, s_old, k_dec = {}, {}, {}, {}, {}
        for ch in chains:
            g, hh = ch
            kf = k[ch].astype(F32)
            v = qkv_ref[g, rows, 2 * GDN_QK + hh * GDN_DV:2 * GDN_QK + (hh + 1) * GDN_DV].astype(F32)
            rhs = jnp.concatenate([beta[ch] * v, (beta[ch] * gamma[ch]) * kf], axis=-1).astype(BF16)
            sol[ch] = _dot(t_inv[ch].astype(BF16), rhs)
            k_dec[ch] = (jnp.exp(g_end[ch] - gcc[ch]) * kf).astype(BF16)
            s_old[ch] = s_scr[g, hh]
            sb[ch] = s_old[ch].astype(BF16)
            q_s[ch] = _dot((gamma[ch] * q[ch].astype(F32)).astype(BF16), sb[ch])
        ub = {}
        for ch in chains:
            u = sol[ch][:, :GDN_DV] - _dot(sol[ch][:, GDN_DV:].astype(BF16), sb[ch])
            ub[ch] = u.astype(BF16)
        for ch in chains:
            g, hh = ch
            o = q_s[ch] + _dot(qk[ch], ub[ch])
            s_scr[g, hh] = jnp.exp(g_end[ch]) * s_old[ch] + _dot_tn(k_dec[ch], ub[ch])
            zz = z_ref[g, rows, hh * GDN_DV:(hh + 1) * GDN_DV].astype(F32)
            og_o[g, rows, hh * GDN_DV:(hh + 1) * GDN_DV] = (_rms_norm(o, ng) * _silu(zz)).astype(BF16)
        return carry

    lax.fori_loop(0, nchunks, chunk_body, 0)
    s_o[...] = s_scr[...]


def _gdn(qkv, gbc, gbr, z, s0, norm_g, chunk, tg, nb):
    bsz, length, _ = qkv.shape
    nl = length // tg
    st_shape = (GDN_HEADS, GDN_DK, GDN_DV)
    gbr = gbr.reshape(bsz, 2 * GDN_HEADS, length // chunk, chunk).transpose(0, 2, 1, 3)
    return pl.pallas_call(
        functools.partial(_gdn_kernel, chunk=chunk),
        grid=(bsz // nb, nl),
        in_specs=[pl.BlockSpec((nb, tg, GDN_CONV_CH), lambda b, l: (b, l, 0)),
                  pl.BlockSpec((nb, tg, 2 * GDN_HEADS), lambda b, l: (b, l, 0)),
                  pl.BlockSpec((nb, tg // chunk, 2 * GDN_HEADS, chunk), lambda b, l: (b, l, 0, 0)),
                  pl.BlockSpec((nb, tg, GDN_HEADS * GDN_DV), lambda b, l: (b, l, 0)),
                  pl.BlockSpec((nb,) + st_shape, lambda b, l: (b, 0, 0, 0)),
                  _const_spec((1, GDN_DV))],
        out_specs=[pl.BlockSpec((nb, tg, GDN_HEADS * GDN_DV), lambda b, l: (b, l, 0)),
                   pl.BlockSpec((nb,) + st_shape, lambda b, l: (b, 0, 0, 0))],
        out_shape=[jax.ShapeDtypeStruct((bsz, length, GDN_HEADS * GDN_DV), BF16),
                   jax.ShapeDtypeStruct((bsz,) + st_shape, F32)],
        scratch_shapes=[pltpu.VMEM((nb,) + st_shape, F32)],
        compiler_params=pltpu.CompilerParams(dimension_semantics=("arbitrary", "arbitrary"),
                                             vmem_limit_bytes=VMEM_LIMIT_BYTES),
        name="gdn",
    )(qkv, gbc, gbr, z, s0.astype(F32), norm_g.reshape(1, -1).astype(F32))


def _flash_kernel(q_ref, kn_ref, krp_ref, vt_ref, o_ref, kcat, *, tq, nh):
    qi = pl.program_id(2)

    @pl.when(qi == 0)
    def _():
        for j in range(nh):
            kcat[j, :, 0:MLA_NOPE] = kn_ref[:, j * MLA_NOPE:(j + 1) * MLA_NOPE]
            kcat[j, :, MLA_NOPE:MLA_QK_PAD] = krp_ref[...]

    qs = [q_ref[:, j * MLA_QK_PAD:(j + 1) * MLA_QK_PAD] for j in range(nh)]

    def tile(ki, carry, masked):
        rows = pl.ds(pl.multiple_of(ki * tq, tq), tq)
        ss = [_dot_nt(kcat[j, rows, :], qs[j]) for j in range(nh)]
        if masked:
            kc = lax.broadcasted_iota(jnp.int32, (tq, tq), 0) // CHUNK
            qc = lax.broadcasted_iota(jnp.int32, (tq, tq), 1) // CHUNK
            ss = [jnp.where(kc <= qc, s, -jnp.inf) for s in ss]
        out = []
        for j in range(nh):
            m, l, acc = carry[j]
            m_new = jnp.maximum(m, jnp.max(ss[j], axis=0, keepdims=True))
            p = jnp.exp2(ss[j] - m_new)
            alpha = jnp.exp2(m - m_new)
            l_new = alpha * l + jnp.sum(p, axis=0, keepdims=True)
            acc = alpha * acc + _dot(vt_ref[j * MLA_V:(j + 1) * MLA_V, rows], p.astype(BF16))
            out.append((m_new, l_new, acc))
        return tuple(out)

    init = tuple((jnp.full((1, tq), -jnp.inf, F32), jnp.zeros((1, tq), F32), jnp.zeros((MLA_V, tq), F32))
                 for _ in range(nh))
    carry = lax.fori_loop(0, qi, lambda ki, c: tile(ki, c, False), init)
    carry = tile(qi, carry, True)
    for j in range(nh):
        _, l, acc = carry[j]
        o_ref[:, j * MLA_V:(j + 1) * MLA_V] = (acc / l).T.astype(BF16)


def _flash_attention(qp, kn, krp, vt, tq, nh):
    bsz, length, _ = qp.shape
    nq = length // tq
    return pl.pallas_call(
        functools.partial(_flash_kernel, tq=tq, nh=nh),
        grid=(bsz, MLA_HEADS // nh, nq),
        in_specs=[pl.BlockSpec((None, tq, nh * MLA_QK_PAD), lambda b, h, i: (b, i, h)),
                  pl.BlockSpec((None, length, nh * MLA_NOPE), lambda b, h, i: (b, 0, h)),
                  pl.BlockSpec((None, length, 128), lambda b, h, i: (b, 0, 0)),
                  pl.BlockSpec((None, nh * MLA_V, length), lambda b, h, i: (b, h, 0))],
        out_specs=pl.BlockSpec((None, tq, nh * MLA_V), lambda b, h, i: (b, i, h)),
        out_shape=jax.ShapeDtypeStruct((bsz, length, MLA_HEADS * MLA_V), BF16),
        scratch_shapes=[pltpu.VMEM((nh, length, MLA_QK_PAD), BF16)],
        compiler_params=pltpu.CompilerParams(dimension_semantics=("arbitrary", "arbitrary", "arbitrary"),
                                             vmem_limit_bytes=VMEM_LIMIT_BYTES),
        name="mla_flash",
    )(qp, kn, krp, vt)


def _cached_attn_kernel(q_ref, cp_ref, krp_ref, knn_ref, krn_ref, vtn_ref, wkv_ref, o_ref, *, past):
    lq = q_ref.shape[0]
    hq = MLA_HEADS * lq
    wide = MLA_HEADS * MLA_NOPE
    q_chunk_p = (past + lax.rem(lax.broadcasted_iota(jnp.int32, (hq, past), 0), lq)) // CHUNK
    vis_p = lax.broadcasted_iota(jnp.int32, (hq, past), 1) // CHUNK <= q_chunk_p
    q_chunk_n = (past + lax.rem(lax.broadcasted_iota(jnp.int32, (hq, lq), 0), lq)) // CHUNK
    vis_n = (past + lax.broadcasted_iota(jnp.int32, (hq, lq), 1)) // CHUNK <= q_chunk_n
    ckv = cp_ref[...].astype(BF16)
    kr_past = krp_ref[...].astype(BF16)
    kr_new = krn_ref[...]
    q_lat, q_rope, s_new = [], [], []
    for hh in range(MLA_HEADS):
        base = hh * MLA_QK_PAD
        qn = q_ref[:, base:base + MLA_NOPE]
        qr = q_ref[:, base + MLA_NOPE:base + MLA_QK_PAD]
        hc = slice(hh * MLA_NOPE, (hh + 1) * MLA_NOPE)
        q_lat.append(_dot_nt(qn, wkv_ref[:, hc]).astype(BF16))
        q_rope.append(qr[:, 0:MLA_ROPE])
        s_new.append(_dot_nt(qn, knn_ref[:, hc]) + _dot_nt(qr, kr_new))
    s_p = _dot_nt(jnp.concatenate(q_lat, axis=0), ckv) + _dot_nt(jnp.concatenate(q_rope, axis=0), kr_past)
    s_p = jnp.where(vis_p, s_p, -jnp.inf)
    s_n = jnp.where(vis_n, jnp.concatenate(s_new, axis=0), -jnp.inf)
    m = jnp.maximum(jnp.max(s_p, axis=-1, keepdims=True), jnp.max(s_n, axis=-1, keepdims=True))
    p_p = jnp.exp2(s_p - m)
    p_n = jnp.exp2(s_n - m)
    l = jnp.sum(p_p, axis=-1, keepdims=True) + jnp.sum(p_n, axis=-1, keepdims=True)
    p_n = p_n.astype(BF16)
    o_lat = _dot(p_p.astype(BF16), ckv).astype(BF16)
    for hh in range(MLA_HEADS):
        rows = slice(hh * lq, (hh + 1) * lq)
        vc = slice(hh * MLA_V, (hh + 1) * MLA_V)
        o = _dot(o_lat[rows, :], wkv_ref[:, wide + hh * MLA_V:wide + (hh + 1) * MLA_V])
        o = o + _dot_nt(p_n[rows, :], vtn_ref[vc, :])
        o_ref[:, vc] = (o / l[rows, :]).astype(BF16)


def _cached_attention(qp, ckv_past, kr_past, kn_new, krp_new, vt_new, w_kv):
    bsz, lq, _ = qp.shape
    past = ckv_past.shape[1]
    wide = MLA_HEADS * MLA_NOPE

    def per_b(rows, width):
        return pl.BlockSpec((None, rows, width), lambda b: (b, 0, 0))

    return pl.pallas_call(
        functools.partial(_cached_attn_kernel, past=past),
        grid=(bsz,),
        in_specs=[per_b(lq, MLA_HEADS * MLA_QK_PAD), per_b(past, MLA_KV_LORA), per_b(past, MLA_ROPE),
                  per_b(lq, wide), per_b(lq, 128), per_b(wide, lq), _const_spec(w_kv.shape)],
        out_specs=per_b(lq, MLA_HEADS * MLA_V),
        out_shape=jax.ShapeDtypeStruct((bsz, lq, MLA_HEADS * MLA_V), BF16),
        compiler_params=pltpu.CompilerParams(dimension_semantics=("arbitrary",),
                                             vmem_limit_bytes=VMEM_LIMIT_BYTES),
        name="mla_cached",
    )(qp, ckv_past, kr_past, kn_new, krp_new, vt_new, w_kv)


def _memproj_kernel(m_ref, wk_ref, wv_ref, k_o, v_o):
    mb = m_ref[...].astype(BF16)
    k_o[...] = _dot(mb, wk_ref[...])
    v_o[...] = _dot(mb, wv_ref[...])


def _memproj(mem, w_mk, w_mv):
    bsz, tokens, _ = mem.shape
    wide = MEM_HEADS * MEM_HEAD_DIM
    return pl.pallas_call(
        _memproj_kernel,
        grid=(bsz,),
        in_specs=[pl.BlockSpec((None, tokens, D_MODEL), lambda b: (b, 0, 0)),
                  _const_spec(w_mk.shape), _const_spec(w_mv.shape)],
        out_specs=[pl.BlockSpec((None, tokens, wide), lambda b: (b, 0, 0)),
                   pl.BlockSpec((None, tokens, wide), lambda b: (b, 0, 0))],
        out_shape=[jax.ShapeDtypeStruct((bsz, tokens, wide), F32), jax.ShapeDtypeStruct((bsz, tokens, wide), F32)],
        compiler_params=pltpu.CompilerParams(dimension_semantics=("arbitrary",),
                                             vmem_limit_bytes=VMEM_LIMIT_BYTES),
        name="mem_proj",
    )(mem, w_mk.astype(BF16), w_mv.astype(BF16))


def _route(logits_t, bias_col):
    scores = _sigmoid(logits_t)
    sel = scores + bias_col
    t = logits_t.shape[1]
    sc = [scores[j * N_GROUPS:(j + 1) * N_GROUPS, :] for j in range(GROUP_SIZE)]
    se = [sel[j * N_GROUPS:(j + 1) * N_GROUPS, :] for j in range(GROUP_SIZE)]
    top1 = se[0]
    top2 = jnp.full_like(top1, -jnp.inf)
    for j in range(1, GROUP_SIZE):
        top2 = jnp.maximum(top2, jnp.minimum(top1, se[j]))
        top1 = jnp.maximum(top1, se[j])
    gs = top1 + top2
    gidx = lax.broadcasted_iota(jnp.int32, (N_GROUPS, t), 0)
    beaten = jnp.zeros((N_GROUPS, t), F32)
    for g2 in range(N_GROUPS):
        row = gs[g2:g2 + 1, :]
        tie = jnp.where(gidx > g2, 1.0, 0.0)
        beaten = beaten + jnp.where(row > gs, 1.0, jnp.where(row == gs, tie, 0.0))
    keep = beaten < TOPK_GROUPS
    ms = [jnp.where(keep, se[j], -jnp.inf) for j in range(GROUP_SIZE)]
    cnt = [jnp.zeros((N_GROUPS, t), F32) for _ in range(GROUP_SIZE)]
    for j2 in range(GROUP_SIZE):
        for g2 in range(N_GROUPS):
            row = ms[j2][g2:g2 + 1, :]
            for j in range(GROUP_SIZE):
                if j2 < j:
                    tie = jnp.where(gidx >= g2, 1.0, 0.0)
                else:
                    tie = jnp.where(gidx > g2, 1.0, 0.0)
                cnt[j] = cnt[j] + jnp.where(row > ms[j], 1.0, jnp.where(row == ms[j], tie, 0.0))
    w = [jnp.where(cnt[j] < TOP_K, sc[j], 0.0) for j in range(GROUP_SIZE)]
    tot = w[0]
    for j in range(1, GROUP_SIZE):
        tot = tot + w[j]
    denom = jnp.sum(tot, axis=0, keepdims=True) + 1e-20
    return [w[j] / denom * ROUTED_SCALE for j in range(GROUP_SIZE)]


def _postmix_kernel(x_ref, og_ref, om_ref, gate_ref, mk_ref, mv_ref, lng_ref, lnb_ref, wpg_ref, wpm_ref, wout_ref,
                    ln1g_ref, ln1b_ref, wmq_ref, wmo_ref, ln2g_ref, ln2b_ref, wrt_ref, rb_ref,
                    h2_o, gt_o, *, parts):
    pr = x_ref.shape[0] // parts
    rs = [slice(i * pr, (i + 1) * pr) for i in range(parts)]
    mk = mk_ref[...].astype(BF16)
    mv = mv_ref[...].astype(BF16)
    h = [_layer_norm(x_ref[r, :], lng_ref[...], lnb_ref[...]) for r in rs]
    bg = [_dot(og_ref[r, :], wpg_ref[...]) for r in rs]
    bm = [_dot(om_ref[r, :], wpm_ref[...]) for r in rs]
    merged = [(_sigmoid(gate_ref[r, 0:D_MODEL].astype(F32)) * bg[i]
               + _sigmoid(gate_ref[r, D_MODEL:2 * D_MODEL].astype(F32)) * bm[i]).astype(BF16)
              for i, r in enumerate(rs)]
    mix = [_dot(m, wout_ref[...]) for m in merged]
    h1 = [_layer_norm(DEEPNORM_ALPHA * h[i] + mix[i], ln1g_ref[...], ln1b_ref[...]) for i in range(parts)]
    q = [_dot(v.astype(BF16), wmq_ref[...]).astype(BF16) for v in h1]
    outs = [[] for _ in range(parts)]
    for hh in range(MEM_HEADS):
        hc = slice(hh * MEM_HEAD_DIM, (hh + 1) * MEM_HEAD_DIM)
        s = [_dot_nt(q[i][:, hc], mk[:, hc]) * (MEM_HEAD_DIM ** -0.5) for i in range(parts)]
        p = [jnp.exp(v - jnp.max(v, axis=-1, keepdims=True)) for v in s]
        p = [(v / jnp.sum(v, axis=-1, keepdims=True)).astype(BF16) for v in p]
        for i in range(parts):
            outs[i].append(_dot(p[i], mv[:, hc]).astype(BF16))
    o = [_dot(jnp.concatenate(v, axis=-1), wmo_ref[...]) for v in outs]
    h2 = [_layer_norm(DEEPNORM_ALPHA * h1[i] + o[i], ln2g_ref[...], ln2b_ref[...]) for i in range(parts)]
    logits_t = [_dot_nt(wrt_ref[...], v, precision=lax.Precision.HIGHEST) for v in h2]
    for i, r in enumerate(rs):
        h2_o[r, :] = h2[i]
        gates = _route(logits_t[i], rb_ref[...])
        for j in range(GROUP_SIZE):
            for g in range(N_GROUPS):
                gt_o[g * GROUP_SIZE + j:g * GROUP_SIZE + j + 1, r] = gates[j][g:g + 1, :]


def _postmix(x, og, om, gate, mem_k, mem_v, pw, td):
    bsz, length, _ = x.shape
    nl = length // td
    tokens = mem_k.shape[1]
    wide = MEM_HEADS * MEM_HEAD_DIM
    consts = [pw['ln_in_g'], pw['ln_in_b'], pw['w_proj_gdn'], pw['w_proj_mla'], pw['w_out'], pw['ln1_g'],
              pw['ln1_b'], pw['w_mq'], pw['w_mo'], pw['ln2_g'], pw['ln2_b'], pw['w_router_t'], pw['router_bias']]

    def rows(width):
        return pl.BlockSpec((None, td, width), lambda b, l: (b, l, 0))

    return pl.pallas_call(
        functools.partial(_postmix_kernel, parts=max(1, td // 256)),
        grid=(bsz, nl),
        in_specs=[rows(D_MODEL), rows(D_MODEL), rows(D_MODEL), rows(2 * D_MODEL),
                  pl.BlockSpec((None, tokens, wide), lambda b, l: (b, 0, 0)),
                  pl.BlockSpec((None, tokens, wide), lambda b, l: (b, 0, 0))]
                 + [_const_spec(c.shape) for c in consts],
        out_specs=[rows(D_MODEL), pl.BlockSpec((None, N_EXPERTS, td), lambda b, l: (b, 0, l))],
        out_shape=[jax.ShapeDtypeStruct((bsz, length, D_MODEL), F32),
                   jax.ShapeDtypeStruct((bsz, N_EXPERTS, length), F32)],
        compiler_params=pltpu.CompilerParams(dimension_semantics=("arbitrary", "arbitrary"),
                                             vmem_limit_bytes=VMEM_LIMIT_BYTES),
        name="postmix",
    )(x, og, om, gate, mem_k, mem_v, *consts)


def _moe_kernel(h_ref, g_ref, wgu_ref, wd_ref, wsgu_ref, wsd_ref, ln3g_ref, ln3b_ref, y_o, xb, acc, *, eb):
    e = pl.program_id(1)

    @pl.when(e == 0)
    def _():
        x = h_ref[...].astype(BF16)
        xb[...] = x
        sgu = _dot(x, wsgu_ref[...])
        hid = _silu(sgu[:, 0:SHARED_FF]) * sgu[:, SHARED_FF:]
        acc[...] = _dot(hid.astype(BF16), wsd_ref[...])

    x = xb[...]
    gb = g_ref[...].astype(BF16)
    erow = lax.broadcasted_iota(jnp.int32, (N_EXPERTS, EXPERT_FF), 0)
    for i in range(eb):
        gu = _dot(x, wgu_ref[i])
        pick = jnp.where(erow == e * eb + i, 1.0, 0.0).astype(BF16)
        gcol = _dot(gb, pick)
        hid = _silu(gu[:, 0:EXPERT_FF]) * gu[:, EXPERT_FF:] * gcol
        acc[...] += _dot(hid.astype(BF16), wd_ref[i])

    @pl.when(e == pl.num_programs(1) - 1)
    def _():
        y_o[...] = _layer_norm(DEEPNORM_ALPHA * h_ref[...] + acc[...], ln3g_ref[...], ln3b_ref[...])


def _moe(h2, gates, ew, tm, eb):
    n = h2.shape[0]
    consts = [ew['ws_gu'], ew['ws_down'], ew['ln3_g'], ew['ln3_b']]
    return pl.pallas_call(
        functools.partial(_moe_kernel, eb=eb),
        grid=(n // tm, N_EXPERTS // eb),
        in_specs=[pl.BlockSpec((tm, D_MODEL), lambda t, e: (t, 0)),
                  pl.BlockSpec((tm, N_EXPERTS), lambda t, e: (t, 0)),
                  pl.BlockSpec((eb, D_MODEL, 2 * EXPERT_FF), lambda t, e: (e, 0, 0)),
                  pl.BlockSpec((eb, EXPERT_FF, D_MODEL), lambda t, e: (e, 0, 0))]
                 + [_const_spec(c.shape) for c in consts],
        out_specs=pl.BlockSpec((tm, D_MODEL), lambda t, e: (t, 0)),
        out_shape=jax.ShapeDtypeStruct((n, D_MODEL), F32),
        scratch_shapes=[pltpu.VMEM((tm, D_MODEL), BF16), pltpu.VMEM((tm, D_MODEL), F32)],
        compiler_params=pltpu.CompilerParams(dimension_semantics=("arbitrary", "arbitrary"),
                                             vmem_limit_bytes=VMEM_LIMIT_BYTES),
        name="moe",
    )(h2, gates, ew['w_gu'], ew['w_down'], *consts)


MOE_ROUTED_TILE = 1024
MOE_ROUTED_SUB = 512
MOE_ROUTED_EXPERTS = 8
MOE_ROUTED_SLOTS = 96


def _moe_routed_kernel(h_ref, g_ref, wgu_ref, wd_ref, wsgu_ref, wsd_ref, ln3g_ref, ln3b_ref, y_o,
                       xb, rank_scr, p_scr, o_scr, *, eb, rb, ts):
    e = pl.program_id(2)
    t = h_ref.shape[0]
    ns = t // ts

    @pl.when(e == 0)
    def _():
        x = h_ref[...].astype(BF16)
        xb[...] = x
        sgu = _dot(x, wsgu_ref[...])
        hid = _silu(sgu[:, 0:SHARED_FF]) * sgu[:, SHARED_FF:]
        y_o[...] = _dot(hid.astype(BF16), wsd_ref[...])
        routed = jnp.where(g_ref[...] > 0.0, 1.0, 0.0).astype(BF16)
        upto = jnp.where(lax.broadcasted_iota(jnp.int32, (ts, ts), 0) <= lax.broadcasted_iota(jnp.int32, (ts, ts), 1),
                         1.0, 0.0).astype(BF16)
        for s in range(ns):
            rank_scr[:, s * ts:(s + 1) * ts] = _dot(routed[:, s * ts:(s + 1) * ts], upto)

    e0 = pl.multiple_of(e * eb, eb)
    gblk = g_ref[pl.ds(e0, eb), :]
    rblk = rank_scr[pl.ds(e0, eb), :]
    npass = (jnp.max(rblk).astype(jnp.int32) + (rb - 1)) // rb
    slot = lax.broadcasted_iota(jnp.int32, (rb, ts), 0).astype(F32)

    def one_pass(k, carry):
        base = (k * rb).astype(F32)
        xg, gate = [], []
        for i in range(eb):
            gi = gblk[i:i + 1, :]
            pos = jnp.where(gi > 0.0, rblk[i:i + 1, :] - 1.0 - base, -1.0)
            xgs, gates = [], []
            for s in range(ns):
                cs = slice(s * ts, (s + 1) * ts)
                hit = pos[:, cs] == slot
                pb = jnp.where(hit, 1.0, 0.0).astype(BF16)
                p_scr[s, i * rb:(i + 1) * rb, :] = pb
                gates.append(jnp.sum(jnp.where(hit, gi[:, cs], 0.0), axis=1, keepdims=True))
                xgs.append(_dot(pb, xb[cs, :]).astype(BF16))
            xg.append(jnp.concatenate(xgs, axis=0))
            gate.append(jnp.concatenate(gates, axis=0))
        gu = [_dot(xg[i], wgu_ref[i]) for i in range(eb)]
        hid = [(_silu(gu[i][:, 0:EXPERT_FF]) * gu[i][:, EXPERT_FF:] * gate[i]).astype(BF16) for i in range(eb)]
        for i in range(eb):
            out = _dot(hid[i], wd_ref[i]).astype(BF16)
            for s in range(ns):
                o_scr[s, i * rb:(i + 1) * rb, :] = out[s * rb:(s + 1) * rb, :]
        for s in range(ns):
            y_o[s * ts:(s + 1) * ts, :] += _dot_tn(p_scr[s], o_scr[s])
        return carry

    lax.fori_loop(0, npass, one_pass, 0)

    @pl.when(e == pl.num_programs(2) - 1)
    def _():
        y_o[...] = _layer_norm(DEEPNORM_ALPHA * h_ref[...] + y_o[...], ln3g_ref[...], ln3b_ref[...])


def _moe_routed(h2, gt, ew, tm, ts, eb, rb):
    bsz, length, _ = h2.shape
    consts = [ew['ws_gu'], ew['ws_down'], ew['ln3_g'], ew['ln3_b']]
    return pl.pallas_call(
        functools.partial(_moe_routed_kernel, eb=eb, rb=rb, ts=ts),
        grid=(bsz, length // tm, N_EXPERTS // eb),
        in_specs=[pl.BlockSpec((None, tm, D_MODEL), lambda b, t, e: (b, t, 0), pipeline_mode=pl.Buffered(1)),
                  pl.BlockSpec((None, N_EXPERTS, tm), lambda b, t, e: (b, 0, t)),
                  pl.BlockSpec((eb, D_MODEL, 2 * EXPERT_FF), lambda b, t, e: (e, 0, 0)),
                  pl.BlockSpec((eb, EXPERT_FF, D_MODEL), lambda b, t, e: (e, 0, 0))]
                 + [_const_spec(c.shape) for c in consts],
        out_specs=pl.BlockSpec((None, tm, D_MODEL), lambda b, t, e: (b, t, 0)),
        out_shape=jax.ShapeDtypeStruct((bsz, length, D_MODEL), F32),
        scratch_shapes=[pltpu.VMEM((tm, D_MODEL), BF16), pltpu.VMEM((N_EXPERTS, tm), F32),
                        pltpu.VMEM((tm // ts, eb * rb, ts), BF16), pltpu.VMEM((tm // ts, eb * rb, D_MODEL), BF16)],
        compiler_params=pltpu.CompilerParams(dimension_semantics=("arbitrary", "arbitrary", "arbitrary"),
                                             vmem_limit_bytes=VMEM_LIMIT_LARGE_BYTES),
        name="moe_routed",
    )(h2, gt, ew['w_gu'], ew['w_down'], *consts)


def _prep_post_weights(ln_in_g, ln_in_b, w_proj_gdn, w_proj_mla, w_out, ln1_g, ln1_b, w_mq, w_mo, ln2_g, ln2_b,
                       w_router, router_bias):
    def row(v):
        return v.reshape(1, -1).astype(F32)

    wrt = w_router.T.reshape(N_GROUPS, GROUP_SIZE, D_MODEL).transpose(1, 0, 2).reshape(N_EXPERTS, D_MODEL)
    rb = router_bias.reshape(N_GROUPS, GROUP_SIZE).T.reshape(N_EXPERTS, 1)
    return dict(ln_in_g=row(ln_in_g), ln_in_b=row(ln_in_b), w_proj_gdn=w_proj_gdn.astype(BF16),
                w_proj_mla=w_proj_mla.astype(BF16), w_out=w_out.astype(BF16), ln1_g=row(ln1_g), ln1_b=row(ln1_b),
                w_mq=w_mq.astype(BF16), w_mo=w_mo.astype(BF16), ln2_g=row(ln2_g), ln2_b=row(ln2_b),
                w_router_t=wrt.astype(F32), router_bias=rb.astype(F32))


def _prep_expert_weights(w_gate, w_up, w_down, ws_gate, ws_up, ws_down, ln3_g, ln3_b):
    return dict(w_gu=jnp.concatenate([w_gate, w_up], axis=-1).astype(BF16), w_down=w_down.astype(BF16),
                ws_gu=jnp.concatenate([ws_gate, ws_up], axis=-1).astype(BF16), ws_down=ws_down.astype(BF16),
                ln3_g=ln3_g.reshape(1, -1).astype(F32), ln3_b=ln3_b.reshape(1, -1).astype(F32))


def _tail(x, og, om, gate, mem_k, mem_v, pw, ew, td):
    bsz, length, _ = x.shape
    h2, gt = _postmix(x, og, om, gate, mem_k, mem_v, pw, td)
    if length % MOE_ROUTED_TILE == 0:
        return _moe_routed(h2, gt, ew, MOE_ROUTED_TILE, MOE_ROUTED_SUB, MOE_ROUTED_EXPERTS, MOE_ROUTED_SLOTS)
    gates = gt.transpose(0, 2, 1).reshape(bsz * length, N_EXPERTS)
    y = _moe(h2.reshape(bsz * length, D_MODEL), gates, ew, bsz * length, 2)
    return y.reshape(bsz, length, D_MODEL)


def kernel(x_prompt, x_sample, mem_prompt, state_gdn_conv, state_gdn_ssm, cache_mla_ckv, cache_mla_krope,
           cache_mem_k, cache_mem_v, ln_in_g, ln_in_b, w_in, w_conv, a_log, dt_bias, gdn_norm_g, w_proj_gdn,
           q_norm_g, w_uq, kv_norm_g, w_ukv, w_proj_mla, w_out, ln1_g, ln1_b, w_mq, w_mk, w_mv, w_mo,
           ln2_g, ln2_b, w_router, router_bias, w_gate, w_up, w_down, ws_gate, ws_up, ws_down, ln3_g, ln3_b):
    lyr = 0
    bsz, seq, _ = x_prompt.shape
    dbsz, dseq, _ = x_sample.shape
    past = cache_mla_ckv.shape[2]
    mw = _prep_mixer_weights(w_in[lyr], w_conv[lyr], a_log[lyr], dt_bias[lyr], q_norm_g[lyr], w_uq[lyr],
                             kv_norm_g[lyr], w_ukv[lyr])
    pw = _prep_post_weights(ln_in_g, ln_in_b, w_proj_gdn[lyr], w_proj_mla[lyr], w_out[lyr], ln1_g[lyr], ln1_b[lyr],
                            w_mq[lyr], w_mo[lyr], ln2_g[lyr], ln2_b[lyr], w_router[lyr], router_bias[lyr])
    ew = _prep_expert_weights(w_gate[lyr], w_up[lyr], w_down[lyr], ws_gate[lyr], ws_up[lyr], ws_down[lyr],
                              ln3_g[lyr], ln3_b[lyr])
    wide = MEM_HEADS * MEM_HEAD_DIM

    conv0 = jnp.zeros((bsz, GDN_CONV - 1, GDN_CONV_CH), F32)
    s0 = jnp.zeros((bsz, GDN_HEADS, GDN_DK, GDN_DV), F32)
    (qkv, z, gate, qp, kn, krp, v, ckv_p, kr_p, conv_p, gbc, gbr) = _inproj(
        x_prompt, ln_in_g, ln_in_b, mw, conv0, 0, tl=min(256, seq))
    og, ssm_p = _gdn(qkv, gbc, gbr, z, s0, gdn_norm_g[lyr], chunk=CHUNK, tg=min(256, seq), nb=math.gcd(bsz, 4))
    om = _flash_attention(qp, kn, krp, v, tq=min(1024, seq), nh=2)
    mem_k, mem_v = _memproj(mem_prompt, w_mk[lyr], w_mv[lyr])
    y_p = _tail(x_prompt, og, om, gate, mem_k, mem_v, pw, ew, td=min(1024, seq))

    (qkv, z, gate, qp, kn, krp, v, ckv_s, kr_s, conv_s, gbc, gbr) = _inproj(
        x_sample, ln_in_g, ln_in_b, mw, state_gdn_conv[lyr], past, tl=dseq)
    chunk_s = CHUNK if dseq % CHUNK == 0 else dseq
    og, ssm_s = _gdn(qkv, gbc, gbr, z, state_gdn_ssm[lyr], gdn_norm_g[lyr], chunk=chunk_s, tg=dseq, nb=math.gcd(dbsz, 2))
    om = _cached_attention(qp, cache_mla_ckv[lyr], cache_mla_krope[lyr], kn, krp, v, mw['w_kv'])
    y_s = _tail(x_sample, og, om, gate, cache_mem_k[lyr].reshape(dbsz, -1, wide),
                cache_mem_v[lyr].reshape(dbsz, -1, wide), pw, ew, td=dseq)

    return (y_p, y_s, conv_p[None], ssm_p[None], ckv_p[None], kr_p[None],
            mem_k.reshape(1, bsz, -1, MEM_HEADS, MEM_HEAD_DIM), mem_v.reshape(1, bsz, -1, MEM_HEADS, MEM_HEAD_DIM),
            conv_s[None], ssm_s[None], ckv_s[None], kr_s[None])
```

```python
import functools
import math

import jax
import jax.numpy as jnp
import numpy as np
from jax import lax
from jax.experimental import pallas as pl
from jax.experimental.pallas import tpu as pltpu

F32 = jnp.float32
BF16 = jnp.bfloat16

D_MODEL = 1024
CHUNK = 64
GDN_HEADS = 8
GDN_DK = 128
GDN_DV = 128
GDN_CONV = 4
GDN_QK = GDN_HEADS * GDN_DK
GDN_CONV_CH = 2 * GDN_QK + GDN_HEADS * GDN_DV
MLA_HEADS = 8
MLA_Q_LORA = 384
MLA_KV_LORA = 256
MLA_NOPE = 128
MLA_ROPE = 64
MLA_V = 128
MLA_SCALE = (MLA_NOPE + MLA_ROPE) ** -0.5
MLA_SCALE_LOG2 = MLA_SCALE * math.log2(math.e)
MLA_QK_PAD = 256
ROPE_THETA = 10000.0
MEM_HEADS = 4
MEM_HEAD_DIM = 128
N_EXPERTS = 64
TOP_K = 8
N_GROUPS = 8
GROUP_SIZE = N_EXPERTS // N_GROUPS
TOPK_GROUPS = 4
EXPERT_FF = 256
SHARED_FF = 256
ROUTED_SCALE = 2.5
DEPTH = 1
DEEPNORM_ALPHA = (2.0 * DEPTH) ** 0.25
IN_WIDTHS = (GDN_CONV_CH, GDN_HEADS * GDN_DV, GDN_HEADS, GDN_HEADS, MLA_Q_LORA, MLA_KV_LORA, MLA_ROPE,
             D_MODEL, D_MODEL)

VMEM_LIMIT_BYTES = 56 * 1024 * 1024
VMEM_LIMIT_LARGE_BYTES = 60 * 1024 * 1024


def _dot(a, b, precision=None):
    return jnp.dot(a, b, preferred_element_type=F32, precision=precision)


def _dot_nt(a, b, precision=None):
    return lax.dot_general(a, b, (((1,), (1,)), ((), ())), preferred_element_type=F32, precision=precision)


def _dot_tn(a, b):
    return lax.dot_general(a, b, (((0,), (0,)), ((), ())), preferred_element_type=F32)


def _layer_norm(x, g, b, eps=1e-5):
    mu = jnp.mean(x, axis=-1, keepdims=True)
    xc = x - mu
    var = jnp.mean(xc * xc, axis=-1, keepdims=True)
    return xc * lax.rsqrt(var + eps) * g + b


def _rms_norm(x, g, eps=1e-6):
    return x * lax.rsqrt(jnp.mean(x * x, axis=-1, keepdims=True) + eps) * g


def _sigmoid(x):
    return 1.0 / (1.0 + jnp.exp(-x))


def _silu(x):
    return x * _sigmoid(x)


def _softplus(x):
    return jnp.maximum(x, 0.0) + jnp.log1p(jnp.exp(-jnp.abs(x)))


def _const_spec(shape):
    nd = len(shape)
    return pl.BlockSpec(shape, lambda *_: (0,) * nd, pipeline_mode=pl.Buffered(1))


def _inproj_kernel(x_ref, lng_ref, lnb_ref, wqkv_ref, wz_ref, wgate_ref, wcq_ref, wckv_ref, wkr_ref, wab_ref,
                   wabt_ref, convw_ref, conv0_ref, arow_ref, acol_ref, qng_ref, wuq_ref, wuqs_ref, kvng_ref,
                   wuk_ref, wuvt_ref, t1_ref, t2_ref,
                   qkv_o, z_o, gate_o, qp_o, kn_o, krp_o, vt_o, ckv_o, kr_o, conv_o, gbc_o, gbr_o,
                   cbuf):
    tl = x_ref.shape[0]
    li = pl.program_id(1)

    h = _layer_norm(x_ref[...], lng_ref[...], lnb_ref[...])
    hb = h.astype(BF16)

    @pl.when(li == 0)
    def _():
        cbuf[0:8, :] = conv0_ref[...]

    cbuf[8:8 + tl, :] = _dot(hb, wqkv_ref[...])
    conv_o[...] = cbuf[tl + 5:tl + 8, :]
    for part in range(3):
        cols = slice(part * GDN_QK, (part + 1) * GDN_QK)
        y = convw_ref[0:1, cols] * cbuf[5:5 + tl, cols]
        for i in range(1, GDN_CONV):
            y = y + convw_ref[i:i + 1, cols] * cbuf[5 + i:5 + i + tl, cols]
        y = _silu(y)
        if part < 2:
            for hh in range(GDN_HEADS):
                hc = slice(hh * GDN_DK, (hh + 1) * GDN_DK)
                yh = y[:, hc]
                r = lax.rsqrt(jnp.sum(yh * yh, axis=-1, keepdims=True) + 1e-6)
                yh = yh * r
                if part == 0:
                    yh = yh * (GDN_DK ** -0.5)
                qkv_o[:, part * GDN_QK + hh * GDN_DK:part * GDN_QK + (hh + 1) * GDN_DK] = yh.astype(BF16)
        else:
            qkv_o[:, cols] = y.astype(BF16)
    cbuf[0:8, :] = cbuf[tl:tl + 8, :]

    z_o[...] = _dot(hb, wz_ref[...]).astype(BF16)
    gate_o[...] = _dot(hb, wgate_ref[...]).astype(BF16)

    ab = _dot(hb, wab_ref[...])
    arow = arow_ref[...]
    g_c = arow[0:1, :] * _softplus(ab[:, 0:GDN_HEADS] + arow[1:2, :])
    b_c = _sigmoid(ab[:, GDN_HEADS:2 * GDN_HEADS])
    gbc_o[:, 0:GDN_HEADS] = g_c
    gbc_o[:, GDN_HEADS:2 * GDN_HEADS] = b_c
    abt = _dot_nt(wabt_ref[...], hb)
    acol = acol_ref[...]
    gbr_o[0:GDN_HEADS, :] = acol[:, 0:1] * _softplus(abt[0:GDN_HEADS, :] + acol[:, 1:2])
    gbr_o[GDN_HEADS:2 * GDN_HEADS, :] = _sigmoid(abt[GDN_HEADS:2 * GDN_HEADS, :])

    t1 = t1_ref[...]
    t2 = t2_ref[...]

    cq = _rms_norm(_dot(hb, wcq_ref[...]), qng_ref[...]).astype(BF16)
    qm = _dot(cq, wuq_ref[...])
    qs = _dot(cq, wuqs_ref[...])
    for hh in range(MLA_HEADS):
        base = hh * MLA_QK_PAD
        qp_o[:, base:base + MLA_NOPE] = (qm[:, base:base + MLA_NOPE] * MLA_SCALE_LOG2).astype(BF16)
        hi = qm[:, base + MLA_NOPE:base + MLA_QK_PAD] * t1 + qs[:, hh * 128:(hh + 1) * 128] * t2
        qp_o[:, base + MLA_NOPE:base + MLA_QK_PAD] = (hi * MLA_SCALE_LOG2).astype(BF16)

    ckv = _rms_norm(_dot(hb, wckv_ref[...]), kvng_ref[...])
    ckv_o[...] = ckv
    ckvb = ckv.astype(BF16)
    kn_o[...] = _dot(ckvb, wuk_ref[...]).astype(BF16)
    vt_o[...] = _dot_nt(wuvt_ref[...], ckvb).astype(BF16)
    krm = _dot(hb, wkr_ref[...])
    krp = krm[:, 0:128] * t1 + krm[:, 128:256] * t2
    kr_o[...] = krp[:, 0:MLA_ROPE]
    krp_o[...] = krp.astype(BF16)


def _swap_halves(w):
    half = w.shape[-1] // 2
    return jnp.concatenate([w[..., half:], w[..., :half]], axis=-1)


def _rope_tables(pos0, length):
    inv_freq = np.power(ROPE_THETA, -np.arange(0, MLA_ROPE, 2, dtype=np.float64) / MLA_ROPE)
    ang = (pos0 + np.arange(length)).astype(np.float64)[:, None] * inv_freq[None, :]
    c, s = np.cos(ang).astype(np.float32), np.sin(ang).astype(np.float32)
    zeros = np.zeros((length, 128 - MLA_ROPE), np.float32)
    return (jnp.asarray(np.concatenate([c, c, zeros], axis=-1)),
            jnp.asarray(np.concatenate([-s, s, zeros], axis=-1)))


def _prep_mixer_weights(w_in, w_conv, a_log, dt_bias, q_norm_g, w_uq, kv_norm_g, w_ukv):
    splits = []
    off = 0
    for wd in IN_WIDTHS:
        splits.append(w_in[:, off:off + wd])
        off += wd
    w_qkv, w_z, w_a, w_b, w_cq, w_ckv, w_kr, w_gg, w_gm = splits
    z64 = jnp.zeros((D_MODEL, 128 - MLA_ROPE), F32)
    w_krg = jnp.concatenate([w_kr, z64, _swap_halves(w_kr), z64], axis=-1)
    w_ab = jnp.concatenate([w_a, w_b], axis=-1)
    wq = w_uq.reshape(MLA_Q_LORA, MLA_HEADS, MLA_NOPE + MLA_ROPE)
    nope, rope = wq[:, :, :MLA_NOPE], wq[:, :, MLA_NOPE:]
    zq = jnp.zeros((MLA_Q_LORA, MLA_HEADS, MLA_QK_PAD - MLA_NOPE - MLA_ROPE), F32)
    wq_main = jnp.concatenate([nope, rope, zq], axis=-1).reshape(MLA_Q_LORA, MLA_HEADS * MLA_QK_PAD)
    wq_swap = jnp.concatenate([_swap_halves(rope), zq], axis=-1).reshape(MLA_Q_LORA, MLA_HEADS * 128)
    wkv = w_ukv.reshape(MLA_KV_LORA, MLA_HEADS, MLA_NOPE + MLA_V)
    w_kv = jnp.concatenate([wkv[:, :, :MLA_NOPE].reshape(MLA_KV_LORA, -1),
                            wkv[:, :, MLA_NOPE:].reshape(MLA_KV_LORA, -1)], axis=-1)
    neg_a = -jnp.exp(a_log.astype(F32))
    arow = jnp.stack([neg_a, dt_bias.astype(F32)], axis=0)
    return dict(
        w_qkv=w_qkv.astype(BF16), w_z=w_z.astype(BF16),
        w_gate=jnp.concatenate([w_gg, w_gm], axis=-1).astype(BF16),
        w_cq=w_cq.astype(BF16), w_ckv=w_ckv.astype(BF16), w_krg=w_krg.astype(BF16),
        w_ab=w_ab.astype(BF16), w_abt=w_ab.T.astype(BF16),
        w_conv=w_conv.astype(F32), arow=arow, acol=arow.T,
        q_norm_g=q_norm_g.reshape(1, -1).astype(F32), wq_main=wq_main.astype(BF16), wq_swap=wq_swap.astype(BF16),
        kv_norm_g=kv_norm_g.reshape(1, -1).astype(F32), w_kv=w_kv.astype(BF16),
        w_k=wkv[:, :, :MLA_NOPE].reshape(MLA_KV_LORA, -1).astype(BF16),
        w_vt=wkv[:, :, MLA_NOPE:].reshape(MLA_KV_LORA, -1).T.astype(BF16))


def _inproj(x, ln_g, ln_b, mw, conv_state, pos0, tl):
    bsz, length, _ = x.shape
    nl = length // tl
    t1, t2 = _rope_tables(pos0, length)
    conv0 = jnp.concatenate([jnp.zeros((bsz, 8 - (GDN_CONV - 1), GDN_CONV_CH), F32), conv_state.astype(F32)], axis=1)

    def rows(width, dtype):
        return (jax.ShapeDtypeStruct((bsz, length, width), dtype),
                pl.BlockSpec((None, tl, width), lambda b, l: (b, l, 0)))

    outs = [rows(GDN_CONV_CH, BF16), rows(GDN_HEADS * GDN_DV, BF16), rows(2 * D_MODEL, BF16),
            rows(MLA_HEADS * MLA_QK_PAD, BF16), rows(MLA_HEADS * MLA_NOPE, BF16), rows(128, BF16),
            (jax.ShapeDtypeStruct((bsz, MLA_HEADS * MLA_V, length), BF16),
             pl.BlockSpec((None, MLA_HEADS * MLA_V, tl), lambda b, l: (b, 0, l))),
            rows(MLA_KV_LORA, F32), rows(MLA_ROPE, F32),
            (jax.ShapeDtypeStruct((bsz, GDN_CONV - 1, GDN_CONV_CH), F32),
             pl.BlockSpec((None, GDN_CONV - 1, GDN_CONV_CH), lambda b, l: (b, 0, 0))),
            rows(2 * GDN_HEADS, F32),
            (jax.ShapeDtypeStruct((bsz, 2 * GDN_HEADS, length), F32),
             pl.BlockSpec((None, 2 * GDN_HEADS, tl), lambda b, l: (b, 0, l)))]
    consts = [ln_g.reshape(1, -1), ln_b.reshape(1, -1), mw['w_qkv'], mw['w_z'], mw['w_gate'], mw['w_cq'],
              mw['w_ckv'], mw['w_krg'], mw['w_ab'], mw['w_abt'], mw['w_conv']]
    consts2 = [mw['arow'], mw['acol'], mw['q_norm_g'], mw['wq_main'], mw['wq_swap'], mw['kv_norm_g'], mw['w_k'],
               mw['w_vt']]
    in_specs = ([pl.BlockSpec((None, tl, D_MODEL), lambda b, l: (b, l, 0))]
                + [_const_spec(c.shape) for c in consts]
                + [pl.BlockSpec((None, 8, GDN_CONV_CH), lambda b, l: (b, 0, 0))]
                + [_const_spec(c.shape) for c in consts2]
                + [pl.BlockSpec((tl, 128), lambda b, l: (l, 0)), pl.BlockSpec((tl, 128), lambda b, l: (l, 0))])
    return pl.pallas_call(
        _inproj_kernel,
        grid=(bsz, nl),
        in_specs=in_specs,
        out_specs=[o[1] for o in outs],
        out_shape=[o[0] for o in outs],
        scratch_shapes=[pltpu.VMEM((tl + 8, GDN_CONV_CH), F32)],
        compiler_params=pltpu.CompilerParams(dimension_semantics=("arbitrary", "arbitrary"),
                                             vmem_limit_bytes=VMEM_LIMIT_BYTES),
        name="inproj",
    )(x, *consts, conv0, *consts2, t1, t2)


def _gdn_kernel(qkv_ref, gbc_ref, gbr_ref, z_ref, s0_ref, ng_ref, og_o, s_o, s_scr, *, chunk):
    tg = qkv_ref.shape[1]
    nchunks = tg // chunk
    li = pl.program_id(1)

    @pl.when(li == 0)
    def _():
        s_scr[...] = s0_ref[...]

    ri = lax.broadcasted_iota(jnp.int32, (chunk, chunk), 0)
    ci = lax.broadcasted_iota(jnp.int32, (chunk, chunk), 1)
    incl = ri >= ci
    strict = ri > ci
    tri = jnp.where(incl, 1.0, 0.0).astype(F32)
    eye = jnp.where(ri == ci, 1.0, 0.0).astype(F32)
    ng = ng_ref[...]
    n_sq = int(math.log2(chunk)) - 1

    nb = qkv_ref.shape[0]
    chains = [(g, hh) for g in range(nb) for hh in range(GDN_HEADS)]

    def chunk_body(c, carry):
        r0 = pl.multiple_of(c * chunk, chunk)
        rows = pl.ds(r0, chunk)
        gbc = [gbc_ref[g, rows, :] for g in range(nb)]
        gc_col = [_dot(tri, gbc[g], precision=lax.Precision.HIGHEST) for g in range(nb)]
        gc_row = [_dot_nt(gbr_ref[g, c], tri, precision=lax.Precision.HIGHEST) for g in range(nb)]
        q, k, qkk = {}, {}, {}
        for ch in chains:
            g, hh = ch
            q[ch] = qkv_ref[g, rows, hh * GDN_DK:(hh + 1) * GDN_DK]
            k[ch] = qkv_ref[g, rows, GDN_QK + hh * GDN_DK:GDN_QK + (hh + 1) * GDN_DK]
            qkk[ch] = _dot_nt(jnp.concatenate([q[ch], k[ch]], axis=0), k[ch])
        gcc, beta, gamma, g_end, qk, xp, t_inv = {}, {}, {}, {}, {}, {}, {}
        for ch in chains:
            g, hh = ch
            gcc[ch] = gc_col[g][:, hh:hh + 1]
            gcr = gc_row[g][hh:hh + 1, :]
            beta[ch] = gbc[g][:, GDN_HEADS + hh:GDN_HEADS + hh + 1]
            g_end[ch] = gc_col[g][chunk - 1:chunk, hh:hh + 1]
            gamma[ch] = jnp.exp(gcc[ch])
            decay = jnp.where(incl, jnp.exp(jnp.where(incl, gcc[ch] - gcr, 0.0)), 0.0)
            qk[ch] = (decay * qkk[ch][0:chunk, :]).astype(BF16)
            xp[ch] = jnp.where(strict, beta[ch] * decay * qkk[ch][chunk:2 * chunk, :], 0.0)
            t_inv[ch] = eye - xp[ch]
        for _ in range(n_sq):
            for ch in chains:
                xb = xp[ch].astype(BF16)
                xp[ch] = _dot(xb, xb)
            for ch in chains:
                t_inv[ch] = t_inv[ch] + _dot(t_inv[ch].astype(BF16), xp[ch].astype(BF16))
        sol, q_s, sb, s_old, k_dec = {}, {}, {}, {}, {}
        for ch in chains:
            g, hh = ch
            kf = k[ch].astype(F32)
            v = qkv_ref[g, rows, 2 * GDN_QK + hh * GDN_DV:2 * GDN_QK + (hh + 1) * GDN_DV].astype(F32)
            rhs = jnp.concatenate([beta[ch] * v, (beta[ch] * gamma[ch]) * kf], axis=-1).astype(BF16)
            sol[ch] = _dot(t_inv[ch].astype(BF16), rhs)
            k_dec[ch] = (jnp.exp(g_end[ch] - gcc[ch]) * kf).astype(BF16)
            s_old[ch] = s_scr[g, hh]
            sb[ch] = s_old[ch].astype(BF16)
            q_s[ch] = _dot((gamma[ch] * q[ch].astype(F32)).astype(BF16), sb[ch])
        ub = {}
        for ch in chains:
            u = sol[ch][:, :GDN_DV] - _dot(sol[ch][:, GDN_DV:].astype(BF16), sb[ch])
            ub[ch] = u.astype(BF16)
        for ch in chains:
            g, hh = ch
            o = q_s[ch] + _dot(qk[ch], ub[ch])
            s_scr[g, hh] = jnp.exp(g_end[ch]) * s_old[ch] + _dot_tn(k_dec[ch], ub[ch])
            zz = z_ref[g, rows, hh * GDN_DV:(hh + 1) * GDN_DV].astype(F32)
            og_o[g, rows, hh * GDN_DV:(hh + 1) * GDN_DV] = (_rms_norm(o, ng) * _silu(zz)).astype(BF16)
        return carry

    lax.fori_loop(0, nchunks, chunk_body, 0)
    s_o[...] = s_scr[...]


def _gdn(qkv, gbc, gbr, z, s0, norm_g, chunk, tg, nb):
    bsz, length, _ = qkv.shape
    nl = length // tg
    st_shape = (GDN_HEADS, GDN_DK, GDN_DV)
    gbr = gbr.reshape(bsz, 2 * GDN_HEADS, length // chunk, chunk).transpose(0, 2, 1, 3)
    return pl.pallas_call(
        functools.partial(_gdn_kernel, chunk=chunk),
        grid=(bsz // nb, nl),
        in_specs=[pl.BlockSpec((nb, tg, GDN_CONV_CH), lambda b, l: (b, l, 0)),
                  pl.BlockSpec((nb, tg, 2 * GDN_HEADS), lambda b, l: (b, l, 0)),
                  pl.BlockSpec((nb, tg // chunk, 2 * GDN_HEADS, chunk), lambda b, l: (b, l, 0, 0)),
                  pl.BlockSpec((nb, tg, GDN_HEADS * GDN_DV), lambda b, l: (b, l, 0)),
                  pl.BlockSpec((nb,) + st_shape, lambda b, l: (b, 0, 0, 0)),
                  _const_spec((1, GDN_DV))],
        out_specs=[pl.BlockSpec((nb, tg, GDN_HEADS * GDN_DV), lambda b, l: (b, l, 0)),
                   pl.BlockSpec((nb,) + st_shape, lambda b, l: (b, 0, 0, 0))],
        out_shape=[jax.ShapeDtypeStruct((bsz, length, GDN_HEADS * GDN_DV), BF16),
                   jax.ShapeDtypeStruct((bsz,) + st_shape, F32)],
        scratch_shapes=[pltpu.VMEM((nb,) + st_shape, F32)],
        compiler_params=pltpu.CompilerParams(dimension_semantics=("arbitrary", "arbitrary"),
                                             vmem_limit_bytes=VMEM_LIMIT_BYTES),
        name="gdn",
    )(qkv, gbc, gbr, z, s0.astype(F32), norm_g.reshape(1, -1).astype(F32))


def _flash_kernel(q_ref, kn_ref, krp_ref, vt_ref, o_ref, kcat, *, tq, nh):
    qi = pl.program_id(2)

    @pl.when(qi == 0)
    def _():
        for j in range(nh):
            kcat[j, :, 0:MLA_NOPE] = kn_ref[:, j * MLA_NOPE:(j + 1) * MLA_NOPE]
            kcat[j, :, MLA_NOPE:MLA_QK_PAD] = krp_ref[...]

    qs = [q_ref[:, j * MLA_QK_PAD:(j + 1) * MLA_QK_PAD] for j in range(nh)]

    def tile(ki, carry, masked):
        rows = pl.ds(pl.multiple_of(ki * tq, tq), tq)
        ss = [_dot_nt(kcat[j, rows, :], qs[j]) for j in range(nh)]
        if masked:
            kc = lax.broadcasted_iota(jnp.int32, (tq, tq), 0) // CHUNK
            qc = lax.broadcasted_iota(jnp.int32, (tq, tq), 1) // CHUNK
            ss = [jnp.where(kc <= qc, s, -jnp.inf) for s in ss]
        out = []
        for j in range(nh):
            m, l, acc = carry[j]
            m_new = jnp.maximum(m, jnp.max(ss[j], axis=0, keepdims=True))
            p = jnp.exp2(ss[j] - m_new)
            alpha = jnp.exp2(m - m_new)
            l_new = alpha * l + jnp.sum(p, axis=0, keepdims=True)
            acc = alpha * acc + _dot(vt_ref[j * MLA_V:(j + 1) * MLA_V, rows], p.astype(BF16))
            out.append((m_new, l_new, acc))
        return tuple(out)

    init = tuple((jnp.full((1, tq), -jnp.inf, F32), jnp.zeros((1, tq), F32), jnp.zeros((MLA_V, tq), F32))
                 for _ in range(nh))
    carry = lax.fori_loop(0, qi, lambda ki, c: tile(ki, c, False), init)
    carry = tile(qi, carry, True)
    for j in range(nh):
        _, l, acc = carry[j]
        o_ref[:, j * MLA_V:(j + 1) * MLA_V] = (acc / l).T.astype(BF16)


def _flash_attention(qp, kn, krp, vt, tq, nh):
    bsz, length, _ = qp.shape
    nq = length // tq
    return pl.pallas_call(
        functools.partial(_flash_kernel, tq=tq, nh=nh),
        grid=(bsz, MLA_HEADS // nh, nq),
        in_specs=[pl.BlockSpec((None, tq, nh * MLA_QK_PAD), lambda b, h, i: (b, i, h)),
                  pl.BlockSpec((None, length, nh * MLA_NOPE), lambda b, h, i: (b, 0, h)),
                  pl.BlockSpec((None, length, 128), lambda b, h, i: (b, 0, 0)),
                  pl.BlockSpec((None, nh * MLA_V, length), lambda b, h, i: (b, h, 0))],
        out_specs=pl.BlockSpec((None, tq, nh * MLA_V), lambda b, h, i: (b, i, h)),
        out_shape=jax.ShapeDtypeStruct((bsz, length, MLA_HEADS * MLA_V), BF16),
        scratch_shapes=[pltpu.VMEM((nh, length, MLA_QK_PAD), BF16)],
        compiler_params=pltpu.CompilerParams(dimension_semantics=("arbitrary", "arbitrary", "arbitrary"),
                                             vmem_limit_bytes=VMEM_LIMIT_BYTES),
        name="mla_flash",
    )(qp, kn, krp, vt)


def _cached_attn_kernel(q_ref, cp_ref, krp_ref, knn_ref, krn_ref, vtn_ref, wkv_ref, o_ref, *, past):
    lq = q_ref.shape[0]
    hq = MLA_HEADS * lq
    wide = MLA_HEADS * MLA_NOPE
    q_chunk_p = (past + lax.rem(lax.broadcasted_iota(jnp.int32, (hq, past), 0), lq)) // CHUNK
    vis_p = lax.broadcasted_iota(jnp.int32, (hq, past), 1) // CHUNK <= q_chunk_p
    q_chunk_n = (past + lax.rem(lax.broadcasted_iota(jnp.int32, (hq, lq), 0), lq)) // CHUNK
    vis_n = (past + lax.broadcasted_iota(jnp.int32, (hq, lq), 1)) // CHUNK <= q_chunk_n
    ckv = cp_ref[...].astype(BF16)
    kr_past = krp_ref[...].astype(BF16)
    kr_new = krn_ref[...]
    q_lat, q_rope, s_new = [], [], []
    for hh in range(MLA_HEADS):
        base = hh * MLA_QK_PAD
        qn = q_ref[:, base:base + MLA_NOPE]
        qr = q_ref[:, base + MLA_NOPE:base + MLA_QK_PAD]
        hc = slice(hh * MLA_NOPE, (hh + 1) * MLA_NOPE)
        q_lat.append(_dot_nt(qn, wkv_ref[:, hc]).astype(BF16))
        q_rope.append(qr[:, 0:MLA_ROPE])
        s_new.append(_dot_nt(qn, knn_ref[:, hc]) + _dot_nt(qr, kr_new))
    s_p = _dot_nt(jnp.concatenate(q_lat, axis=0), ckv) + _dot_nt(jnp.concatenate(q_rope, axis=0), kr_past)
    s_p = jnp.where(vis_p, s_p, -jnp.inf)
    s_n = jnp.where(vis_n, jnp.concatenate(s_new, axis=0), -jnp.inf)
    m = jnp.maximum(jnp.max(s_p, axis=-1, keepdims=True), jnp.max(s_n, axis=-1, keepdims=True))
    p_p = jnp.exp2(s_p - m)
    p_n = jnp.exp2(s_n - m)
    l = jnp.sum(p_p, axis=-1, keepdims=True) + jnp.sum(p_n, axis=-1, keepdims=True)
    p_n = p_n.astype(BF16)
    o_lat = _dot(p_p.astype(BF16), ckv).astype(BF16)
    for hh in range(MLA_HEADS):
        rows = slice(hh * lq, (hh + 1) * lq)
        vc = slice(hh * MLA_V, (hh + 1) * MLA_V)
        o = _dot(o_lat[rows, :], wkv_ref[:, wide + hh * MLA_V:wide + (hh + 1) * MLA_V])
        o = o + _dot_nt(p_n[rows, :], vtn_ref[vc, :])
        o_ref[:, vc] = (o / l[rows, :]).astype(BF16)


def _cached_attention(qp, ckv_past, kr_past, kn_new, krp_new, vt_new, w_kv):
    bsz, lq, _ = qp.shape
    past = ckv_past.shape[1]
    wide = MLA_HEADS * MLA_NOPE

    def per_b(rows, width):
        return pl.BlockSpec((None, rows, width), lambda b: (b, 0, 0))

    return pl.pallas_call(
        functools.partial(_cached_attn_kernel, past=past),
        grid=(bsz,),
        in_specs=[per_b(lq, MLA_HEADS * MLA_QK_PAD), per_b(past, MLA_KV_LORA), per_b(past, MLA_ROPE),
                  per_b(lq, wide), per_b(lq, 128), per_b(wide, lq), _const_spec(w_kv.shape)],
        out_specs=per_b(lq, MLA_HEADS * MLA_V),
        out_shape=jax.ShapeDtypeStruct((bsz, lq, MLA_HEADS * MLA_V), BF16),
        compiler_params=pltpu.CompilerParams(dimension_semantics=("arbitrary",),
                                             vmem_limit_bytes=VMEM_LIMIT_BYTES),
        name="mla_cached",
    )(qp, ckv_past, kr_past, kn_new, krp_new, vt_new, w_kv)


def _memproj_kernel(m_ref, wk_ref, wv_ref, k_o, v_o):
    mb = m_ref[...].astype(BF16)
    k_o[...] = _dot(mb, wk_ref[...])
    v_o[...] = _dot(mb, wv_ref[...])


def _memproj(mem, w_mk, w_mv):
    bsz, tokens, _ = mem.shape
    wide = MEM_HEADS * MEM_HEAD_DIM
    return pl.pallas_call(
        _memproj_kernel,
        grid=(bsz,),
        in_specs=[pl.BlockSpec((None, tokens, D_MODEL), lambda b: (b, 0, 0)),
                  _const_spec(w_mk.shape), _const_spec(w_mv.shape)],
        out_specs=[pl.BlockSpec((None, tokens, wide), lambda b: (b, 0, 0)),
                   pl.BlockSpec((None, tokens, wide), lambda b: (b, 0, 0))],
        out_shape=[jax.ShapeDtypeStruct((bsz, tokens, wide), F32), jax.ShapeDtypeStruct((bsz, tokens, wide), F32)],
        compiler_params=pltpu.CompilerParams(dimension_semantics=("arbitrary",),
                                             vmem_limit_bytes=VMEM_LIMIT_BYTES),
        name="mem_proj",
    )(mem, w_mk.astype(BF16), w_mv.astype(BF16))


def _route(logits_t, bias_col):
    scores = _sigmoid(logits_t)
    sel = scores + bias_col
    t = logits_t.shape[1]
    sc = [scores[j * N_GROUPS:(j + 1) * N_GROUPS, :] for j in range(GROUP_SIZE)]
    se = [sel[j * N_GROUPS:(j + 1) * N_GROUPS, :] for j in range(GROUP_SIZE)]
    top1 = se[0]
    top2 = jnp.full_like(top1, -jnp.inf)
    for j in range(1, GROUP_SIZE):
        top2 = jnp.maximum(top2, jnp.minimum(top1, se[j]))
        top1 = jnp.maximum(top1, se[j])
    gs = top1 + top2
    gidx = lax.broadcasted_iota(jnp.int32, (N_GROUPS, t), 0)
    beaten = jnp.zeros((N_GROUPS, t), F32)
    for g2 in range(N_GROUPS):
        row = gs[g2:g2 + 1, :]
        tie = jnp.where(gidx > g2, 1.0, 0.0)
        beaten = beaten + jnp.where(row > gs, 1.0, jnp.where(row == gs, tie, 0.0))
    keep = beaten < TOPK_GROUPS
    ms = [jnp.where(keep, se[j], -jnp.inf) for j in range(GROUP_SIZE)]
    cnt = [jnp.zeros((N_GROUPS, t), F32) for _ in range(GROUP_SIZE)]
    for j2 in range(GROUP_SIZE):
        for g2 in range(N_GROUPS):
            row = ms[j2][g2:g2 + 1, :]
            for j in range(GROUP_SIZE):
                if j2 < j:
                    tie = jnp.where(gidx >= g2, 1.0, 0.0)
                else:
                    tie = jnp.where(gidx > g2, 1.0, 0.0)
                cnt[j] = cnt[j] + jnp.where(row > ms[j], 1.0, jnp.where(row == ms[j], tie, 0.0))
    w = [jnp.where(cnt[j] < TOP_K, sc[j], 0.0) for j in range(GROUP_SIZE)]
    tot = w[0]
    for j in range(1, GROUP_SIZE):
        tot = tot + w[j]
    denom = jnp.sum(tot, axis=0, keepdims=True) + 1e-20
    return [w[j] / denom * ROUTED_SCALE for j in range(GROUP_SIZE)]


def _postmix_kernel(x_ref, og_ref, om_ref, gate_ref, mk_ref, mv_ref, lng_ref, lnb_ref, wpg_ref, wpm_ref, wout_ref,
                    ln1g_ref, ln1b_ref, wmq_ref, wmo_ref, ln2g_ref, ln2b_ref, wrt_ref, rb_ref,
                    h2_o, gt_o, *, parts):
    pr = x_ref.shape[0] // parts
    rs = [slice(i * pr, (i + 1) * pr) for i in range(parts)]
    mk = mk_ref[...].astype(BF16)
    mv = mv_ref[...].astype(BF16)
    h = [_layer_norm(x_ref[r, :], lng_ref[...], lnb_ref[...]) for r in rs]
    bg = [_dot(og_ref[r, :], wpg_ref[...]) for r in rs]
    bm = [_dot(om_ref[r, :], wpm_ref[...]) for r in rs]
    merged = [(_sigmoid(gate_ref[r, 0:D_MODEL].astype(F32)) * bg[i]
               + _sigmoid(gate_ref[r, D_MODEL:2 * D_MODEL].astype(F32)) * bm[i]).astype(BF16)
              for i, r in enumerate(rs)]
    mix = [_dot(m, wout_ref[...]) for m in merged]
    h1 = [_layer_norm(DEEPNORM_ALPHA * h[i] + mix[i], ln1g_ref[...], ln1b_ref[...]) for i in range(parts)]
    q = [_dot(v.astype(BF16), wmq_ref[...]).astype(BF16) for v in h1]
    outs = [[] for _ in range(parts)]
    for hh in range(MEM_HEADS):
        hc = slice(hh * MEM_HEAD_DIM, (hh + 1) * MEM_HEAD_DIM)
        s = [_dot_nt(q[i][:, hc], mk[:, hc]) * (MEM_HEAD_DIM ** -0.5) for i in range(parts)]
        p = [jnp.exp(v - jnp.max(v, axis=-1, keepdims=True)) for v in s]
        p = [(v / jnp.sum(v, axis=-1, keepdims=True)).astype(BF16) for v in p]
        for i in range(parts):
            outs[i].append(_dot(p[i], mv[:, hc]).astype(BF16))
    o = [_dot(jnp.concatenate(v, axis=-1), wmo_ref[...]) for v in outs]
    h2 = [_layer_norm(DEEPNORM_ALPHA * h1[i] + o[i], ln2g_ref[...], ln2b_ref[...]) for i in range(parts)]
    logits_t = [_dot_nt(wrt_ref[...], v, precision=lax.Precision.HIGHEST) for v in h2]
    for i, r in enumerate(rs):
        h2_o[r, :] = h2[i]
        gates = _route(logits_t[i], rb_ref[...])
        for j in range(GROUP_SIZE):
            for g in range(N_GROUPS):
                gt_o[g * GROUP_SIZE + j:g * GROUP_SIZE + j + 1, r] = gates[j][g:g + 1, :]


def _postmix(x, og, om, gate, mem_k, mem_v, pw, td):
    bsz, length, _ = x.shape
    nl = length // td
    tokens = mem_k.shape[1]
    wide = MEM_HEADS * MEM_HEAD_DIM
    consts = [pw['ln_in_g'], pw['ln_in_b'], pw['w_proj_gdn'], pw['w_proj_mla'], pw['w_out'], pw['ln1_g'],
              pw['ln1_b'], pw['w_mq'], pw['w_mo'], pw['ln2_g'], pw['ln2_b'], pw['w_router_t'], pw['router_bias']]

    def rows(width):
        return pl.BlockSpec((None, td, width), lambda b, l: (b, l, 0))

    return pl.pallas_call(
        functools.partial(_postmix_kernel, parts=max(1, td // 256)),
        grid=(bsz, nl),
        in_specs=[rows(D_MODEL), rows(D_MODEL), rows(D_MODEL), rows(2 * D_MODEL),
                  pl.BlockSpec((None, tokens, wide), lambda b, l: (b, 0, 0)),
                  pl.BlockSpec((None, tokens, wide), lambda b, l: (b, 0, 0))]
                 + [_const_spec(c.shape) for c in consts],
        out_specs=[rows(D_MODEL), pl.BlockSpec((None, N_EXPERTS, td), lambda b, l: (b, 0, l))],
        out_shape=[jax.ShapeDtypeStruct((bsz, length, D_MODEL), F32),
                   jax.ShapeDtypeStruct((bsz, N_EXPERTS, length), F32)],
        compiler_params=pltpu.CompilerParams(dimension_semantics=("arbitrary", "arbitrary"),
                                             vmem_limit_bytes=VMEM_LIMIT_BYTES),
        name="postmix",
    )(x, og, om, gate, mem_k, mem_v, *consts)


def _moe_kernel(h_ref, g_ref, wg_ref, wu_ref, wd_ref, wsgu_ref, wsd_ref, ln3g_ref, ln3b_ref, y_o, xb, acc, *, eb):
    e = pl.program_id(1)

    @pl.when(e == 0)
    def _():
        x = h_ref[...].astype(BF16)
        xb[...] = x
        sgu = _dot(x, wsgu_ref[...])
        hid = _silu(sgu[:, 0:SHARED_FF]) * sgu[:, SHARED_FF:]
        acc[...] = _dot(hid.astype(BF16), wsd_ref[...])

    x = xb[...]
    gb = g_ref[...].astype(BF16)
    erow = lax.broadcasted_iota(jnp.int32, (N_EXPERTS, EXPERT_FF), 0)
    for i in range(eb):
        pick = jnp.where(erow == e * eb + i, 1.0, 0.0).astype(BF16)
        gcol = _dot(gb, pick)
        hid = _silu(_dot(x, wg_ref[i])) * _dot(x, wu_ref[i]) * gcol
        acc[...] += _dot(hid.astype(BF16), wd_ref[i])

    @pl.when(e == pl.num_programs(1) - 1)
    def _():
        y_o[...] = _layer_norm(DEEPNORM_ALPHA * h_ref[...] + acc[...], ln3g_ref[...], ln3b_ref[...])


def _moe(h2, gates, ew, tm, eb):
    n = h2.shape[0]
    consts = [ew['ws_gu'], ew['ws_down'], ew['ln3_g'], ew['ln3_b']]
    return pl.pallas_call(
        functools.partial(_moe_kernel, eb=eb),
        grid=(n // tm, N_EXPERTS // eb),
        in_specs=[pl.BlockSpec((tm, D_MODEL), lambda t, e: (t, 0)),
                  pl.BlockSpec((tm, N_EXPERTS), lambda t, e: (t, 0)),
                  pl.BlockSpec((eb, D_MODEL, EXPERT_FF), lambda t, e: (e, 0, 0)),
                  pl.BlockSpec((eb, D_MODEL, EXPERT_FF), lambda t, e: (e, 0, 0)),
                  pl.BlockSpec((eb, EXPERT_FF, D_MODEL), lambda t, e: (e, 0, 0))]
                 + [_const_spec(c.shape) for c in consts],
        out_specs=pl.BlockSpec((tm, D_MODEL), lambda t, e: (t, 0)),
        out_shape=jax.ShapeDtypeStruct((n, D_MODEL), F32),
        scratch_shapes=[pltpu.VMEM((tm, D_MODEL), BF16), pltpu.VMEM((tm, D_MODEL), F32)],
        compiler_params=pltpu.CompilerParams(dimension_semantics=("arbitrary", "arbitrary"),
                                             vmem_limit_bytes=VMEM_LIMIT_BYTES),
        name="moe",
    )(h2, gates, ew['w_gate'], ew['w_up'], ew['w_down'], *consts)


MOE_ROUTED_TILE = 1024
MOE_ROUTED_SUB = 512
MOE_ROUTED_EXPERTS = 8
MOE_ROUTED_SLOTS = 96


def _moe_routed_kernel(h_ref, g_ref, wg_ref, wu_ref, wd_ref, wsgu_ref, wsd_ref, ln3g_ref, ln3b_ref, y_o,
                       xb, rank_scr, p_scr, o_scr, *, eb, rb, ts):
    e = pl.program_id(2)
    t = h_ref.shape[0]
    ns = t // ts

    @pl.when(e == 0)
    def _():
        x = h_ref[...].astype(BF16)
        xb[...] = x
        sgu = _dot(x, wsgu_ref[...])
        hid = _silu(sgu[:, 0:SHARED_FF]) * sgu[:, SHARED_FF:]
        y_o[...] = _dot(hid.astype(BF16), wsd_ref[...])
        routed = jnp.where(g_ref[...] > 0.0, 1.0, 0.0).astype(BF16)
        upto = jnp.where(lax.broadcasted_iota(jnp.int32, (ts, ts), 0) <= lax.broadcasted_iota(jnp.int32, (ts, ts), 1),
                         1.0, 0.0).astype(BF16)
        for s in range(ns):
            rank_scr[:, s * ts:(s + 1) * ts] = _dot(routed[:, s * ts:(s + 1) * ts], upto)

    e0 = pl.multiple_of(e * eb, eb)
    gblk = g_ref[pl.ds(e0, eb), :]
    rblk = rank_scr[pl.ds(e0, eb), :]
    npass = (jnp.max(rblk).astype(jnp.int32) + (rb - 1)) // rb
    slot = lax.broadcasted_iota(jnp.int32, (rb, ts), 0).astype(F32)

    def one_pass(k, carry):
        base = (k * rb).astype(F32)
        xg, gate = [], []
        for i in range(eb):
            gi = gblk[i:i + 1, :]
            pos = jnp.where(gi > 0.0, rblk[i:i + 1, :] - 1.0 - base, -1.0)
            xgs, gates = [], []
            for s in range(ns):
                cs = slice(s * ts, (s + 1) * ts)
                hit = pos[:, cs] == slot
                pb = jnp.where(hit, 1.0, 0.0).astype(BF16)
                p_scr[s, i * rb:(i + 1) * rb, :] = pb
                gates.append(jnp.sum(jnp.where(hit, gi[:, cs], 0.0), axis=1, keepdims=True))
                xgs.append(_dot(pb, xb[cs, :]).astype(BF16))
            xg.append(jnp.concatenate(xgs, axis=0))
            gate.append(jnp.concatenate(gates, axis=0))
        hg = [_dot(xg[i], wg_ref[i]) for i in range(eb)]
        hu = [_dot(xg[i], wu_ref[i]) for i in range(eb)]
        hid = [(_silu(hg[i]) * hu[i] * gate[i]).astype(BF16) for i in range(eb)]
        for i in range(eb):
            out = _dot(hid[i], wd_ref[i]).astype(BF16)
            for s in range(ns):
                o_scr[s, i * rb:(i + 1) * rb, :] = out[s * rb:(s + 1) * rb, :]
        for s in range(ns):
            y_o[s * ts:(s + 1) * ts, :] += _dot_tn(p_scr[s], o_scr[s])
        return carry

    lax.fori_loop(0, npass, one_pass, 0)

    @pl.when(e == pl.num_programs(2) - 1)
    def _():
        y_o[...] = _layer_norm(DEEPNORM_ALPHA * h_ref[...] + y_o[...], ln3g_ref[...], ln3b_ref[...])


def _moe_routed(h2, gt, ew, tm, ts, eb, rb):
    bsz, length, _ = h2.shape
    consts = [ew['ws_gu'], ew['ws_down'], ew['ln3_g'], ew['ln3_b']]
    return pl.pallas_call(
        functools.partial(_moe_routed_kernel, eb=eb, rb=rb, ts=ts),
        grid=(bsz, length // tm, N_EXPERTS // eb),
        in_specs=[pl.BlockSpec((None, tm, D_MODEL), lambda b, t, e: (b, t, 0), pipeline_mode=pl.Buffered(1)),
                  pl.BlockSpec((None, N_EXPERTS, tm), lambda b, t, e: (b, 0, t)),
                  pl.BlockSpec((eb, D_MODEL, EXPERT_FF), lambda b, t, e: (e, 0, 0)),
                  pl.BlockSpec((eb, D_MODEL, EXPERT_FF), lambda b, t, e: (e, 0, 0)),
                  pl.BlockSpec((eb, EXPERT_FF, D_MODEL), lambda b, t, e: (e, 0, 0))]
                 + [_const_spec(c.shape) for c in consts],
        out_specs=pl.BlockSpec((None, tm, D_MODEL), lambda b, t, e: (b, t, 0)),
        out_shape=jax.ShapeDtypeStruct((bsz, length, D_MODEL), F32),
        scratch_shapes=[pltpu.VMEM((tm, D_MODEL), BF16), pltpu.VMEM((N_EXPERTS, tm), F32),
                        pltpu.VMEM((tm // ts, eb * rb, ts), BF16), pltpu.VMEM((tm // ts, eb * rb, D_MODEL), BF16)],
        compiler_params=pltpu.CompilerParams(dimension_semantics=("arbitrary", "arbitrary", "arbitrary"),
                                             vmem_limit_bytes=VMEM_LIMIT_LARGE_BYTES),
        name="moe_routed",
    )(h2, gt, ew['w_gate'], ew['w_up'], ew['w_down'], *consts)


def _prep_post_weights(ln_in_g, ln_in_b, w_proj_gdn, w_proj_mla, w_out, ln1_g, ln1_b, w_mq, w_mo, ln2_g, ln2_b,
                       w_router, router_bias):
    def row(v):
        return v.reshape(1, -1).astype(F32)

    wrt = w_router.T.reshape(N_GROUPS, GROUP_SIZE, D_MODEL).transpose(1, 0, 2).reshape(N_EXPERTS, D_MODEL)
    rb = router_bias.reshape(N_GROUPS, GROUP_SIZE).T.reshape(N_EXPERTS, 1)
    return dict(ln_in_g=row(ln_in_g), ln_in_b=row(ln_in_b), w_proj_gdn=w_proj_gdn.astype(BF16),
                w_proj_mla=w_proj_mla.astype(BF16), w_out=w_out.astype(BF16), ln1_g=row(ln1_g), ln1_b=row(ln1_b),
                w_mq=w_mq.astype(BF16), w_mo=w_mo.astype(BF16), ln2_g=row(ln2_g), ln2_b=row(ln2_b),
                w_router_t=wrt.astype(F32), router_bias=rb.astype(F32))


def _prep_expert_weights(w_gate, w_up, w_down, ws_gate, ws_up, ws_down, ln3_g, ln3_b):
    return dict(w_gate=w_gate.astype(BF16), w_up=w_up.astype(BF16), w_down=w_down.astype(BF16),
                ws_gu=jnp.concatenate([ws_gate, ws_up], axis=-1).astype(BF16), ws_down=ws_down.astype(BF16),
                ln3_g=ln3_g.reshape(1, -1).astype(F32), ln3_b=ln3_b.reshape(1, -1).astype(F32))


def _tail(x, og, om, gate, mem_k, mem_v, pw, ew, td):
    bsz, length, _ = x.shape
    h2, gt = _postmix(x, og, om, gate, mem_k, mem_v, pw, td)
    if length % MOE_ROUTED_TILE == 0:
        return _moe_routed(h2, gt, ew, MOE_ROUTED_TILE, MOE_ROUTED_SUB, MOE_ROUTED_EXPERTS, MOE_ROUTED_SLOTS)
    gates = gt.transpose(0, 2, 1).reshape(bsz * length, N_EXPERTS)
    y = _moe(h2.reshape(bsz * length, D_MODEL), gates, ew, bsz * length, 2)
    return y.reshape(bsz, length, D_MODEL)


def kernel(x_prompt, x_sample, mem_prompt, state_gdn_conv, state_gdn_ssm, cache_mla_ckv, cache_mla_krope,
           cache_mem_k, cache_mem_v, ln_in_g, ln_in_b, w_in, w_conv, a_log, dt_bias, gdn_norm_g, w_proj_gdn,
           q_norm_g, w_uq, kv_norm_g, w_ukv, w_proj_mla, w_out, ln1_g, ln1_b, w_mq, w_mk, w_mv, w_mo,
           ln2_g, ln2_b, w_router, router_bias, w_gate, w_up, w_down, ws_gate, ws_up, ws_down, ln3_g, ln3_b):
    lyr = 0
    bsz, seq, _ = x_prompt.shape
    dbsz, dseq, _ = x_sample.shape
    past = cache_mla_ckv.shape[2]
    mw = _prep_mixer_weights(w_in[lyr], w_conv[lyr], a_log[lyr], dt_bias[lyr], q_norm_g[lyr], w_uq[lyr],
                             kv_norm_g[lyr], w_ukv[lyr])
    pw = _prep_post_weights(ln_in_g, ln_in_b, w_proj_gdn[lyr], w_proj_mla[lyr], w_out[lyr], ln1_g[lyr], ln1_b[lyr],
                            w_mq[lyr], w_mo[lyr], ln2_g[lyr], ln2_b[lyr], w_router[lyr], router_bias[lyr])
    ew = _prep_expert_weights(w_gate[lyr], w_up[lyr], w_down[lyr], ws_gate[lyr], ws_up[lyr], ws_down[lyr],
                              ln3_g[lyr], ln3_b[lyr])
    wide = MEM_HEADS * MEM_HEAD_DIM

    conv0 = jnp.zeros((bsz, GDN_CONV - 1, GDN_CONV_CH), F32)
    s0 = jnp.zeros((bsz, GDN_HEADS, GDN_DK, GDN_DV), F32)
    (qkv, z, gate, qp, kn, krp, v, ckv_p, kr_p, conv_p, gbc, gbr) = _inproj(
        x_prompt, ln_in_g, ln_in_b, mw, conv0, 0, tl=min(256, seq))
    og, ssm_p = _gdn(qkv, gbc, gbr, z, s0, gdn_norm_g[lyr], chunk=CHUNK, tg=min(256, seq), nb=math.gcd(bsz, 4))
    om = _flash_attention(qp, kn, krp, v, tq=min(1024, seq), nh=2)
    mem_k, mem_v = _memproj(mem_prompt, w_mk[lyr], w_mv[lyr])
    y_p = _tail(x_prompt, og, om, gate, mem_k, mem_v, pw, ew, td=min(1024, seq))

    (qkv, z, gate, qp, kn, krp, v, ckv_s, kr_s, conv_s, gbc, gbr) = _inproj(
        x_sample, ln_in_g, ln_in_b, mw, state_gdn_conv[lyr], past, tl=dseq)
    chunk_s = CHUNK if dseq % CHUNK == 0 else dseq
    og, ssm_s = _gdn(qkv, gbc, gbr, z, state_gdn_ssm[lyr], gdn_norm_g[lyr], chunk=chunk_s, tg=dseq, nb=math.gcd(dbsz, 2))
    om = _cached_attention(qp, cache_mla_ckv[lyr], cache_mla_krope[lyr], kn, krp, v, mw['w_kv'])
    y_s = _tail(x_sample, og, om, gate, cache_mem_k[lyr].reshape(dbsz, -1, wide),
                cache_mem_v[lyr].reshape(dbsz, -1, wide), pw, ew, td=dseq)

    return (y_p, y_s, conv_p[None], ssm_p[None], ckv_p[None], kr_p[None],
            mem_k.reshape(1, bsz, -1, MEM_HEADS, MEM_HEAD_DIM), mem_v.reshape(1, bsz, -1, MEM_HEADS, MEM_HEAD_DIM),
            conv_s[None], ssm_s[None], ckv_s[None], kr_s[None])
```
